```python
import jax, jax.numpy as jnp
from jax import lax
import numpy as np

D_MODEL = 1024
BATCH = 8
SEQ = 2048
DEPTH = 1
DEC_BATCH = 128
DEC_SEQ = 8
PAST_LEN = 16384
PAGE_SIZE = 128

EPS = 1e-6
POOL_WINDOWS = (2, 4, 8, 16)
N_POOL_GROUPS = len(POOL_WINDOWS)
D_POOL = D_MODEL // 2
POOL_GROUP = D_POOL // N_POOL_GROUPS
POOL_BUF = max(POOL_WINDOWS) - 1
SSM_EXPAND = 2
D_INNER = SSM_EXPAND * D_MODEL
HEAD_DIM = 64
N_HEADS = D_INNER // HEAD_DIM
N_BC_GROUPS = 4
HEADS_PER_GROUP = N_HEADS // N_BC_GROUPS
D_STATE = 128
CONV_W = 4
D_XBC = D_INNER + 2 * N_BC_GROUPS * D_STATE
CHUNK = 128
N_BRANCH = 2
SPLITS = (D_POOL, 2 * D_POOL, 2 * D_POOL + D_INNER, 2 * D_POOL + D_INNER + D_XBC,
          2 * D_POOL + D_INNER + D_XBC + N_HEADS)
IN_COLS = 2 * D_POOL + D_INNER + D_XBC + N_HEADS + N_BRANCH * D_MODEL

kernel_name = "pool_ssd_gated_hybrid_step"


def rmsnorm(x, g):
    xf = x.astype(jnp.float32)
    y = xf * lax.rsqrt(jnp.mean(xf * xf, axis=-1, keepdims=True) + EPS)
    return (y * g.astype(jnp.float32)).astype(x.dtype)


def pool_mix(u, buf, pos0, pool_w, pool_scale):
    b, L, _ = u.shape
    xp = jnp.concatenate([buf.astype(u.dtype), u], axis=1)
    cs = jnp.pad(jnp.cumsum(xp.astype(jnp.float32), axis=1), ((0, 0), (1, 0), (0, 0)))
    pos = pos0 + jnp.arange(L)
    means = []
    for gi, w in enumerate(POOL_WINDOWS):
        sl = slice(gi * POOL_GROUP, (gi + 1) * POOL_GROUP)
        s = cs[:, POOL_BUF + 1:POOL_BUF + 1 + L, sl] - cs[:, POOL_BUF + 1 - w:POOL_BUF + 1 - w + L, sl]
        cnt = jnp.minimum(pos + 1, w).astype(jnp.float32)
        means.append(s / cnt[None, :, None])
    d = jnp.concatenate(means, axis=-1) - u.astype(jnp.float32)
    d = d.reshape(b, L, N_POOL_GROUPS, POOL_GROUP)
    y = jnp.einsum('blgc,gcd->blgd', d, pool_w.astype(jnp.float32)).reshape(b, L, D_POOL)
    y = y * pool_scale.astype(jnp.float32)
    return y.astype(u.dtype), xp[:, -POOL_BUF:]


def causal_conv(xbc, buf, w, bias):
    L = xbc.shape[1]
    xp = jnp.concatenate([buf.astype(xbc.dtype), xbc], axis=1)
    y = bias[None, None, :] + sum(xp[:, k:k + L] * w[k][None, None, :] for k in range(CONV_W))
    return y, xp[:, -(CONV_W - 1):]


def ssd(x, dt, A, B, C, state0):
    b, L = x.shape[:2]
    Q = CHUNK if L % CHUNK == 0 else L
    nc = L // Q

    def to_chunks(t):
        return jnp.moveaxis(t.reshape((b, nc, Q) + t.shape[2:]), 1, 0)

    xc = to_chunks(x.reshape(b, L, N_BC_GROUPS, HEADS_PER_GROUP, HEAD_DIM))
    dtc = to_chunks(dt.reshape(b, L, N_BC_GROUPS, HEADS_PER_GROUP))
    Bc, Cc = to_chunks(B), to_chunks(C)
    Ag = A.reshape(N_BC_GROUPS, HEADS_PER_GROUP)
    mask = jnp.tril(jnp.ones((Q, Q), dtype=bool))[None, :, :, None, None]

    def step(h, inp):
        xq, dq, Bq, Cq = inp
        acs = jnp.cumsum(dq * Ag, axis=1)
        seg = acs[:, :, None] - acs[:, None, :]
        decay = jnp.exp(jnp.where(mask, seg, -jnp.inf))
        cb = jnp.einsum('bign,bjgn->bijg', Cq, Bq)
        att = cb[..., None] * decay * dq[:, None]
        y = jnp.einsum('bijge,bjgep->bigep', att, xq)
        y = y + jnp.einsum('bign,bige,bgepn->bigep', Cq, jnp.exp(acs), h)
        last = acs[:, -1]
        w_in = jnp.exp(last[:, None] - acs) * dq
        h = h * jnp.exp(last)[..., None, None] + jnp.einsum('bjgn,bjge,bjgep->bgepn', Bq, w_in, xq)
        return h, y

    h0 = state0.astype(jnp.float32).reshape(b, N_BC_GROUPS, HEADS_PER_GROUP, HEAD_DIM, D_STATE)
    hT, ys = lax.scan(step, h0, (xc, dtc, Bc, Cc))
    y = jnp.moveaxis(ys, 0, 1).reshape(b, L, N_HEADS, HEAD_DIM)
    return y, hT.reshape(b, N_HEADS, HEAD_DIM, D_STATE)


def layer(x, c, pos0, pool_buf, conv_buf, ssm_state, w_ada, b_ada, norm_g, w_in, conv_w, conv_b,
          dt_bias, a_log, d_skip, ssm_norm_g, pool_w, pool_scale, w_pool_out, w_ssm_out, w_o):
    b, L, _ = x.shape
    mod = jax.nn.silu(c) @ w_ada + b_ada
    shift, scale, gate = jnp.split(mod, 3, axis=-1)
    h = rmsnorm(x, norm_g) * (1 + scale[:, None]) + shift[:, None]
    proj = h @ w_in
    u_pool, z_pool, z_ssm, xbc, dt_raw, g_raw = jnp.split(proj, SPLITS, axis=-1)
    p, pool_new = pool_mix(u_pool, pool_buf, pos0, pool_w, pool_scale)
    p = (p * jax.nn.silu(z_pool)) @ w_pool_out
    xbc_c, conv_new = causal_conv(xbc, conv_buf, conv_w, conv_b)
    xbc_c = jax.nn.silu(xbc_c).astype(jnp.float32)
    xs = xbc_c[..., :D_INNER].reshape(b, L, N_HEADS, HEAD_DIM)
    Bm = xbc_c[..., D_INNER:D_INNER + N_BC_GROUPS * D_STATE].reshape(b, L, N_BC_GROUPS, D_STATE)
    Cm = xbc_c[..., D_INNER + N_BC_GROUPS * D_STATE:].reshape(b, L, N_BC_GROUPS, D_STATE)
    dt = jax.nn.softplus(dt_raw.astype(jnp.float32) + dt_bias.astype(jnp.float32))
    A = -jnp.exp(a_log.astype(jnp.float32))
    y, ssm_new = ssd(xs, dt, A, Bm, Cm, ssm_state)
    y = (y + d_skip.astype(jnp.float32)[:, None] * xs).reshape(b, L, D_INNER)
    y = rmsnorm(y * jax.nn.silu(z_ssm.astype(jnp.float32)), ssm_norm_g)
    s = y.astype(x.dtype) @ w_ssm_out
    gates = jax.nn.sigmoid(g_raw)
    m = gates[..., :D_MODEL] * p + gates[..., D_MODEL:] * s
    x = x + gate[:, None] * (m @ w_o)
    return x, pool_new, conv_new, ssm_new


def setup_inputs(seed: int = 0) -> dict:
    key = jax.random.key(seed)
    ks = jax.random.split(key, 24)
    nrm = lambda k, shape, s: jax.random.normal(k, shape, jnp.float32) * s
    dt0 = jnp.exp(jax.random.uniform(ks[13], (DEPTH, N_HEADS), jnp.float32, np.log(1e-3), np.log(1e-1)))
    return {
        "x_prompt": nrm(ks[0], (BATCH, SEQ, D_MODEL), 1.0),
        "x_sample": nrm(ks[1], (DEC_BATCH, DEC_SEQ, D_MODEL), 1.0),
        "state_pool": nrm(ks[2], (DEPTH, DEC_BATCH, POOL_BUF, D_POOL), 1.0),
        "state_conv": nrm(ks[3], (DEPTH, DEC_BATCH, CONV_W - 1, D_XBC), 1.0),
        "state_ssm": nrm(ks[4], (DEPTH, DEC_BATCH, N_HEADS, HEAD_DIM, D_STATE), 0.5),
        "c_prompt": nrm(ks[5], (BATCH, D_MODEL), 1.0),
        "c_sample": nrm(ks[6], (DEC_BATCH, D_MODEL), 1.0),
        "w_ada": nrm(ks[7], (DEPTH, D_MODEL, 3 * D_MODEL), 0.5 * D_MODEL ** -0.5),
        "b_ada": nrm(ks[8], (DEPTH, 3 * D_MODEL), 0.02),
        "norm_g": 1.0 + nrm(ks[9], (DEPTH, D_MODEL), 0.05),
        "w_in": nrm(ks[10], (DEPTH, D_MODEL, IN_COLS), D_MODEL ** -0.5),
        "conv_w": nrm(ks[11], (DEPTH, CONV_W, D_XBC), CONV_W ** -0.5),
        "conv_b": nrm(ks[12], (DEPTH, D_XBC), 0.02),
        "dt_bias": dt0 + jnp.log(-jnp.expm1(-dt0)),
        "a_log": jnp.log(jax.random.uniform(ks[14], (DEPTH, N_HEADS), jnp.float32, 1.0, 16.0)),
        "d_skip": 1.0 + nrm(ks[15], (DEPTH, N_HEADS), 0.1),
        "ssm_norm_g": 1.0 + nrm(ks[16], (DEPTH, D_INNER), 0.05),
        "pool_w": nrm(ks[17], (DEPTH, N_POOL_GROUPS, POOL_GROUP, POOL_GROUP), POOL_GROUP ** -0.5),
        "pool_scale": 1.0 + nrm(ks[18], (DEPTH, D_POOL), 0.1),
        "w_pool_out": nrm(ks[19], (DEPTH, D_POOL, D_MODEL), D_POOL ** -0.5),
        "w_ssm_out": nrm(ks[20], (DEPTH, D_INNER, D_MODEL), D_INNER ** -0.5),
        "w_o": nrm(ks[21], (DEPTH, D_MODEL, D_MODEL), D_MODEL ** -0.5),
        "final_g": 1.0 + nrm(ks[22], (D_MODEL,), 0.05),
    }


def reference(x_prompt, x_sample, state_pool, state_conv, state_ssm, c_prompt, c_sample,
              w_ada, b_ada, norm_g, w_in, conv_w, conv_b, dt_bias, a_log, d_skip, ssm_norm_g,
              pool_w, pool_scale, w_pool_out, w_ssm_out, w_o, final_g):
    bp = x_prompt.shape[0]
    xp, xs = x_prompt, x_sample
    pp, cp, sp, ps, cs, ss = [], [], [], [], [], []
    for l in range(DEPTH):
        wl = (w_ada[l], b_ada[l], norm_g[l], w_in[l], conv_w[l], conv_b[l], dt_bias[l], a_log[l],
              d_skip[l], ssm_norm_g[l], pool_w[l], pool_scale[l], w_pool_out[l], w_ssm_out[l], w_o[l])
        xp, p_new, c_new, s_new = layer(
            xp, c_prompt, 0,
            jnp.zeros((bp, POOL_BUF, D_POOL), xp.dtype),
            jnp.zeros((bp, CONV_W - 1, D_XBC), xp.dtype),
            jnp.zeros((bp, N_HEADS, HEAD_DIM, D_STATE), jnp.float32), *wl)
        pp.append(p_new); cp.append(c_new); sp.append(s_new)
        xs, p_new, c_new, s_new = layer(xs, c_sample, PAST_LEN, state_pool[l], state_conv[l], state_ssm[l], *wl)
        ps.append(p_new); cs.append(c_new); ss.append(s_new)
    y_prompt = rmsnorm(xp, final_g)
    y_sample = rmsnorm(xs, final_g)
    return (y_prompt, y_sample, jnp.stack(pp), jnp.stack(cp), jnp.stack(sp),
            jnp.stack(ps), jnp.stack(cs), jnp.stack(ss))
```

```python
import functools

import jax
import jax.numpy as jnp
from jax import lax
from jax.experimental import pallas as pl
from jax.experimental.pallas import tpu as pltpu

F32 = jnp.float32
BF16 = jnp.bfloat16

EPS = 1e-6
D_MODEL = 1024
POOL_WINDOWS = (2, 4, 8, 16)
POOL_GROUP = 128
D_POOL = POOL_GROUP * len(POOL_WINDOWS)
POOL_BUF = max(POOL_WINDOWS) - 1
D_INNER = 2048
HEAD_DIM = 64
N_HEADS = D_INNER // HEAD_DIM
N_BC_GROUPS = 4
HEADS_PER_GROUP = N_HEADS // N_BC_GROUPS
GROUP_COLS = HEADS_PER_GROUP * HEAD_DIM
D_STATE = 128
CONV_W = 4
D_XBC = D_INNER + 2 * N_BC_GROUPS * D_STATE
N_MAIN = 2 * D_POOL + D_INNER + D_XBC
PAST_LEN = 16384

LANES = 128
CHUNK = 128
POOL_HDR = 16
CONV_HDR = 8
DOT_COLS = 512
VMEM_LIMIT_BYTES = 58 * 1024 * 1024


def _dot(a, b):
    return jnp.dot(a, b, preferred_element_type=F32)


def _dot_nt(a, b):
    return lax.dot_general(a, b, (((1,), (1,)), ((), ())), preferred_element_type=F32)


def _split3(v):
    hi = v.astype(BF16)
    r = v - hi.astype(F32)
    mid = r.astype(BF16)
    lo = (r - mid.astype(F32)).astype(BF16)
    return hi, mid, lo


def _silu(v):
    return v / (1.0 + jnp.exp(-v))


def _sigmoid(v):
    return 1.0 / (1.0 + jnp.exp(-v))


def _softplus(v):
    return jnp.maximum(v, 0.0) + jnp.log1p(jnp.exp(-jnp.abs(v)))


def _adaln_kernel(c_ref, w_ref, b_ref, o_ref):
    c = c_ref[...]
    o_ref[...] = _dot(_silu(c).astype(BF16), w_ref[...]) + b_ref[...]


def _layer_kernel(*refs, S, L, QL, pos0, has_state, n_inner, seq_per_inner):
    R = S * L
    n_chunks = R // CHUNK
    it = iter(refs)
    x_ref = next(it)
    mod_ref = next(it)
    if has_state:
        pool_in_ref = next(it)
        conv_in_ref = next(it)
        ssm_in_ref = next(it)
    (normg_ref, wmain_ref, wdt_ref, wg_ref, convw_ref, convb_ref, dtb_ref, alog_ref,
     dskip_ref, ssmg_ref, poolw_ref, pscale_ref, wpo_ref, wso_ref, wo_ref, fg_ref,
     ta_ref, e3_ref) = [next(it) for _ in range(18)]
    y_ref, pool_out_ref, conv_out_ref, ssm_out_ref = [next(it) for _ in range(4)]
    (h_scr, up_scr, zp_scr, zs_scr, xbc_scr, g_scr, xs_scr, b_scr, c_scr, dt_scr,
     yssd_scr, pm_scr, p_scr, ea_scr, we_scr, elt_scr) = [next(it) for _ in range(16)]

    step = pl.program_id(1)

    def dense_in():
        x3 = x_ref[...]
        mod = mod_ref[...]
        shift = mod[:, :, 0:D_MODEL]
        scale = mod[:, :, D_MODEL:2 * D_MODEL]
        ms = jnp.mean(x3 * x3, axis=-1, keepdims=True)
        hn = x3 * lax.rsqrt(ms + EPS) * normg_ref[...]
        hm = hn * (1.0 + scale) + shift
        h_scr[...] = hm.reshape(R, D_MODEL).astype(BF16)

        def proj(col0, ncols, store):
            for j in range(0, ncols, DOT_COLS):
                w = min(DOT_COLS, ncols - j)
                store(j, w, _dot(h_scr[...], wmain_ref[:, col0 + j:col0 + j + w]))

        def st_up(j, w, v):
            up_scr[:, POOL_HDR:POOL_HDR + L, j:j + w] = v.reshape(S, L, w)

        def st_zp(j, w, v):
            zp_scr[:, j:j + w] = v

        def st_zs(j, w, v):
            zs_scr[:, j:j + w] = v

        def st_xbc(j, w, v):
            xbc_scr[:, CONV_HDR:CONV_HDR + L, j:j + w] = v.reshape(S, L, w)

        proj(0, D_POOL, st_up)
        proj(D_POOL, D_POOL, st_zp)
        proj(2 * D_POOL, D_INNER, st_zs)
        proj(2 * D_POOL + D_INNER, D_XBC, st_xbc)
        for j in range(0, 2 * D_MODEL, DOT_COLS):
            g_scr[:, j:j + DOT_COLS] = _dot(h_scr[...], wg_ref[:, j:j + DOT_COLS])
        dt_raw = _dot(h_scr[...], wdt_ref[...])
        dt_scr[...] = _softplus(dt_raw + dtb_ref[...])

        if has_state:
            up_scr[:, 1:POOL_HDR, :] = pool_in_ref[...]
            pos = pos0 + lax.broadcasted_iota(jnp.int32, (S, L, POOL_GROUP), 1)
        else:
            pos = pos0 + step * L + lax.broadcasted_iota(jnp.int32, (S, L, POOL_GROUP), 1)
        for gi, w in enumerate(POOL_WINDOWS):
            sl = slice(gi * POOL_GROUP, (gi + 1) * POOL_GROUP)
            u = up_scr[:, POOL_HDR:POOL_HDR + L, sl]
            s = u
            for k in range(1, w):
                s = s + up_scr[:, POOL_HDR - k:POOL_HDR - k + L, sl]
            cnt = jnp.minimum(pos + 1, w).astype(F32)
            d = (s / cnt - u).reshape(R, POOL_GROUP)
            yg = _dot(d.astype(BF16), poolw_ref[gi]) * pscale_ref[:, sl]
            pm_scr[:, sl] = (yg * _silu(zp_scr[:, sl])).astype(BF16)
        for j in range(0, D_MODEL, DOT_COLS):
            p_scr[:, j:j + DOT_COLS] = _dot(pm_scr[...], wpo_ref[:, j:j + DOT_COLS])
        pool_out_ref[...] = up_scr[:, L + 1:L + POOL_HDR, :]
        if not has_state:
            up_scr[:, 0:POOL_HDR, :] = up_scr[:, L:L + POOL_HDR, :]

        if has_state:
            xbc_scr[:, CONV_HDR - (CONV_W - 1):CONV_HDR, :] = conv_in_ref[...]
        for j in range(0, D_XBC, DOT_COLS):
            sl = slice(j, j + DOT_COLS)
            acc = convb_ref[:, sl]
            for k in range(CONV_W):
                r0 = CONV_HDR - (CONV_W - 1) + k
                acc = acc + xbc_scr[:, r0:r0 + L, sl] * convw_ref[k:k + 1, sl]
            v = _silu(acc).reshape(R, DOT_COLS)
            if j < D_INNER:
                xs_scr[:, sl] = v
            elif j < D_INNER + N_BC_GROUPS * D_STATE:
                b_scr[...] = v
            else:
                c_scr[...] = v
        conv_out_ref[...] = xbc_scr[:, L + CONV_HDR - (CONV_W - 1):L + CONV_HDR, :]
        if not has_state:
            xbc_scr[:, 0:CONV_HDR, :] = xbc_scr[:, L:L + CONV_HDR, :]

    def ssd_intra(c):
        rows = pl.ds(c * CHUNK, CHUNK)
        lane = lax.broadcasted_iota(jnp.int32, (CHUNK, LANES), 1)
        row = lax.broadcasted_iota(jnp.int32, (CHUNK, LANES), 0)
        neg_a = jnp.where(lane[0:1, :] < N_HEADS, -jnp.exp(alog_ref[...]), 0.0)
        dtc = dt_scr[rows, :]
        a = dtc * neg_a
        hi, mid, lo = _split3(a)
        cs = _dot(ta_ref[...], hi) + _dot(ta_ref[...], mid) + _dot(ta_ref[...], lo)
        acs = cs[0:CHUNK, :]
        last = cs[CHUNK:2 * CHUNK, :]
        exp_acs = jnp.exp(acs)
        wgt = jnp.exp(last - acs) * dtc
        ea_scr[...] = _dot(jnp.concatenate(_split3(exp_acs), axis=1), e3_ref[...])
        we_scr[...] = _dot(jnp.concatenate(_split3(wgt), axis=1), e3_ref[...])
        elt_scr[...] = jnp.exp(last).T
        acs_t = acs.T
        dt_t = dtc.T
        ql_bits = QL.bit_length() - 1
        mask = ((lax.shift_right_logical(row, ql_bits) == lax.shift_right_logical(lane, ql_bits))
                & (lane <= row))
        for g in range(N_BC_GROUPS):
            gs = slice(g * D_STATE, (g + 1) * D_STATE)
            cb = _dot_nt(c_scr[rows, gs].astype(BF16), b_scr[rows, gs].astype(BF16))
            for k in range(HEADS_PER_GROUP // 2):
                atts = []
                for h in (g * HEADS_PER_GROUP + 2 * k, g * HEADS_PER_GROUP + 2 * k + 1):
                    seg = acs[:, h:h + 1] - acs_t[h:h + 1, :]
                    decay = jnp.exp(jnp.where(mask, seg, -jnp.inf))
                    atts.append((cb * decay * dt_t[h:h + 1, :]).astype(BF16))
                ps = slice((g * HEADS_PER_GROUP + 2 * k) * HEAD_DIM,
                           (g * HEADS_PER_GROUP + 2 * k + 2) * HEAD_DIM)
                xp = xs_scr[rows, ps]
                rhs = jnp.concatenate([jnp.where(lane < HEAD_DIM, xp, 0.0).astype(BF16),
                                       jnp.where(lane >= HEAD_DIM, xp, 0.0).astype(BF16)], axis=0)
                yp = _dot(jnp.concatenate(atts, axis=1), rhs)
                yssd_scr[rows, ps] = yp + dskip_ref[:, ps] * xp

    def ssd_state(c, r0, masked, hin, hout):
        rows = pl.ds(c * CHUNK, CHUNK)
        lane = lax.broadcasted_iota(jnp.int32, (CHUNK, LANES), 1)
        elt = elt_scr[...]
        col = jnp.sum(jnp.where(lane == r0, elt, 0.0), axis=1, keepdims=True)
        rb = jnp.broadcast_to(col, (CHUNK, LANES))
        if masked:
            rowi = lax.broadcasted_iota(jnp.int32, (CHUNK, GROUP_COLS), 0)
            rmask = (rowi >= r0) & (rowi < r0 + QL)
        for g in range(N_BC_GROUPS):
            gs = slice(g * D_STATE, (g + 1) * D_STATE)
            cs = slice(g * GROUP_COLS, (g + 1) * GROUP_COLS)
            hg = hin[cs, :]
            yoff = _dot_nt(c_scr[rows, gs].astype(BF16), hg.astype(BF16)) * ea_scr[:, cs]
            xw = xs_scr[rows, cs] * we_scr[:, cs]
            if masked:
                yoff = jnp.where(rmask, yoff, 0.0)
                xw = jnp.where(rmask, xw, 0.0)
            yssd_scr[rows, cs] = yssd_scr[rows, cs] + yoff
            upd = _dot(xw.T.astype(BF16), b_scr[rows, gs].astype(BF16))
            for e in range(HEADS_PER_GROUP):
                h = g * HEADS_PER_GROUP + e
                hs = slice(h * HEAD_DIM, (h + 1) * HEAD_DIM)
                es = slice(e * HEAD_DIM, (e + 1) * HEAD_DIM)
                hout[hs, :] = hg[es, :] * rb[h:h + 1, :] + upd[es, :]

    def dense_out():
        y = yssd_scr[...] * _silu(zs_scr[...])
        ms = jnp.mean(y * y, axis=-1, keepdims=True)
        yn = (y * lax.rsqrt(ms + EPS) * ssmg_ref[...]).astype(BF16)
        gates = _sigmoid(g_scr[...])
        m = (gates[:, 0:D_MODEL] * p_scr[...]
             + gates[:, D_MODEL:2 * D_MODEL] * _dot(yn, wso_ref[...])).astype(BF16)
        o = _dot(m, wo_ref[...]).reshape(S, L, D_MODEL)
        gate = mod_ref[...][:, :, 2 * D_MODEL:3 * D_MODEL]
        xn = x_ref[...] + gate * o
        ms2 = jnp.mean(xn * xn, axis=-1, keepdims=True)
        y_ref[...] = xn * lax.rsqrt(ms2 + EPS) * fg_ref[...]

    if has_state:
        @pl.when(step == 0)
        def _():
            dense_in()
            ssd_intra(0)

        for q in range(seq_per_inner):
            r0 = (step * seq_per_inner + q) * QL
            ssd_state(0, r0, True, ssm_in_ref.at[q], ssm_out_ref.at[q])

        @pl.when(step == n_inner - 1)
        def _():
            dense_out()
    else:
        @pl.when(step == 0)
        def _():
            up_scr[:, 0:POOL_HDR, :] = jnp.zeros((S, POOL_HDR, D_POOL), F32)
            xbc_scr[:, 0:CONV_HDR, :] = jnp.zeros((S, CONV_HDR, D_XBC), F32)
            ssm_out_ref[...] = jnp.zeros(ssm_out_ref.shape, F32)

        dense_in()
        for c in range(n_chunks):
            ssd_intra(c)
            ssd_state(c, 0, False, ssm_out_ref.at[0], ssm_out_ref.at[0])
        dense_out()


def _resident(shape):
    zeros = (0,) * len(shape)
    return pl.BlockSpec(shape, lambda *_: zeros, pipeline_mode=pl.Buffered(1))


def _layer_call(x, mod, states, weights, *, S, L, QL, pos0, n_inner, seq_per_inner, name):
    nseq = x.shape[0]
    has_state = states is not None
    R = S * L
    if has_state:
        grid = (nseq // S, n_inner)
        seq_map = lambda o, i: (o, 0, 0)
        st_map = lambda o, i: (o * n_inner + i, 0, 0)
        x_spec = pl.BlockSpec((S, L, D_MODEL), seq_map)
        st_rows = seq_per_inner
    else:
        grid = (nseq, x.shape[1] // L)
        seq_map = lambda b, t: (b, 0, 0)
        st_map = seq_map
        x_spec = pl.BlockSpec((1, L, D_MODEL), lambda b, t: (b, t, 0))
        st_rows = 1
    in_specs = [x_spec, pl.BlockSpec((S, 1, 3 * D_MODEL), seq_map)]
    args = [x, mod]
    if has_state:
        in_specs += [pl.BlockSpec((S, POOL_BUF, D_POOL), seq_map),
                     pl.BlockSpec((S, CONV_W - 1, D_XBC), seq_map),
                     pl.BlockSpec((st_rows, D_INNER, D_STATE), st_map)]
        args += list(states)
    in_specs += [_resident(w.shape) for w in weights]
    args += list(weights)
    out_shape = (jax.ShapeDtypeStruct(x.shape, F32),
                 jax.ShapeDtypeStruct((nseq, POOL_BUF, D_POOL), F32),
                 jax.ShapeDtypeStruct((nseq, CONV_W - 1, D_XBC), F32),
                 jax.ShapeDtypeStruct((nseq, D_INNER, D_STATE), F32))
    out_specs = (x_spec,
                 pl.BlockSpec((S, POOL_BUF, D_POOL), seq_map),
                 pl.BlockSpec((S, CONV_W - 1, D_XBC), seq_map),
                 pl.BlockSpec((st_rows, D_INNER, D_STATE), st_map))
    scratch = [
        pltpu.VMEM((R, D_MODEL), BF16),
        pltpu.VMEM((S, POOL_HDR + L, D_POOL), F32),
        pltpu.VMEM((R, D_POOL), F32),
        pltpu.VMEM((R, D_INNER), F32),
        pltpu.VMEM((S, CONV_HDR + L, D_XBC), F32),
        pltpu.VMEM((R, 2 * D_MODEL), F32),
        pltpu.VMEM((R, D_INNER), F32),
        pltpu.VMEM((R, N_BC_GROUPS * D_STATE), F32),
        pltpu.VMEM((R, N_BC_GROUPS * D_STATE), F32),
        pltpu.VMEM((R, LANES), F32),
        pltpu.VMEM((R, D_INNER), F32),
        pltpu.VMEM((R, D_POOL), BF16),
        pltpu.VMEM((R, D_MODEL), F32),
        pltpu.VMEM((CHUNK, D_INNER), F32),
        pltpu.VMEM((CHUNK, D_INNER), F32),
        pltpu.VMEM((LANES, CHUNK), F32),
    ]
    kern = functools.partial(_layer_kernel, S=S, L=L, QL=QL, pos0=pos0, has_state=has_state,
                             n_inner=n_inner, seq_per_inner=seq_per_inner)
    return pl.pallas_call(
        kern, grid=grid, in_specs=in_specs, out_specs=out_specs, out_shape=out_shape,
        scratch_shapes=scratch, name=name,
        compiler_params=pltpu.CompilerParams(
            dimension_semantics=("arbitrary", "arbitrary"),
            vmem_limit_bytes=VMEM_LIMIT_BYTES),
    )(*args)


def _block_sum_matrix(ql):
    r = jnp.arange(CHUNK)[:, None]
    c = jnp.arange(CHUNK)[None, :]
    same = (r // ql) == (c // ql)
    return jnp.concatenate([same & (c <= r), same], axis=0).astype(BF16)


def _expand_matrix():
    k = jnp.arange(3 * LANES)[:, None] % LANES
    c = jnp.arange(D_INNER)[None, :] // HEAD_DIM
    return (k == c).astype(BF16)


def kernel(x_prompt, x_sample, state_pool, state_conv, state_ssm, c_prompt, c_sample, w_ada, b_ada, norm_g, w_in, conv_w, conv_b, dt_bias, a_log, d_skip, ssm_norm_g, pool_w, pool_scale, w_pool_out, w_ssm_out, w_o, final_g):
    depth = w_ada.shape[0]
    assert depth == 1, "kernel is written for a single layer"
    nb, seq, _ = x_prompt.shape
    ns, dec_seq, _ = x_sample.shape
    l = 0
    pad_heads = ((0, 0), (0, LANES - N_HEADS))

    c_all = jnp.concatenate([c_prompt, c_sample], axis=0)
    mod = pl.pallas_call(
        _adaln_kernel, out_shape=jax.ShapeDtypeStruct((nb + ns, 3 * D_MODEL), F32), name="adaln",
        compiler_params=pltpu.CompilerParams(vmem_limit_bytes=VMEM_LIMIT_BYTES),
    )(c_all, w_ada[l].astype(BF16), b_ada[l][None, :])
    mod = mod[:, None, :]

    e3 = _expand_matrix()

    def weights(ql):
        return [
            norm_g[l][None, :],
            w_in[l][:, :N_MAIN].astype(BF16),
            jnp.pad(w_in[l][:, N_MAIN:N_MAIN + N_HEADS], pad_heads).astype(BF16),
            w_in[l][:, N_MAIN + N_HEADS:].astype(BF16),
            conv_w[l], conv_b[l][None, :],
            jnp.pad(dt_bias[l][None, :], pad_heads),
            jnp.pad(a_log[l][None, :], pad_heads),
            jnp.repeat(d_skip[l], HEAD_DIM)[None, :],
            ssm_norm_g[l][None, :],
            pool_w[l].astype(BF16), pool_scale[l][None, :],
            w_pool_out[l].astype(BF16), w_ssm_out[l].astype(BF16), w_o[l].astype(BF16),
            final_g[None, :],
            _block_sum_matrix(ql), e3,
        ]

    tile = 2 * CHUNK
    y_p, pool_p, conv_p, ssm_p = _layer_call(
        x_prompt, mod[:nb], None, weights(CHUNK), S=1, L=tile, QL=CHUNK, pos0=0,
        n_inner=1, seq_per_inner=1, name="layer_prompt")

    seq_per_step = CHUNK // dec_seq
    seq_per_inner = 2
    states = (state_pool[l], state_conv[l],
              state_ssm[l].reshape(ns, D_INNER, D_STATE))
    y_s, pool_s, conv_s, ssm_s = _layer_call(
        x_sample, mod[nb:], states, weights(dec_seq), S=seq_per_step, L=dec_seq, QL=dec_seq,
        pos0=PAST_LEN, n_inner=seq_per_step // seq_per_inner, seq_per_inner=seq_per_inner,
        name="layer_sample")

    shp = (N_HEADS, HEAD_DIM, D_STATE)
    return (y_p, y_s, pool_p[None], conv_p[None], ssm_p.reshape((1, nb) + shp),
            pool_s[None], conv_s[None], ssm_s.reshape((1, ns) + shp))
```

```python
import functools

import jax
import jax.numpy as jnp
from jax import lax
from jax.experimental import pallas as pl
from jax.experimental.pallas import tpu as pltpu

F32 = jnp.float32
BF16 = jnp.bfloat16

EPS = 1e-6
D_MODEL = 1024
POOL_WINDOWS = (2, 4, 8, 16)
POOL_GROUP = 128
D_POOL = POOL_GROUP * len(POOL_WINDOWS)
POOL_BUF = max(POOL_WINDOWS) - 1
D_INNER = 2048
HEAD_DIM = 64
N_HEADS = D_INNER // HEAD_DIM
N_BC_GROUPS = 4
HEADS_PER_GROUP = N_HEADS // N_BC_GROUPS
GROUP_COLS = HEADS_PER_GROUP * HEAD_DIM
D_STATE = 128
CONV_W = 4
D_XBC = D_INNER + 2 * N_BC_GROUPS * D_STATE
N_MAIN = 2 * D_POOL + D_INNER + D_XBC
PAST_LEN = 16384

LANES = 128
CHUNK = 128
POOL_HDR = 16
CONV_HDR = 8
DOT_COLS = 512
VMEM_LIMIT_BYTES = 58 * 1024 * 1024


def _dot(a, b):
    return jnp.dot(a, b, preferred_element_type=F32)


def _dot_nt(a, b):
    return lax.dot_general(a, b, (((1,), (1,)), ((), ())), preferred_element_type=F32)


def _pack3(v, lane):
    hi = v.astype(BF16).astype(F32)
    r = v - hi
    mid = r.astype(BF16).astype(F32)
    lo = r - mid
    packed = jnp.where(lane < N_HEADS, hi,
                       jnp.where(lane < 2 * N_HEADS, mid,
                                 jnp.where(lane < 3 * N_HEADS, lo, 0.0)))
    return packed.astype(BF16)


def _silu(v):
    return v / (1.0 + jnp.exp(-v))


def _sigmoid(v):
    return 1.0 / (1.0 + jnp.exp(-v))


def _softplus(v):
    return jnp.maximum(v, 0.0) + jnp.log1p(jnp.exp(-jnp.abs(v)))


def _adaln_kernel(c_ref, w_ref, b_ref, o_ref):
    c = c_ref[...]
    o_ref[...] = _dot(_silu(c).astype(BF16), w_ref[...]) + b_ref[...]


def _layer_kernel(*refs, S, L, QL, pos0, has_state, n_inner, seq_per_inner):
    R = S * L
    n_chunks = R // CHUNK
    it = iter(refs)
    x_ref = next(it)
    mod_ref = next(it)
    if has_state:
        pool_in_ref = next(it)
        conv_in_ref = next(it)
        ssm_in_ref = next(it)
    (normg_ref, wcat_ref, convw_ref, convb_ref, dtb_ref, alog_ref,
     dskip_ref, ssmg_ref, poolw_ref, pscale_ref, wpo_ref, wso_ref, wo_ref, fg_ref,
     ta_ref, e3_ref) = [next(it) for _ in range(16)]
    y_ref, pool_out_ref, conv_out_ref, ssm_out_ref = [next(it) for _ in range(4)]
    (h_scr, up_scr, zp_scr, zs_scr, xbc_scr, g_scr, xs_scr, b_scr, c_scr, dt_scr,
     yssd_scr, pm_scr, p_scr, ea_scr, we_scr, aux_scr) = [next(it) for _ in range(16)]

    step = pl.program_id(1)

    def dense_in():
        x3 = x_ref[...]
        mod = mod_ref[...]
        shift = mod[:, :, 0:D_MODEL]
        scale = mod[:, :, D_MODEL:2 * D_MODEL]
        ms = jnp.mean(x3 * x3, axis=-1, keepdims=True)
        hn = x3 * lax.rsqrt(ms + EPS) * normg_ref[...]
        hm = hn * (1.0 + scale) + shift
        h_scr[...] = hm.reshape(R, D_MODEL).astype(BF16)

        def proj(col0, ncols, store):
            for j in range(0, ncols, DOT_COLS):
                w = min(DOT_COLS, ncols - j)
                store(j, w, _dot(h_scr[...], wcat_ref[:, col0 + j:col0 + j + w]))

        def st_up(j, w, v):
            up_scr[:, POOL_HDR:POOL_HDR + L, j:j + w] = v.reshape(S, L, w)

        def st_zp(j, w, v):
            zp_scr[:, j:j + w] = v

        def st_zs(j, w, v):
            zs_scr[:, j:j + w] = v

        def st_xbc(j, w, v):
            xbc_scr[:, CONV_HDR:CONV_HDR + L, j:j + w] = v.reshape(S, L, w)

        def st_dt(j, w, v):
            dt_scr[...] = _softplus(v + dtb_ref[...])

        def st_g(j, w, v):
            g_scr[:, j:j + w] = v

        proj(0, D_POOL, st_up)
        proj(D_POOL, D_POOL, st_zp)
        proj(2 * D_POOL, D_INNER, st_zs)
        proj(2 * D_POOL + D_INNER, D_XBC, st_xbc)
        proj(N_MAIN, LANES, st_dt)
        proj(N_MAIN + LANES, 2 * D_MODEL, st_g)

        if has_state:
            up_scr[:, 1:POOL_HDR, :] = pool_in_ref[...]
            pos = pos0 + lax.broadcasted_iota(jnp.int32, (S, L, POOL_GROUP), 1)
        else:
            pos = pos0 + step * L + lax.broadcasted_iota(jnp.int32, (S, L, POOL_GROUP), 1)
        for gi, w in enumerate(POOL_WINDOWS):
            sl = slice(gi * POOL_GROUP, (gi + 1) * POOL_GROUP)
            u = up_scr[:, POOL_HDR:POOL_HDR + L, sl]
            s = u
            for k in range(1, w):
                s = s + up_scr[:, POOL_HDR - k:POOL_HDR - k + L, sl]
            cnt = jnp.minimum(pos + 1, w).astype(F32)
            d = (s / cnt - u).reshape(R, POOL_GROUP)
            yg = _dot(d.astype(BF16), poolw_ref[gi]) * pscale_ref[:, sl]
            pm_scr[:, sl] = (yg * _silu(zp_scr[:, sl])).astype(BF16)
        for j in range(0, D_MODEL, DOT_COLS):
            p_scr[:, j:j + DOT_COLS] = _dot(pm_scr[...], wpo_ref[:, j:j + DOT_COLS])
        pool_out_ref[...] = up_scr[:, L + 1:L + POOL_HDR, :]
        if not has_state:
            up_scr[:, 0:POOL_HDR, :] = up_scr[:, L:L + POOL_HDR, :]

        if has_state:
            xbc_scr[:, CONV_HDR - (CONV_W - 1):CONV_HDR, :] = conv_in_ref[...]
        for j in range(0, D_XBC, DOT_COLS):
            sl = slice(j, j + DOT_COLS)
            acc = convb_ref[:, sl]
            for k in range(CONV_W):
                r0 = CONV_HDR - (CONV_W - 1) + k
                acc = acc + xbc_scr[:, r0:r0 + L, sl] * convw_ref[k:k + 1, sl]
            v = _silu(acc).reshape(R, DOT_COLS)
            if j < D_INNER:
                xs_scr[:, sl] = v
            elif j < D_INNER + N_BC_GROUPS * D_STATE:
                b_scr[...] = v
            else:
                c_scr[...] = v
        conv_out_ref[...] = xbc_scr[:, L + CONV_HDR - (CONV_W - 1):L + CONV_HDR, :]
        if not has_state:
            xbc_scr[:, 0:CONV_HDR, :] = xbc_scr[:, L:L + CONV_HDR, :]

    def ssd_intra(c):
        rows = pl.ds(c * CHUNK, CHUNK)
        lane = lax.broadcasted_iota(jnp.int32, (CHUNK, LANES), 1)
        row = lax.broadcasted_iota(jnp.int32, (CHUNK, LANES), 0)
        neg_a = jnp.where(lane[0:1, :] < 3 * N_HEADS, -jnp.exp(alog_ref[...]), 0.0)
        dtc = dt_scr[rows, :]
        a = dtc * neg_a
        cs = _dot(ta_ref[...], _pack3(a, lane))
        cs = cs + pltpu.roll(cs, 2 * N_HEADS, 1)
        cs = cs + pltpu.roll(cs, N_HEADS, 1)
        acs = cs[0:CHUNK, :]
        last = cs[CHUNK:2 * CHUNK, :]
        exp_acs = jnp.exp(acs)
        wgt = jnp.exp(last - acs) * dtc
        ea_scr[...] = _dot(_pack3(exp_acs, lane), e3_ref[...])
        we_scr[...] = _dot(_pack3(wgt, lane), e3_ref[...])
        if has_state:
            aux_scr[...] = jnp.exp(last).T
        acs_t = acs.T
        dt_t = dtc.T
        ql_bits = QL.bit_length() - 1
        mask = ((lax.shift_right_logical(row, ql_bits) == lax.shift_right_logical(lane, ql_bits))
                & (lane <= row))
        for g in range(N_BC_GROUPS):
            gs = slice(g * D_STATE, (g + 1) * D_STATE)
            cb = _dot_nt(c_scr[rows, gs].astype(BF16), b_scr[rows, gs].astype(BF16))
            yoff = None if has_state else carried_state_group(rows, g)
            for k in range(HEADS_PER_GROUP // 2):
                atts = []
                for h in (g * HEADS_PER_GROUP + 2 * k, g * HEADS_PER_GROUP + 2 * k + 1):
                    seg = acs[:, h:h + 1] - acs_t[h:h + 1, :]
                    decay = jnp.exp(jnp.where(mask, seg, -jnp.inf))
                    atts.append((cb * decay * dt_t[h:h + 1, :]).astype(BF16))
                ps = slice((g * HEADS_PER_GROUP + 2 * k) * HEAD_DIM,
                           (g * HEADS_PER_GROUP + 2 * k + 2) * HEAD_DIM)
                xp = xs_scr[rows, ps]
                rhs = jnp.concatenate([jnp.where(lane < HEAD_DIM, xp, 0.0).astype(BF16),
                                       jnp.where(lane >= HEAD_DIM, xp, 0.0).astype(BF16)], axis=0)
                yp = _dot(jnp.concatenate(atts, axis=1), rhs) + dskip_ref[:, ps] * xp
                if yoff is not None:
                    yp = yp + yoff[:, 2 * k * HEAD_DIM:(2 * k + 2) * HEAD_DIM]
                yssd_scr[rows, ps] = yp

    def carried_state_group(rows, g):
        gs = slice(g * D_STATE, (g + 1) * D_STATE)
        cs = slice(g * GROUP_COLS, (g + 1) * GROUP_COLS)
        ht = aux_scr[:, cs]
        yoff = _dot(c_scr[rows, gs].astype(BF16), ht.astype(BF16)) * ea_scr[:, cs]
        xw = (xs_scr[rows, cs] * we_scr[:, cs]).astype(BF16)
        upd = _dot(b_scr[rows, gs].T.astype(BF16), xw)
        aux_scr[:, cs] = ht * ea_scr[CHUNK - 1:CHUNK, cs] + upd
        return yoff

    def ssd_state(c, r0, masked, hin, hout):
        rows = pl.ds(c * CHUNK, CHUNK)
        lane = lax.broadcasted_iota(jnp.int32, (CHUNK, LANES), 1)
        elt = aux_scr[...]
        col = jnp.sum(jnp.where(lane == r0, elt, 0.0), axis=1, keepdims=True)
        rb = jnp.broadcast_to(col, (CHUNK, LANES))
        if masked:
            rowi = lax.broadcasted_iota(jnp.int32, (CHUNK, GROUP_COLS), 0)
            rmask = (rowi >= r0) & (rowi < r0 + QL)
        for g in range(N_BC_GROUPS):
            gs = slice(g * D_STATE, (g + 1) * D_STATE)
            cs = slice(g * GROUP_COLS, (g + 1) * GROUP_COLS)
            hg = hin[cs, :]
            yoff = _dot_nt(c_scr[rows, gs].astype(BF16), hg.astype(BF16)) * ea_scr[:, cs]
            xw = xs_scr[rows, cs] * we_scr[:, cs]
            if masked:
                yoff = jnp.where(rmask, yoff, 0.0)
                xw = jnp.where(rmask, xw, 0.0)
            yssd_scr[rows, cs] = yssd_scr[rows, cs] + yoff
            upd = _dot(xw.T.astype(BF16), b_scr[rows, gs].astype(BF16))
            for e in range(HEADS_PER_GROUP):
                h = g * HEADS_PER_GROUP + e
                hs = slice(h * HEAD_DIM, (h + 1) * HEAD_DIM)
                es = slice(e * HEAD_DIM, (e + 1) * HEAD_DIM)
                hout[hs, :] = hg[es, :] * rb[h:h + 1, :] + upd[es, :]

    def dense_out():
        y = yssd_scr[...] * _silu(zs_scr[...])
        ms = jnp.mean(y * y, axis=-1, keepdims=True)
        yn = (y * lax.rsqrt(ms + EPS) * ssmg_ref[...]).astype(BF16)
        gates = _sigmoid(g_scr[...])
        m = (gates[:, 0:D_MODEL] * p_scr[...]
             + gates[:, D_MODEL:2 * D_MODEL] * _dot(yn, wso_ref[...])).astype(BF16)
        o = _dot(m, wo_ref[...]).reshape(S, L, D_MODEL)
        gate = mod_ref[...][:, :, 2 * D_MODEL:3 * D_MODEL]
        xn = x_ref[...] + gate * o
        ms2 = jnp.mean(xn * xn, axis=-1, keepdims=True)
        y_ref[...] = xn * lax.rsqrt(ms2 + EPS) * fg_ref[...]

    if has_state:
        @pl.when(step == 0)
        def _():
            dense_in()
            ssd_intra(0)

        for q in range(seq_per_inner):
            r0 = (step * seq_per_inner + q) * QL
            ssd_state(0, r0, True, ssm_in_ref.at[q], ssm_out_ref.at[q])

        @pl.when(step == n_inner - 1)
        def _():
            dense_out()
    else:
        @pl.when(step == 0)
        def _():
            up_scr[:, 0:POOL_HDR, :] = jnp.zeros((S, POOL_HDR, D_POOL), F32)
            xbc_scr[:, 0:CONV_HDR, :] = jnp.zeros((S, CONV_HDR, D_XBC), F32)
            aux_scr[...] = jnp.zeros(aux_scr.shape, F32)

        dense_in()
        for c in range(n_chunks):
            ssd_intra(c)
        dense_out()

        @pl.when(step == pl.num_programs(1) - 1)
        def _():
            for i in range(D_INNER // LANES):
                cols = slice(i * LANES, (i + 1) * LANES)
                ssm_out_ref[0, cols, :] = aux_scr[:, cols].T


def _resident(shape):
    zeros = (0,) * len(shape)
    return pl.BlockSpec(shape, lambda *_: zeros, pipeline_mode=pl.Buffered(1))


def _layer_call(x, mod, states, weights, *, S, L, QL, pos0, n_inner, seq_per_inner, name):
    nseq = x.shape[0]
    has_state = states is not None
    R = S * L
    if has_state:
        grid = (nseq // S, n_inner)
        seq_map = lambda o, i: (o, 0, 0)
        st_map = lambda o, i: (o * n_inner + i, 0, 0)
        x_spec = pl.BlockSpec((S, L, D_MODEL), seq_map)
        st_rows = seq_per_inner
    else:
        grid = (nseq, x.shape[1] // L)
        seq_map = lambda b, t: (b, 0, 0)
        st_map = seq_map
        x_spec = pl.BlockSpec((1, L, D_MODEL), lambda b, t: (b, t, 0))
        st_rows = 1
    in_specs = [x_spec, pl.BlockSpec((S, 1, 3 * D_MODEL), seq_map)]
    args = [x, mod]
    if has_state:
        in_specs += [pl.BlockSpec((S, POOL_BUF, D_POOL), seq_map),
                     pl.BlockSpec((S, CONV_W - 1, D_XBC), seq_map),
                     pl.BlockSpec((st_rows, D_INNER, D_STATE), st_map)]
        args += list(states)
    in_specs += [_resident(w.shape) for w in weights]
    args += list(weights)
    out_shape = (jax.ShapeDtypeStruct(x.shape, F32),
                 jax.ShapeDtypeStruct((nseq, POOL_BUF, D_POOL), F32),
                 jax.ShapeDtypeStruct((nseq, CONV_W - 1, D_XBC), F32),
                 jax.ShapeDtypeStruct((nseq, D_INNER, D_STATE), F32))
    out_specs = (x_spec,
                 pl.BlockSpec((S, POOL_BUF, D_POOL), seq_map),
                 pl.BlockSpec((S, CONV_W - 1, D_XBC), seq_map),
                 pl.BlockSpec((st_rows, D_INNER, D_STATE), st_map))
    scratch = [
        pltpu.VMEM((R, D_MODEL), BF16),
        pltpu.VMEM((S, POOL_HDR + L, D_POOL), F32),
        pltpu.VMEM((R, D_POOL), F32),
        pltpu.VMEM((R, D_INNER), F32),
        pltpu.VMEM((S, CONV_HDR + L, D_XBC), F32),
        pltpu.VMEM((R, 2 * D_MODEL), F32),
        pltpu.VMEM((R, D_INNER), F32),
        pltpu.VMEM((R, N_BC_GROUPS * D_STATE), F32),
        pltpu.VMEM((R, N_BC_GROUPS * D_STATE), F32),
        pltpu.VMEM((R, LANES), F32),
        pltpu.VMEM((R, D_INNER), F32),
        pltpu.VMEM((R, D_POOL), BF16),
        pltpu.VMEM((R, D_MODEL), F32),
        pltpu.VMEM((CHUNK, D_INNER), F32),
        pltpu.VMEM((CHUNK, D_INNER), F32),
        pltpu.VMEM((LANES, CHUNK) if has_state else (D_STATE, D_INNER), F32),
    ]
    kern = functools.partial(_layer_kernel, S=S, L=L, QL=QL, pos0=pos0, has_state=has_state,
                             n_inner=n_inner, seq_per_inner=seq_per_inner)
    return pl.pallas_call(
        kern, grid=grid, in_specs=in_specs, out_specs=out_specs, out_shape=out_shape,
        scratch_shapes=scratch, name=name,
        compiler_params=pltpu.CompilerParams(
            dimension_semantics=("arbitrary", "arbitrary"),
            vmem_limit_bytes=VMEM_LIMIT_BYTES),
    )(*args)


def _block_sum_matrix(ql):
    r = jnp.arange(CHUNK)[:, None]
    c = jnp.arange(CHUNK)[None, :]
    same = (r // ql) == (c // ql)
    return jnp.concatenate([same & (c <= r), same], axis=0).astype(BF16)


def _expand_matrix():
    k = jnp.arange(LANES)[:, None]
    c = jnp.arange(D_INNER)[None, :] // HEAD_DIM
    return ((k % N_HEADS == c) & (k < 3 * N_HEADS)).astype(BF16)


def kernel(x_prompt, x_sample, state_pool, state_conv, state_ssm, c_prompt, c_sample, w_ada, b_ada, norm_g, w_in, conv_w, conv_b, dt_bias, a_log, d_skip, ssm_norm_g, pool_w, pool_scale, w_pool_out, w_ssm_out, w_o, final_g):
    depth = w_ada.shape[0]
    assert depth == 1, "kernel is written for a single layer"
    nb, seq, _ = x_prompt.shape
    ns, dec_seq, _ = x_sample.shape
    l = 0

    def head_lanes(v):
        return jnp.pad(jnp.tile(v, (1, 3)), ((0, 0), (0, LANES - 3 * N_HEADS)))

    c_all = jnp.concatenate([c_prompt, c_sample], axis=0)
    mod = pl.pallas_call(
        _adaln_kernel, out_shape=jax.ShapeDtypeStruct((nb + ns, 3 * D_MODEL), F32), name="adaln",
        compiler_params=pltpu.CompilerParams(vmem_limit_bytes=VMEM_LIMIT_BYTES),
    )(c_all, w_ada[l].astype(BF16), b_ada[l][None, :])
    mod = mod[:, None, :]

    e3 = _expand_matrix()
    w_cat = jnp.concatenate(
        [w_in[l][:, :N_MAIN], head_lanes(w_in[l][:, N_MAIN:N_MAIN + N_HEADS]),
         w_in[l][:, N_MAIN + N_HEADS:]], axis=1).astype(BF16)

    def weights(ql):
        return [
            norm_g[l][None, :],
            w_cat,
            conv_w[l], conv_b[l][None, :],
            head_lanes(dt_bias[l][None, :]),
            head_lanes(a_log[l][None, :]),
            jnp.repeat(d_skip[l], HEAD_DIM)[None, :],
            ssm_norm_g[l][None, :],
            pool_w[l].astype(BF16), pool_scale[l][None, :],
            w_pool_out[l].astype(BF16), w_ssm_out[l].astype(BF16), w_o[l].astype(BF16),
            final_g[None, :],
            _block_sum_matrix(ql), e3,
        ]

    tile = 2 * CHUNK
    y_p, pool_p, conv_p, ssm_p = _layer_call(
        x_prompt, mod[:nb], None, weights(CHUNK), S=1, L=tile, QL=CHUNK, pos0=0,
        n_inner=1, seq_per_inner=1, name="layer_prompt")

    seq_per_step = CHUNK // dec_seq
    seq_per_inner = 2
    states = (state_pool[l], state_conv[l],
              state_ssm[l].reshape(ns, D_INNER, D_STATE))
    y_s, pool_s, conv_s, ssm_s = _layer_call(
        x_sample, mod[nb:], states, weights(dec_seq), S=seq_per_step, L=dec_seq, QL=dec_seq,
        pos0=PAST_LEN, n_inner=seq_per_step // seq_per_inner, seq_per_inner=seq_per_inner,
        name="layer_sample")

    shp = (N_HEADS, HEAD_DIM, D_STATE)
    return (y_p, y_s, pool_p[None], conv_p[None], ssm_p.reshape((1, nb) + shp),
            pool_s[None], conv_s[None], ssm_s.reshape((1, ns) + shp))
```

```python
import functools
import types

import jax
import jax.numpy as jnp
from jax import lax
from jax.experimental import pallas as pl
from jax.experimental.pallas import tpu as pltpu

F32 = jnp.float32
BF16 = jnp.bfloat16

EPS = 1e-6
D_MODEL = 1024
POOL_WINDOWS = (2, 4, 8, 16)
POOL_GROUP = 128
D_POOL = POOL_GROUP * len(POOL_WINDOWS)
POOL_BUF = max(POOL_WINDOWS) - 1
D_INNER = 2048
HEAD_DIM = 64
N_HEADS = D_INNER // HEAD_DIM
N_BC_GROUPS = 4
HEADS_PER_GROUP = N_HEADS // N_BC_GROUPS
GROUP_COLS = HEADS_PER_GROUP * HEAD_DIM
D_STATE = 128
CONV_W = 4
D_XBC = D_INNER + 2 * N_BC_GROUPS * D_STATE
PAST_LEN = 16384

LANES = 128
CHUNK = 128
POOL_HDR = 16
CONV_HDR = 8
DOT_COLS = 512
ROW_BLOCK = 16
CONV_ROW_BLOCK = 32
VMEM_LIMIT_BYTES = 58 * 1024 * 1024

COL_UP = 0
COL_ZP = D_POOL
COL_ZS = 2 * D_POOL
COL_XBC = COL_ZS + D_INNER
COL_DT = COL_XBC + D_XBC
COL_G = COL_DT + LANES
N_WEIGHTS = 16


def _dot(a, b):
    return jnp.dot(a, b, preferred_element_type=F32)


def _dot_nt(a, b):
    return lax.dot_general(a, b, (((1,), (1,)), ((), ())), preferred_element_type=F32)


def _pack3(v, lane):
    hi = v.astype(BF16).astype(F32)
    r = v - hi
    mid = r.astype(BF16).astype(F32)
    lo = r - mid
    packed = jnp.where(lane < N_HEADS, hi,
                       jnp.where(lane < 2 * N_HEADS, mid,
                                 jnp.where(lane < 3 * N_HEADS, lo, 0.0)))
    return packed.astype(BF16)


def _silu(v):
    return v / (1.0 + jnp.exp(-v))


def _sigmoid(v):
    return 1.0 / (1.0 + jnp.exp(-v))


def _softplus(v):
    return jnp.maximum(v, 0.0) + jnp.log1p(jnp.exp(-jnp.abs(v)))


def _adaln_kernel(c_ref, w_ref, b_ref, o_ref):
    c = c_ref[...]
    o_ref[...] = _dot(_silu(c).astype(BF16), w_ref[...]) + b_ref[...]


def _weight_refs(it):
    names = ("normg", "wcat", "convw", "convb", "dtb", "alog", "dskip", "ssmg", "poolw",
             "pscale", "wpo", "wso", "wo", "fg", "ta", "e3")
    assert len(names) == N_WEIGHTS
    return types.SimpleNamespace(**{n: next(it) for n in names})


def _row_blocks(S, L, nrows):
    if L >= nrows:
        return [(slice(s, s + 1), slice(l0, l0 + nrows), slice(s * L + l0, s * L + l0 + nrows))
                for s in range(S) for l0 in range(0, L, nrows)]
    sb = nrows // L
    return [(slice(s0, s0 + sb), slice(0, L), slice(s0 * L, (s0 + sb) * L))
            for s0 in range(0, S, sb)]


def _norm_modulate(w, x_ref, mod_of, h_ref, S, L):
    for ss, ls, rs in _row_blocks(S, L, ROW_BLOCK):
        x3 = x_ref[ss, ls, :]
        mod3 = mod_of(ss)
        ms = jnp.mean(x3 * x3, axis=-1, keepdims=True)
        hn = x3 * lax.rsqrt(ms + EPS) * w.normg[...]
        hm = hn * (1.0 + mod3[:, :, D_MODEL:2 * D_MODEL]) + mod3[:, :, 0:D_MODEL]
        h_ref[rs, :] = hm.reshape(rs.stop - rs.start, D_MODEL).astype(BF16)


def _proj_chunks(col0, ncols):
    return [(col0 + j, min(DOT_COLS, ncols - j), j) for j in range(0, ncols, DOT_COLS)]


def _pool_branch(w, up3, zp, t, pos0, S, L, fill=lambda n: None):
    for ss, ls, rs in _row_blocks(S, L, ROW_BLOCK):
        nseq, ntok = ss.stop - ss.start, ls.stop - ls.start
        pos = pos0 + ls.start + lax.broadcasted_iota(jnp.int32, (nseq, ntok, POOL_GROUP), 1)
        for gi, win in enumerate(POOL_WINDOWS):
            sl = slice(gi * POOL_GROUP, (gi + 1) * POOL_GROUP)
            u = up3[ss, POOL_HDR + ls.start:POOL_HDR + ls.stop, sl]
            s = u
            for k in range(1, win):
                s = s + up3[ss, POOL_HDR + ls.start - k:POOL_HDR + ls.stop - k, sl]
            cnt = jnp.minimum(pos + 1, win).astype(F32)
            t.pm[rs, sl] = (s / cnt - u).reshape(nseq * ntok, POOL_GROUP).astype(BF16)
    fill(2)
    for gi in range(len(POOL_WINDOWS)):
        sl = slice(gi * POOL_GROUP, (gi + 1) * POOL_GROUP)
        yg = _dot(t.pm[:, sl], w.poolw[gi]) * w.pscale[:, sl]
        t.pm[:, sl] = (yg * _silu(zp[:, sl])).astype(BF16)
    for j in range(0, D_MODEL, DOT_COLS):
        t.p[:, j:j + DOT_COLS] = _dot(t.pm[...], w.wpo[:, j:j + DOT_COLS])


def _conv_chunk(w, xbc3, t, j, S, L):
    sl = slice(j, j + DOT_COLS)
    for ss, ls, rs in _row_blocks(S, L, CONV_ROW_BLOCK):
        acc = w.convb[:, sl]
        for k in range(CONV_W):
            r0 = CONV_HDR - (CONV_W - 1) + k
            acc = acc + xbc3[ss, r0 + ls.start:r0 + ls.stop, sl] * w.convw[k:k + 1, sl]
        v = _silu(acc).reshape(rs.stop - rs.start, DOT_COLS)
        if j < D_INNER:
            t.xs[rs, sl] = v
        elif j < D_INNER + N_BC_GROUPS * D_STATE:
            t.b[rs, :] = v
        else:
            t.c[rs, :] = v


def _ssd_intra(w, t, dt_ref, c, QL, prompt, fill=lambda n: None):
    row_start = c * CHUNK
    rows = pl.ds(row_start, CHUNK)
    lane = lax.broadcasted_iota(jnp.int32, (CHUNK, LANES), 1)
    row = lax.broadcasted_iota(jnp.int32, (CHUNK, LANES), 0)
    neg_a = jnp.where(lane[0:1, :] < 3 * N_HEADS, -jnp.exp(w.alog[...]), 0.0)
    dtc = dt_ref[rows, :]
    a = dtc * neg_a
    cs = _dot(w.ta[...], _pack3(a, lane))
    cs = cs + pltpu.roll(cs, 2 * N_HEADS, 1)
    cs = cs + pltpu.roll(cs, N_HEADS, 1)
    acs = cs[0:CHUNK, :]
    last = cs[CHUNK:2 * CHUNK, :]
    exp_acs = jnp.exp(acs)
    wgt = jnp.exp(last - acs) * dtc
    t.ea[...] = _dot(_pack3(exp_acs, lane), w.e3[...])
    t.we[...] = _dot(_pack3(wgt, lane), w.e3[...])
    if not prompt:
        t.aux[...] = jnp.exp(last).T
    acs_t = acs.T
    dt_t = dtc.T
    ql_bits = QL.bit_length() - 1
    mask = ((lax.shift_right_logical(row, ql_bits) == lax.shift_right_logical(lane, ql_bits))
            & (lane <= row))
    for g in range(N_BC_GROUPS):
        gs = slice(g * D_STATE, (g + 1) * D_STATE)
        gc = slice(g * GROUP_COLS, (g + 1) * GROUP_COLS)
        cg = t.c[rows, gs].astype(BF16)
        cb = _dot_nt(cg, t.b[rows, gs].astype(BF16))
        if prompt:
            ht = t.aux[:, gc]
            t.yoff[...] = _dot(cg, ht.astype(BF16))
            xw = (t.xs[rows, gc] * t.we[:, gc]).astype(BF16)
            upd = _dot(t.b[rows, gs].T.astype(BF16), xw)
            t.aux[:, gc] = ht * t.ea[CHUNK - 1:CHUNK, gc] + upd
        for k in range(HEADS_PER_GROUP // 2):
            atts = []
            for h in (g * HEADS_PER_GROUP + 2 * k, g * HEADS_PER_GROUP + 2 * k + 1):
                seg = acs[:, h:h + 1] - acs_t[h:h + 1, :]
                decay = jnp.exp(jnp.where(mask, seg, -jnp.inf))
                atts.append((cb * decay * dt_t[h:h + 1, :]).astype(BF16))
            ps = slice((g * HEADS_PER_GROUP + 2 * k) * HEAD_DIM,
                       (g * HEADS_PER_GROUP + 2 * k + 2) * HEAD_DIM)
            xp = t.xs[rows, ps]
            rhs = jnp.concatenate([jnp.where(lane < HEAD_DIM, xp, 0.0).astype(BF16),
                                   jnp.where(lane >= HEAD_DIM, xp, 0.0).astype(BF16)], axis=0)
            if g % 2 == 0 and k == HEADS_PER_GROUP // 2 - 1:
                fill(1)
            yp = _dot(jnp.concatenate(atts, axis=1), rhs) + w.dskip[:, ps] * xp
            if prompt:
                yp = yp + t.yoff[:, 2 * k * HEAD_DIM:(2 * k + 2) * HEAD_DIM] * t.ea[:, ps]
            t.yssd[rows, ps] = yp


def _sample_state(t, r0, QL, hin, hout):
    rows = pl.ds(0, CHUNK)
    lane = lax.broadcasted_iota(jnp.int32, (CHUNK, LANES), 1)
    col = jnp.sum(jnp.where(lane == r0, t.aux[...], 0.0), axis=1, keepdims=True)
    rb = jnp.broadcast_to(col, (CHUNK, LANES))
    rowi = lax.broadcasted_iota(jnp.int32, (CHUNK, GROUP_COLS), 0)
    rmask = (rowi >= r0) & (rowi < r0 + QL)
    for g in range(N_BC_GROUPS):
        gs = slice(g * D_STATE, (g + 1) * D_STATE)
        cs = slice(g * GROUP_COLS, (g + 1) * GROUP_COLS)
        hg = hin[cs, :]
        yoff = _dot_nt(t.c[rows, gs].astype(BF16), hg.astype(BF16)) * t.ea[:, cs]
        t.yssd[rows, cs] = t.yssd[rows, cs] + jnp.where(rmask, yoff, 0.0)
        xw = jnp.where(rmask, t.xs[rows, cs] * t.we[:, cs], 0.0)
        upd = _dot(xw.T.astype(BF16), t.b[rows, gs].astype(BF16))
        for e in range(HEADS_PER_GROUP):
            h = g * HEADS_PER_GROUP + e
            hs = slice(h * HEAD_DIM, (h + 1) * HEAD_DIM)
            es = slice(e * HEAD_DIM, (e + 1) * HEAD_DIM)
            hout[hs, :] = hg[es, :] * rb[h:h + 1, :] + upd[es, :]


def _dense_out(w, t, x_ref, gate_of, y_ref, S, L, fill=lambda n: None, fill_g=lambda: None):
    R = S * L
    col_chunks = [slice(j, j + DOT_COLS) for j in range(0, D_INNER, DOT_COLS)]
    for r in range(0, R, ROW_BLOCK):
        rs = slice(r, r + ROW_BLOCK)
        ys = [t.yssd[rs, sl] * _silu(t.zs[rs, sl]) for sl in col_chunks]
        ssq = sum(jnp.sum(y * y, axis=-1, keepdims=True) for y in ys)
        inv = lax.rsqrt(ssq * (1.0 / D_INNER) + EPS)
        for y, sl in zip(ys, col_chunks):
            t.yn[rs, sl] = (y * inv * w.ssmg[:, sl]).astype(BF16)
    fill(1)
    for j in range(0, D_MODEL, DOT_COLS):
        t.s[:, j:j + DOT_COLS] = _dot(t.yn[...], w.wso[:, j:j + DOT_COLS])
    fill_g()
    for r in range(0, R, ROW_BLOCK):
        rs = slice(r, r + ROW_BLOCK)
        for j in range(0, D_MODEL, DOT_COLS):
            sl = slice(j, j + DOT_COLS)
            sl2 = slice(D_MODEL + j, D_MODEL + j + DOT_COLS)
            t.yn[rs, sl] = (_sigmoid(t.g[rs, sl]) * t.p[rs, sl]
                            + _sigmoid(t.g[rs, sl2]) * t.s[rs, sl]).astype(BF16)
    for j in range(0, D_MODEL, DOT_COLS):
        t.s[:, j:j + DOT_COLS] = _dot(t.yn[:, 0:D_MODEL], w.wo[:, j:j + DOT_COLS])
    for ss, ls, rs in _row_blocks(S, L, ROW_BLOCK):
        o3 = t.s[rs, :].reshape(ss.stop - ss.start, ls.stop - ls.start, D_MODEL)
        xn = x_ref[ss, ls, :] + gate_of(ss) * o3
        ms = jnp.mean(xn * xn, axis=-1, keepdims=True)
        y_ref[ss, ls, :] = xn * lax.rsqrt(ms + EPS) * w.fg[...]


def _ssd_refs(it):
    names = ("xs", "b", "c", "yssd", "pm", "p", "ea", "we", "aux", "yn", "s", "zs")
    return types.SimpleNamespace(**{n: next(it) for n in names})


def _prompt_kernel(*refs, L, NT, n_tiles):
    S = 1
    it = iter(refs)
    x_ref, xprev_ref, mod_ref = next(it), next(it), next(it)
    w = _weight_refs(it)
    y_ref, pool_out_ref, conv_out_ref, ssm_out_ref = [next(it) for _ in range(4)]
    sets = [types.SimpleNamespace(h=next(it), up=next(it), zp=next(it), xbc=next(it), dt=next(it))
            for _ in range(2)]
    t = _ssd_refs(it)
    t.yoff = next(it)
    t.g = t.yssd

    i = pl.program_id(0)
    tile_a = jnp.minimum(i, n_tiles - 1)
    tile_b = jnp.maximum(i - 1, 0)
    batch_a = tile_a // NT
    batch_b = tile_b // NT
    t_b = tile_b % NT

    def always(k):
        return jnp.minimum(i, -k) == -k

    def body(A, B):
        @pl.when(i == 0)
        def _():
            for ref in (B.h, B.up, B.zp, B.xbc, B.dt):
                ref[...] = jnp.zeros(ref.shape, ref.dtype)

        @pl.when(t_b == 0)
        def _():
            B.up[:, 0:POOL_HDR, :] = jnp.zeros((S, POOL_HDR, D_POOL), F32)
            B.xbc[:, 0:CONV_HDR, :] = jnp.zeros((S, CONV_HDR, D_XBC), F32)
            t.aux[...] = jnp.zeros(t.aux.shape, F32)

        def store_up(j, wd, v):
            A.up[:, POOL_HDR:POOL_HDR + L, j:j + wd] = v.reshape(S, L, wd)

        def store_zp(j, wd, v):
            A.zp[:, j:j + wd] = v

        def store_xbc(j, wd, v):
            A.xbc[:, CONV_HDR:CONV_HDR + L, j:j + wd] = v.reshape(S, L, wd)

        def store_dt(j, wd, v):
            A.dt[...] = _softplus(v + w.dtb[...])

        def store_zs(j, wd, v):
            t.zs[:, j:j + wd] = v

        def store_g(j, wd, v):
            t.g[:, j:j + wd] = v

        def chunks(h_ref, col0, ncols, store):
            return [(h_ref, c, store) for c in _proj_chunks(col0, ncols)]

        fill = (chunks(B.h, COL_ZS, D_INNER, store_zs)
                + chunks(A.h, COL_XBC, D_XBC, store_xbc) + chunks(A.h, COL_UP, D_POOL, store_up)
                + chunks(A.h, COL_ZP, D_POOL, store_zp) + chunks(A.h, COL_DT, LANES, store_dt))

        def fill_g():
            for _, (col, wd, j), store in chunks(B.h, COL_G, 2 * D_MODEL, store_g):
                store(j, wd, _dot(B.h[...], w.wcat[:, col:col + wd]))

        def run_fill(n):
            for _ in range(n):
                if fill:
                    h_ref, (col, wd, j), store = fill.pop(0)
                    store(j, wd, _dot(h_ref[...], w.wcat[:, col:col + wd]))

        @pl.when(always(0))
        def _():
            _norm_modulate(w, x_ref, lambda ss: mod_ref[pl.ds(batch_a, 1)], A.h, S, L)
            _pool_branch(w, B.up, B.zp, t, t_b * L, S, L, run_fill)
            pool_out_ref[...] = B.up[:, L + 1:L + POOL_HDR, :]
            conv_out_ref[...] = B.xbc[:, L + CONV_HDR - (CONV_W - 1):L + CONV_HDR, :]

        for j in range(0, D_XBC, DOT_COLS):
            @pl.when(always(1 + j // DOT_COLS))
            def _(j=j):
                _conv_chunk(w, B.xbc, t, j, S, L)
                run_fill(1)

        n_chunks = L // CHUNK
        for c in range(n_chunks):
            @pl.when(always(1 + D_XBC // DOT_COLS + c))
            def _(c=c):
                _ssd_intra(w, t, B.dt, c, CHUNK, True, run_fill)

        @pl.when(always(1 + D_XBC // DOT_COLS + n_chunks))
        def _():
            _dense_out(w, t, xprev_ref,
                       lambda ss: mod_ref[pl.ds(batch_b, 1)][:, :, 2 * D_MODEL:3 * D_MODEL],
                       y_ref, S, L, run_fill, fill_g)
            assert not fill
            A.up[:, 0:POOL_HDR, :] = B.up[:, L:L + POOL_HDR, :]
            A.xbc[:, 0:CONV_HDR, :] = B.xbc[:, L:L + CONV_HDR, :]

        @pl.when(t_b == NT - 1)
        def _():
            for n in range(D_INNER // LANES):
                cols = slice(n * LANES, (n + 1) * LANES)
                ssm_out_ref[0, cols, :] = t.aux[:, cols].T

    @pl.when(i % 2 == 0)
    def _():
        body(sets[0], sets[1])

    @pl.when(i % 2 == 1)
    def _():
        body(sets[1], sets[0])


def _sample_kernel(*refs, S, L, n_inner, seq_per_inner):
    it = iter(refs)
    x_ref, mod_ref, pool_in_ref, conv_in_ref, ssm_in_ref = [next(it) for _ in range(5)]
    w = _weight_refs(it)
    y_ref, pool_out_ref, conv_out_ref, ssm_out_ref = [next(it) for _ in range(4)]
    h_scr, up_scr, zp_scr, xbc_scr, dt_scr = [next(it) for _ in range(5)]
    t = _ssd_refs(it)
    t.g = next(it)
    step = pl.program_id(1)

    @pl.when(step == 0)
    def _():
        _norm_modulate(w, x_ref, lambda ss: mod_ref[ss], h_scr, S, L)

        def store_up(j, wd, v):
            up_scr[:, POOL_HDR:POOL_HDR + L, j:j + wd] = v.reshape(S, L, wd)

        def store_zp(j, wd, v):
            zp_scr[:, j:j + wd] = v

        def store_zs(j, wd, v):
            t.zs[:, j:j + wd] = v

        def store_xbc(j, wd, v):
            xbc_scr[:, CONV_HDR:CONV_HDR + L, j:j + wd] = v.reshape(S, L, wd)

        def store_dt(j, wd, v):
            dt_scr[...] = _softplus(v + w.dtb[...])

        def store_g(j, wd, v):
            t.g[:, j:j + wd] = v

        for col0, ncols, store in ((COL_UP, D_POOL, store_up), (COL_ZP, D_POOL, store_zp),
                                   (COL_ZS, D_INNER, store_zs), (COL_XBC, D_XBC, store_xbc),
                                   (COL_DT, LANES, store_dt), (COL_G, 2 * D_MODEL, store_g)):
            for col, wd, j in _proj_chunks(col0, ncols):
                store(j, wd, _dot(h_scr[...], w.wcat[:, col:col + wd]))

        up_scr[:, 1:POOL_HDR, :] = pool_in_ref[...]
        _pool_branch(w, up_scr, zp_scr, t, PAST_LEN, S, L)
        pool_out_ref[...] = up_scr[:, L + 1:L + POOL_HDR, :]

        xbc_scr[:, CONV_HDR - (CONV_W - 1):CONV_HDR, :] = conv_in_ref[...]
        for j in range(0, D_XBC, DOT_COLS):
            _conv_chunk(w, xbc_scr, t, j, S, L)
        conv_out_ref[...] = xbc_scr[:, L + CONV_HDR - (CONV_W - 1):L + CONV_HDR, :]
        _ssd_intra(w, t, dt_scr, 0, L, False)

    for q in range(seq_per_inner):
        r0 = (step * seq_per_inner + q) * L
        _sample_state(t, r0, L, ssm_in_ref.at[q], ssm_out_ref.at[q])

    @pl.when(step == n_inner - 1)
    def _():
        _dense_out(w, t, x_ref, lambda ss: mod_ref[ss][:, :, 2 * D_MODEL:3 * D_MODEL],
                   y_ref, S, L)


def _resident(shape):
    zeros = (0,) * len(shape)
    return pl.BlockSpec(shape, lambda *_: zeros, pipeline_mode=pl.Buffered(1))


def _ssd_scratch(R, aux_shape):
    return [
        pltpu.VMEM((R, D_INNER), F32),
        pltpu.VMEM((R, N_BC_GROUPS * D_STATE), F32),
        pltpu.VMEM((R, N_BC_GROUPS * D_STATE), F32),
        pltpu.VMEM((R, D_INNER), F32),
        pltpu.VMEM((R, D_POOL), BF16),
        pltpu.VMEM((R, D_MODEL), F32),
        pltpu.VMEM((CHUNK, D_INNER), F32),
        pltpu.VMEM((CHUNK, D_INNER), F32),
        pltpu.VMEM(aux_shape, F32),
        pltpu.VMEM((R, D_INNER), BF16),
        pltpu.VMEM((R, D_MODEL), F32),
        pltpu.VMEM((R, D_INNER), F32),
    ]


def _prompt_call(x, mod, weights, *, L):
    nb, seq, _ = x.shape
    NT = seq // L
    n_tiles = nb * NT
    tile_a = lambda i: jnp.minimum(i, n_tiles - 1)
    tile_b = lambda i: jnp.maximum(i - 1, 0)
    x_spec_a = pl.BlockSpec((1, L, D_MODEL), lambda i: (tile_a(i) // NT, tile_a(i) % NT, 0))
    x_spec_b = pl.BlockSpec((1, L, D_MODEL), lambda i: (tile_b(i) // NT, tile_b(i) % NT, 0))
    seq_map_b = lambda i: (tile_b(i) // NT, 0, 0)
    in_specs = [x_spec_a, x_spec_b, _resident(mod.shape)] + [_resident(v.shape) for v in weights]
    out_shape = (jax.ShapeDtypeStruct(x.shape, F32),
                 jax.ShapeDtypeStruct((nb, POOL_BUF, D_POOL), F32),
                 jax.ShapeDtypeStruct((nb, CONV_W - 1, D_XBC), F32),
                 jax.ShapeDtypeStruct((nb, D_INNER, D_STATE), F32))
    out_specs = (x_spec_b,
                 pl.BlockSpec((1, POOL_BUF, D_POOL), seq_map_b),
                 pl.BlockSpec((1, CONV_W - 1, D_XBC), seq_map_b),
                 pl.BlockSpec((1, D_INNER, D_STATE), seq_map_b))
    stage_set = [
        pltpu.VMEM((L, D_MODEL), BF16),
        pltpu.VMEM((1, POOL_HDR + L, D_POOL), F32),
        pltpu.VMEM((L, D_POOL), F32),
        pltpu.VMEM((1, CONV_HDR + L, D_XBC), F32),
        pltpu.VMEM((L, LANES), F32),
    ]
    scratch = (stage_set + stage_set + _ssd_scratch(L, (D_STATE, D_INNER))
               + [pltpu.VMEM((CHUNK, GROUP_COLS), F32)])
    kern = functools.partial(_prompt_kernel, L=L, NT=NT, n_tiles=n_tiles)
    return pl.pallas_call(
        kern, grid=(n_tiles + 1,), in_specs=in_specs, out_specs=out_specs, out_shape=out_shape,
        scratch_shapes=scratch, name="layer_prompt",
        compiler_params=pltpu.CompilerParams(
            dimension_semantics=("arbitrary",), vmem_limit_bytes=VMEM_LIMIT_BYTES),
    )(x, x, mod, *weights)


def _sample_call(x, mod, states, weights, *, S, seq_per_inner):
    ns, L, _ = x.shape
    R = S * L
    n_inner = S // seq_per_inner
    seq_map = lambda o, i: (o, 0, 0)
    st_map = lambda o, i: (o * n_inner + i, 0, 0)
    x_spec = pl.BlockSpec((S, L, D_MODEL), seq_map)
    pool_spec = pl.BlockSpec((S, POOL_BUF, D_POOL), seq_map)
    conv_spec = pl.BlockSpec((S, CONV_W - 1, D_XBC), seq_map)
    st_spec = pl.BlockSpec((seq_per_inner, D_INNER, D_STATE), st_map)
    in_specs = ([x_spec, pl.BlockSpec((S, 1, 3 * D_MODEL), seq_map), pool_spec, conv_spec, st_spec]
                + [_resident(v.shape) for v in weights])
    out_shape = (jax.ShapeDtypeStruct(x.shape, F32),
                 jax.ShapeDtypeStruct((ns, POOL_BUF, D_POOL), F32),
                 jax.ShapeDtypeStruct((ns, CONV_W - 1, D_XBC), F32),
                 jax.ShapeDtypeStruct((ns, D_INNER, D_STATE), F32))
    out_specs = (x_spec, pool_spec, conv_spec, st_spec)
    scratch = [
        pltpu.VMEM((R, D_MODEL), BF16),
        pltpu.VMEM((S, POOL_HDR + L, D_POOL), F32),
        pltpu.VMEM((R, D_POOL), F32),
        pltpu.VMEM((S, CONV_HDR + L, D_XBC), F32),
        pltpu.VMEM((R, LANES), F32),
    ] + _ssd_scratch(R, (LANES, CHUNK)) + [
        pltpu.VMEM((R, 2 * D_MODEL), F32),
    ]
    kern = functools.partial(_sample_kernel, S=S, L=L, n_inner=n_inner,
                             seq_per_inner=seq_per_inner)
    return pl.pallas_call(
        kern, grid=(ns // S, n_inner), in_specs=in_specs, out_specs=out_specs,
        out_shape=out_shape, scratch_shapes=scratch, name="layer_sample",
        compiler_params=pltpu.CompilerParams(
            dimension_semantics=("arbitrary", "arbitrary"), vmem_limit_bytes=VMEM_LIMIT_BYTES),
    )(x, mod, *states, *weights)


def _block_sum_matrix(ql):
    r = jnp.arange(CHUNK)[:, None]
    c = jnp.arange(CHUNK)[None, :]
    same = (r // ql) == (c // ql)
    return jnp.concatenate([same & (c <= r), same], axis=0).astype(BF16)


def _expand_matrix():
    k = jnp.arange(LANES)[:, None]
    c = jnp.arange(D_INNER)[None, :] // HEAD_DIM
    return ((k % N_HEADS == c) & (k < 3 * N_HEADS)).astype(BF16)


def kernel(x_prompt, x_sample, state_pool, state_conv, state_ssm, c_prompt, c_sample, w_ada, b_ada, norm_g, w_in, conv_w, conv_b, dt_bias, a_log, d_skip, ssm_norm_g, pool_w, pool_scale, w_pool_out, w_ssm_out, w_o, final_g):
    depth = w_ada.shape[0]
    assert depth == 1, "kernel is written for a single layer"
    nb, seq, _ = x_prompt.shape
    ns, dec_seq, _ = x_sample.shape
    l = 0

    def head_lanes(v):
        return jnp.pad(jnp.tile(v, (1, 3)), ((0, 0), (0, LANES - 3 * N_HEADS)))

    c_all = jnp.concatenate([c_prompt, c_sample], axis=0)
    mod = pl.pallas_call(
        _adaln_kernel, out_shape=jax.ShapeDtypeStruct((nb + ns, 3 * D_MODEL), F32), name="adaln",
        compiler_params=pltpu.CompilerParams(vmem_limit_bytes=VMEM_LIMIT_BYTES),
    )(c_all, w_ada[l].astype(BF16), b_ada[l][None, :])
    mod = mod[:, None, :]

    e3 = _expand_matrix()
    w_cat = jnp.concatenate(
        [w_in[l][:, :COL_DT], head_lanes(w_in[l][:, COL_DT:COL_DT + N_HEADS]),
         w_in[l][:, COL_DT + N_HEADS:]], axis=1).astype(BF16)

    def weights(ql):
        return [
            norm_g[l][None, :],
            w_cat,
            conv_w[l], conv_b[l][None, :],
            head_lanes(dt_bias[l][None, :]),
            head_lanes(a_log[l][None, :]),
            jnp.repeat(d_skip[l], HEAD_DIM)[None, :],
            ssm_norm_g[l][None, :],
            pool_w[l].astype(BF16), pool_scale[l][None, :],
            w_pool_out[l].astype(BF16), w_ssm_out[l].astype(BF16), w_o[l].astype(BF16),
            final_g[None, :],
            _block_sum_matrix(ql), e3,
        ]

    y_p, pool_p, conv_p, ssm_p = _prompt_call(x_prompt, mod[:nb], weights(CHUNK), L=2 * CHUNK)

    states = (state_pool[l], state_conv[l], state_ssm[l].reshape(ns, D_INNER, D_STATE))
    y_s, pool_s, conv_s, ssm_s = _sample_call(
        x_sample, mod[nb:], states, weights(dec_seq), S=CHUNK // dec_seq, seq_per_inner=2)

    shp = (N_HEADS, HEAD_DIM, D_STATE)
    return (y_p, y_s, pool_p[None], conv_p[None], ssm_p.reshape((1, nb) + shp),
            pool_s[None], conv_s[None], ssm_s.reshape((1, ns) + shp))
```

```python
import functools
import types

import jax
import jax.numpy as jnp
from jax import lax
from jax.experimental import pallas as pl
from jax.experimental.pallas import tpu as pltpu

F32 = jnp.float32
BF16 = jnp.bfloat16

EPS = 1e-6
D_MODEL = 1024
POOL_WINDOWS = (2, 4, 8, 16)
POOL_GROUP = 128
D_POOL = POOL_GROUP * len(POOL_WINDOWS)
POOL_BUF = max(POOL_WINDOWS) - 1
D_INNER = 2048
HEAD_DIM = 64
N_HEADS = D_INNER // HEAD_DIM
N_BC_GROUPS = 4
HEADS_PER_GROUP = N_HEADS // N_BC_GROUPS
GROUP_COLS = HEADS_PER_GROUP * HEAD_DIM
D_STATE = 128
CONV_W = 4
D_XBC = D_INNER + 2 * N_BC_GROUPS * D_STATE
PAST_LEN = 16384

LANES = 128
CHUNK = 128
POOL_HDR = 16
CONV_HDR = 8
DOT_COLS = 512
ROW_BLOCK = 16
CONV_ROW_BLOCK = 32
VMEM_LIMIT_BYTES = 58 * 1024 * 1024

COL_UP = 0
COL_ZP = D_POOL
COL_ZS = 2 * D_POOL
COL_XBC = COL_ZS + D_INNER
COL_DT = COL_XBC + D_XBC
COL_G = COL_DT + LANES
N_WEIGHTS = 16


def _dot(a, b):
    return jnp.dot(a, b, preferred_element_type=F32)


def _dot_nt(a, b):
    return lax.dot_general(a, b, (((1,), (1,)), ((), ())), preferred_element_type=F32)


def _pack3(v, lane):
    hi = v.astype(BF16).astype(F32)
    r = v - hi
    mid = r.astype(BF16).astype(F32)
    lo = r - mid
    packed = jnp.where(lane < N_HEADS, hi,
                       jnp.where(lane < 2 * N_HEADS, mid,
                                 jnp.where(lane < 3 * N_HEADS, lo, 0.0)))
    return packed.astype(BF16)


def _silu(v):
    return v / (1.0 + jnp.exp(-v))


def _sigmoid(v):
    return 1.0 / (1.0 + jnp.exp(-v))


def _softplus(v):
    return jnp.maximum(v, 0.0) + jnp.log1p(jnp.exp(-jnp.abs(v)))


def _adaln_kernel(c_ref, w_ref, b_ref, o_ref):
    c = c_ref[...]
    o_ref[...] = _dot(_silu(c).astype(BF16), w_ref[...]) + b_ref[...]


def _weight_refs(it):
    names = ("normg", "wcat", "convw", "convb", "dtb", "alog", "dskip", "ssmg", "poolw",
             "pscale", "wpo", "wso", "wo", "fg", "ta", "e3")
    assert len(names) == N_WEIGHTS
    return types.SimpleNamespace(**{n: next(it) for n in names})


def _row_blocks(S, L, nrows):
    if L >= nrows:
        return [(slice(s, s + 1), slice(l0, l0 + nrows), slice(s * L + l0, s * L + l0 + nrows))
                for s in range(S) for l0 in range(0, L, nrows)]
    sb = nrows // L
    return [(slice(s0, s0 + sb), slice(0, L), slice(s0 * L, (s0 + sb) * L))
            for s0 in range(0, S, sb)]


def _norm_modulate(w, x_ref, mod_of, h_ref, S, L):
    for ss, ls, rs in _row_blocks(S, L, ROW_BLOCK):
        x3 = x_ref[ss, ls, :]
        mod3 = mod_of(ss)
        ms = jnp.mean(x3 * x3, axis=-1, keepdims=True)
        hn = x3 * lax.rsqrt(ms + EPS) * w.normg[...]
        hm = hn * (1.0 + mod3[:, :, D_MODEL:2 * D_MODEL]) + mod3[:, :, 0:D_MODEL]
        h_ref[rs, :] = hm.reshape(rs.stop - rs.start, D_MODEL).astype(BF16)


def _proj_chunks(col0, ncols):
    return [(col0 + j, min(DOT_COLS, ncols - j), j) for j in range(0, ncols, DOT_COLS)]


def _pool_windows(up3, t, pos0, S, L):
    for ss, ls, rs in _row_blocks(S, L, ROW_BLOCK):
        nseq, ntok = ss.stop - ss.start, ls.stop - ls.start
        pos = pos0 + ls.start + lax.broadcasted_iota(jnp.int32, (nseq, ntok, POOL_GROUP), 1)
        for gi, win in enumerate(POOL_WINDOWS):
            sl = slice(gi * POOL_GROUP, (gi + 1) * POOL_GROUP)
            u = up3[ss, POOL_HDR + ls.start:POOL_HDR + ls.stop, sl]
            s = u
            for k in range(1, win):
                s = s + up3[ss, POOL_HDR + ls.start - k:POOL_HDR + ls.stop - k, sl]
            cnt = jnp.minimum(pos + 1, win).astype(F32)
            t.pm[rs, sl] = (s / cnt - u).reshape(nseq * ntok, POOL_GROUP).astype(BF16)


def _pool_mix(w, zp, t):
    for gi in range(len(POOL_WINDOWS)):
        sl = slice(gi * POOL_GROUP, (gi + 1) * POOL_GROUP)
        yg = _dot(t.pm[:, sl], w.poolw[gi]) * w.pscale[:, sl]
        t.pm[:, sl] = (yg * _silu(zp[:, sl])).astype(BF16)
    for j in range(0, D_MODEL, DOT_COLS):
        t.p[:, j:j + DOT_COLS] = _dot(t.pm[...], w.wpo[:, j:j + DOT_COLS])


def _conv_chunk(w, xbc3, t, j, S, L):
    sl = slice(j, j + DOT_COLS)
    for ss, ls, rs in _row_blocks(S, L, CONV_ROW_BLOCK):
        acc = w.convb[:, sl]
        for k in range(CONV_W):
            r0 = CONV_HDR - (CONV_W - 1) + k
            acc = acc + xbc3[ss, r0 + ls.start:r0 + ls.stop, sl] * w.convw[k:k + 1, sl]
        v = _silu(acc).reshape(rs.stop - rs.start, DOT_COLS)
        if j < D_INNER:
            t.xs[rs, sl] = v
        elif j < D_INNER + N_BC_GROUPS * D_STATE:
            t.b[rs, :] = v
        else:
            t.c[rs, :] = v


def _ssd_prep(w, t, dt_ref, c, prompt):
    rows = pl.ds(c * CHUNK, CHUNK)
    lane = lax.broadcasted_iota(jnp.int32, (CHUNK, LANES), 1)
    neg_a = jnp.where(lane[0:1, :] < 3 * N_HEADS, -jnp.exp(w.alog[...]), 0.0)
    dtc = dt_ref[rows, :]
    a = dtc * neg_a
    cs = _dot(w.ta[...], _pack3(a, lane))
    cs = cs + pltpu.roll(cs, 2 * N_HEADS, 1)
    cs = cs + pltpu.roll(cs, N_HEADS, 1)
    acs = cs[0:CHUNK, :]
    last = cs[CHUNK:2 * CHUNK, :]
    exp_acs = jnp.exp(acs)
    wgt = jnp.exp(last - acs) * dtc
    t.ea[rows, :] = _dot(_pack3(exp_acs, lane), w.e3[...])
    t.we[rows, :] = _dot(_pack3(wgt, lane), w.e3[...])
    if not prompt:
        t.aux[...] = jnp.exp(last).T
    t.hd[c, 0] = acs
    t.hd[c, 1] = acs.T
    t.hd[c, 2] = dtc.T


def _ssd_intra(w, t, c, QL, prompt, fill=lambda n: None):
    row_start = c * CHUNK
    rows = pl.ds(row_start, CHUNK)
    lane = lax.broadcasted_iota(jnp.int32, (CHUNK, LANES), 1)
    row = lax.broadcasted_iota(jnp.int32, (CHUNK, LANES), 0)
    acs, acs_t, dt_t = t.hd[c, 0], t.hd[c, 1], t.hd[c, 2]
    ql_bits = QL.bit_length() - 1
    mask = ((lax.shift_right_logical(row, ql_bits) == lax.shift_right_logical(lane, ql_bits))
            & (lane <= row))
    for g in range(N_BC_GROUPS):
        gs = slice(g * D_STATE, (g + 1) * D_STATE)
        gc = slice(g * GROUP_COLS, (g + 1) * GROUP_COLS)
        cg = t.c[rows, gs].astype(BF16)
        cb = _dot_nt(cg, t.b[rows, gs].astype(BF16))
        if prompt:
            ht = t.aux[:, gc]
            t.yoff[...] = _dot(cg, ht.astype(BF16))
            xw = (t.xs[rows, gc] * t.we[rows, gc]).astype(BF16)
            upd = _dot(t.b[rows, gs].T.astype(BF16), xw)
            t.aux[:, gc] = ht * t.ea[row_start + CHUNK - 1:row_start + CHUNK, gc] + upd
        for k in range(HEADS_PER_GROUP // 2):
            atts = []
            for h in (g * HEADS_PER_GROUP + 2 * k, g * HEADS_PER_GROUP + 2 * k + 1):
                seg = acs[:, h:h + 1] - acs_t[h:h + 1, :]
                decay = jnp.exp(jnp.where(mask, seg, -jnp.inf))
                atts.append((cb * decay * dt_t[h:h + 1, :]).astype(BF16))
            ps = slice((g * HEADS_PER_GROUP + 2 * k) * HEAD_DIM,
                       (g * HEADS_PER_GROUP + 2 * k + 2) * HEAD_DIM)
            xp = t.xs[rows, ps]
            rhs = jnp.concatenate([jnp.where(lane < HEAD_DIM, xp, 0.0).astype(BF16),
                                   jnp.where(lane >= HEAD_DIM, xp, 0.0).astype(BF16)], axis=0)
            if g % 2 == 0 and k == HEADS_PER_GROUP // 2 - 1:
                fill(1)
            yp = _dot(jnp.concatenate(atts, axis=1), rhs) + w.dskip[:, ps] * xp
            if prompt:
                yp = yp + t.yoff[:, 2 * k * HEAD_DIM:(2 * k + 2) * HEAD_DIM] * t.ea[rows, ps]
            t.yssd[rows, ps] = yp


def _sample_state(t, r0, QL, hin, hout):
    rows = pl.ds(0, CHUNK)
    lane = lax.broadcasted_iota(jnp.int32, (CHUNK, LANES), 1)
    col = jnp.sum(jnp.where(lane == r0, t.aux[...], 0.0), axis=1, keepdims=True)
    rb = jnp.broadcast_to(col, (CHUNK, LANES))
    rowi = lax.broadcasted_iota(jnp.int32, (CHUNK, GROUP_COLS), 0)
    rmask = (rowi >= r0) & (rowi < r0 + QL)
    for g in range(N_BC_GROUPS):
        gs = slice(g * D_STATE, (g + 1) * D_STATE)
        cs = slice(g * GROUP_COLS, (g + 1) * GROUP_COLS)
        hg = hin[cs, :]
        yoff = _dot_nt(t.c[rows, gs].astype(BF16), hg.astype(BF16)) * t.ea[:, cs]
        t.yssd[rows, cs] = t.yssd[rows, cs] + jnp.where(rmask, yoff, 0.0)
        xw = jnp.where(rmask, t.xs[rows, cs] * t.we[:, cs], 0.0)
        upd = _dot(xw.T.astype(BF16), t.b[rows, gs].astype(BF16))
        for e in range(HEADS_PER_GROUP):
            h = g * HEADS_PER_GROUP + e
            hs = slice(h * HEAD_DIM, (h + 1) * HEAD_DIM)
            es = slice(e * HEAD_DIM, (e + 1) * HEAD_DIM)
            hout[hs, :] = hg[es, :] * rb[h:h + 1, :] + upd[es, :]


def _dense_out(w, t, x_ref, gate_of, y_ref, S, L, fill=lambda n: None):
    R = S * L
    col_chunks = [slice(j, j + DOT_COLS) for j in range(0, D_INNER, DOT_COLS)]
    for r in range(0, R, ROW_BLOCK):
        rs = slice(r, r + ROW_BLOCK)
        ys = [t.yssd[rs, sl] * _silu(t.zs[rs, sl]) for sl in col_chunks]
        ssq = sum(jnp.sum(y * y, axis=-1, keepdims=True) for y in ys)
        inv = lax.rsqrt(ssq * (1.0 / D_INNER) + EPS)
        for y, sl in zip(ys, col_chunks):
            t.yn[rs, sl] = (y * inv * w.ssmg[:, sl]).astype(BF16)
    fill(2)
    for j in range(0, D_MODEL, DOT_COLS):
        t.s[:, j:j + DOT_COLS] = _dot(t.yn[...], w.wso[:, j:j + DOT_COLS])
    for r in range(0, R, ROW_BLOCK):
        rs = slice(r, r + ROW_BLOCK)
        for j in range(0, D_MODEL, DOT_COLS):
            sl = slice(j, j + DOT_COLS)
            sl2 = slice(D_MODEL + j, D_MODEL + j + DOT_COLS)
            t.yn[rs, sl] = (_sigmoid(t.g[rs, sl]) * t.p[rs, sl]
                            + _sigmoid(t.g[rs, sl2]) * t.s[rs, sl]).astype(BF16)
    fill(2)
    for j in range(0, D_MODEL, DOT_COLS):
        t.s[:, j:j + DOT_COLS] = _dot(t.yn[:, 0:D_MODEL], w.wo[:, j:j + DOT_COLS])
    for ss, ls, rs in _row_blocks(S, L, ROW_BLOCK):
        o3 = t.s[rs, :].reshape(ss.stop - ss.start, ls.stop - ls.start, D_MODEL)
        xn = x_ref[ss, ls, :] + gate_of(ss) * o3
        ms = jnp.mean(xn * xn, axis=-1, keepdims=True)
        y_ref[ss, ls, :] = xn * lax.rsqrt(ms + EPS) * w.fg[...]
    fill(1)


def _ssd_refs(it):
    names = ("xs", "b", "c", "yssd", "pm", "p", "ea", "we", "hd", "aux", "yn", "s", "zs")
    return types.SimpleNamespace(**{n: next(it) for n in names})


def _prompt_kernel(*refs, L, NT, n_tiles):
    S = 1
    it = iter(refs)
    x_ref, xprev_ref, mod_ref = next(it), next(it), next(it)
    w = _weight_refs(it)
    y_ref, pool_out_ref, conv_out_ref, ssm_out_ref = [next(it) for _ in range(4)]
    h_scr, hn_scr, up_scr, zp_scr, xbc_scr, dt_scr = [next(it) for _ in range(6)]
    t = _ssd_refs(it)
    t.yoff, t.g = next(it), next(it)

    i = pl.program_id(0)
    tile_a = jnp.minimum(i, n_tiles - 1)
    tile_b = jnp.maximum(i - 1, 0)
    batch_a = tile_a // NT
    batch_b = tile_b // NT
    t_b = tile_b % NT

    def always(k):
        return jnp.minimum(i, -k) == -k

    @pl.when(i == 0)
    def _():
        for ref in (h_scr, up_scr, zp_scr, xbc_scr, dt_scr):
            ref[...] = jnp.zeros(ref.shape, ref.dtype)

    @pl.when(t_b == 0)
    def _():
        up_scr[:, 0:POOL_HDR, :] = jnp.zeros((S, POOL_HDR, D_POOL), F32)
        xbc_scr[:, 0:CONV_HDR, :] = jnp.zeros((S, CONV_HDR, D_XBC), F32)
        t.aux[...] = jnp.zeros(t.aux.shape, F32)

    def store_up(j, wd, v):
        up_scr[:, POOL_HDR:POOL_HDR + L, j:j + wd] = v.reshape(S, L, wd)

    def store_zp(j, wd, v):
        zp_scr[:, j:j + wd] = v

    def store_xbc(j, wd, v):
        xbc_scr[:, CONV_HDR:CONV_HDR + L, j:j + wd] = v.reshape(S, L, wd)

    def store_dt(j, wd, v):
        dt_scr[...] = _softplus(v + w.dtb[...])

    def store_zs(j, wd, v):
        t.zs[:, j:j + wd] = v

    def store_g(j, wd, v):
        t.g[:, j:j + wd] = v

    def chunks(h_ref, col0, ncols, store):
        return [(h_ref, c, store) for c in _proj_chunks(col0, ncols)]

    def run(chunk):
        h_ref, (col, wd, j), store = chunk
        store(j, wd, _dot(h_ref[...], w.wcat[:, col:col + wd]))

    zs_chunks = chunks(h_scr, COL_ZS, D_INNER, store_zs)
    xbc_chunks = chunks(hn_scr, COL_XBC, D_XBC, store_xbc)
    fill = (zs_chunks[:3] + xbc_chunks + zs_chunks[3:] + chunks(h_scr, COL_G, 2 * D_MODEL, store_g)
            + chunks(hn_scr, COL_UP, D_POOL, store_up) + chunks(hn_scr, COL_ZP, D_POOL, store_zp)
            + chunks(hn_scr, COL_DT, LANES, store_dt))

    def run_fill(n):
        for _ in range(n):
            if fill:
                run(fill.pop(0))

    n_chunks = L // CHUNK

    @pl.when(always(0))
    def _():
        _norm_modulate(w, x_ref, lambda ss: mod_ref[pl.ds(batch_a, 1)], hn_scr, S, L)
        _pool_windows(up_scr, t, t_b * L, S, L)
        pool_out_ref[...] = up_scr[:, L + 1:L + POOL_HDR, :]
        up_scr[:, 0:POOL_HDR, :] = up_scr[:, L:L + POOL_HDR, :]
        run_fill(2)

    @pl.when(always(100))
    def _():
        _pool_mix(w, zp_scr, t)
        for c in range(n_chunks):
            _ssd_prep(w, t, dt_scr, c, True)

    for j in range(0, D_XBC, DOT_COLS):
        @pl.when(always(1 + j // DOT_COLS))
        def _(j=j):
            sl = slice(j, j + DOT_COLS)
            _conv_chunk(w, xbc_scr, t, j, S, L)
            conv_out_ref[:, :, sl] = xbc_scr[:, L + CONV_HDR - (CONV_W - 1):L + CONV_HDR, sl]
            xbc_scr[:, 0:CONV_HDR, sl] = xbc_scr[:, L:L + CONV_HDR, sl]
            run_fill(1)

    for c in range(n_chunks):
        @pl.when(always(1 + D_XBC // DOT_COLS + c))
        def _(c=c):
            _ssd_intra(w, t, c, CHUNK, True, run_fill)

    @pl.when(always(1 + D_XBC // DOT_COLS + n_chunks))
    def _():
        _dense_out(w, t, xprev_ref,
                   lambda ss: mod_ref[pl.ds(batch_b, 1)][:, :, 2 * D_MODEL:3 * D_MODEL],
                   y_ref, S, L, run_fill)
        assert not fill
        h_scr[...] = hn_scr[...]

    @pl.when(t_b == NT - 1)
    def _():
        for n in range(D_INNER // LANES):
            cols = slice(n * LANES, (n + 1) * LANES)
            ssm_out_ref[0, cols, :] = t.aux[:, cols].T


def _sample_kernel(*refs, S, L, n_inner, seq_per_inner):
    it = iter(refs)
    x_ref, mod_ref, pool_in_ref, conv_in_ref, ssm_in_ref = [next(it) for _ in range(5)]
    w = _weight_refs(it)
    y_ref, pool_out_ref, conv_out_ref, ssm_out_ref = [next(it) for _ in range(4)]
    h_scr, up_scr, zp_scr, xbc_scr, dt_scr = [next(it) for _ in range(5)]
    t = _ssd_refs(it)
    t.g = next(it)
    step = pl.program_id(1)

    @pl.when(step == 0)
    def _():
        _norm_modulate(w, x_ref, lambda ss: mod_ref[ss], h_scr, S, L)

        def store_up(j, wd, v):
            up_scr[:, POOL_HDR:POOL_HDR + L, j:j + wd] = v.reshape(S, L, wd)

        def store_zp(j, wd, v):
            zp_scr[:, j:j + wd] = v

        def store_zs(j, wd, v):
            t.zs[:, j:j + wd] = v

        def store_xbc(j, wd, v):
            xbc_scr[:, CONV_HDR:CONV_HDR + L, j:j + wd] = v.reshape(S, L, wd)

        def store_dt(j, wd, v):
            dt_scr[...] = _softplus(v + w.dtb[...])

        def store_g(j, wd, v):
            t.g[:, j:j + wd] = v

        for col0, ncols, store in ((COL_UP, D_POOL, store_up), (COL_ZP, D_POOL, store_zp),
                                   (COL_ZS, D_INNER, store_zs), (COL_XBC, D_XBC, store_xbc),
                                   (COL_DT, LANES, store_dt), (COL_G, 2 * D_MODEL, store_g)):
            for col, wd, j in _proj_chunks(col0, ncols):
                store(j, wd, _dot(h_scr[...], w.wcat[:, col:col + wd]))

        up_scr[:, 1:POOL_HDR, :] = pool_in_ref[...]
        _pool_windows(up_scr, t, PAST_LEN, S, L)
        _pool_mix(w, zp_scr, t)
        pool_out_ref[...] = up_scr[:, L + 1:L + POOL_HDR, :]

        xbc_scr[:, CONV_HDR - (CONV_W - 1):CONV_HDR, :] = conv_in_ref[...]
        for j in range(0, D_XBC, DOT_COLS):
            _conv_chunk(w, xbc_scr, t, j, S, L)
        conv_out_ref[...] = xbc_scr[:, L + CONV_HDR - (CONV_W - 1):L + CONV_HDR, :]
        _ssd_prep(w, t, dt_scr, 0, False)
        _ssd_intra(w, t, 0, L, False)

    for q in range(seq_per_inner):
        r0 = (step * seq_per_inner + q) * L
        _sample_state(t, r0, L, ssm_in_ref.at[q], ssm_out_ref.at[q])

    @pl.when(step == n_inner - 1)
    def _():
        _dense_out(w, t, x_ref, lambda ss: mod_ref[ss][:, :, 2 * D_MODEL:3 * D_MODEL],
                   y_ref, S, L)


def _resident(shape):
    zeros = (0,) * len(shape)
    return pl.BlockSpec(shape, lambda *_: zeros, pipeline_mode=pl.Buffered(1))


def _ssd_scratch(R, aux_shape):
    return [
        pltpu.VMEM((R, D_INNER), F32),
        pltpu.VMEM((R, N_BC_GROUPS * D_STATE), F32),
        pltpu.VMEM((R, N_BC_GROUPS * D_STATE), F32),
        pltpu.VMEM((R, D_INNER), F32),
        pltpu.VMEM((R, D_POOL), BF16),
        pltpu.VMEM((R, D_MODEL), F32),
        pltpu.VMEM((R, D_INNER), F32),
        pltpu.VMEM((R, D_INNER), F32),
        pltpu.VMEM((R // CHUNK, 3, CHUNK, LANES), F32),
        pltpu.VMEM(aux_shape, F32),
        pltpu.VMEM((R, D_INNER), BF16),
        pltpu.VMEM((R, D_MODEL), F32),
        pltpu.VMEM((R, D_INNER), F32),
    ]


def _prompt_call(x, mod, weights, *, L):
    nb, seq, _ = x.shape
    NT = seq // L
    n_tiles = nb * NT
    tile_a = lambda i: jnp.minimum(i, n_tiles - 1)
    tile_b = lambda i: jnp.maximum(i - 1, 0)
    x_spec_a = pl.BlockSpec((1, L, D_MODEL), lambda i: (tile_a(i) // NT, tile_a(i) % NT, 0))
    x_spec_b = pl.BlockSpec((1, L, D_MODEL), lambda i: (tile_b(i) // NT, tile_b(i) % NT, 0))
    seq_map_b = lambda i: (tile_b(i) // NT, 0, 0)
    in_specs = [x_spec_a, x_spec_b, _resident(mod.shape)] + [_resident(v.shape) for v in weights]
    out_shape = (jax.ShapeDtypeStruct(x.shape, F32),
                 jax.ShapeDtypeStruct((nb, POOL_BUF, D_POOL), F32),
                 jax.ShapeDtypeStruct((nb, CONV_W - 1, D_XBC), F32),
                 jax.ShapeDtypeStruct((nb, D_INNER, D_STATE), F32))
    out_specs = (x_spec_b,
                 pl.BlockSpec((1, POOL_BUF, D_POOL), seq_map_b),
                 pl.BlockSpec((1, CONV_W - 1, D_XBC), seq_map_b),
                 pl.BlockSpec((1, D_INNER, D_STATE), seq_map_b))
    scratch = [
        pltpu.VMEM((L, D_MODEL), BF16),
        pltpu.VMEM((L, D_MODEL), BF16),
        pltpu.VMEM((1, POOL_HDR + L, D_POOL), F32),
        pltpu.VMEM((L, D_POOL), F32),
        pltpu.VMEM((1, CONV_HDR + L, D_XBC), F32),
        pltpu.VMEM((L, LANES), F32),
    ] + _ssd_scratch(L, (D_STATE, D_INNER)) + [
        pltpu.VMEM((CHUNK, GROUP_COLS), F32),
        pltpu.VMEM((L, 2 * D_MODEL), F32),
    ]
    kern = functools.partial(_prompt_kernel, L=L, NT=NT, n_tiles=n_tiles)
    return pl.pallas_call(
        kern, grid=(n_tiles + 1,), in_specs=in_specs, out_specs=out_specs, out_shape=out_shape,
        scratch_shapes=scratch, name="layer_prompt",
        compiler_params=pltpu.CompilerParams(
            dimension_semantics=("arbitrary",), vmem_limit_bytes=VMEM_LIMIT_BYTES),
    )(x, x, mod, *weights)


def _sample_call(x, mod, states, weights, *, S, seq_per_inner):
    ns, L, _ = x.shape
    R = S * L
    n_inner = S // seq_per_inner
    seq_map = lambda o, i: (o, 0, 0)
    st_map = lambda o, i: (o * n_inner + i, 0, 0)
    x_spec = pl.BlockSpec((S, L, D_MODEL), seq_map)
    pool_spec = pl.BlockSpec((S, POOL_BUF, D_POOL), seq_map)
    conv_spec = pl.BlockSpec((S, CONV_W - 1, D_XBC), seq_map)
    st_spec = pl.BlockSpec((seq_per_inner, D_INNER, D_STATE), st_map)
    in_specs = ([x_spec, pl.BlockSpec((S, 1, 3 * D_MODEL), seq_map), pool_spec, conv_spec, st_spec]
                + [_resident(v.shape) for v in weights])
    out_shape = (jax.ShapeDtypeStruct(x.shape, F32),
                 jax.ShapeDtypeStruct((ns, POOL_BUF, D_POOL), F32),
                 jax.ShapeDtypeStruct((ns, CONV_W - 1, D_XBC), F32),
                 jax.ShapeDtypeStruct((ns, D_INNER, D_STATE), F32))
    out_specs = (x_spec, pool_spec, conv_spec, st_spec)
    scratch = [
        pltpu.VMEM((R, D_MODEL), BF16),
        pltpu.VMEM((S, POOL_HDR + L, D_POOL), F32),
        pltpu.VMEM((R, D_POOL), F32),
        pltpu.VMEM((S, CONV_HDR + L, D_XBC), F32),
        pltpu.VMEM((R, LANES), F32),
    ] + _ssd_scratch(R, (LANES, CHUNK)) + [
        pltpu.VMEM((R, 2 * D_MODEL), F32),
    ]
    kern = functools.partial(_sample_kernel, S=S, L=L, n_inner=n_inner,
                             seq_per_inner=seq_per_inner)
    return pl.pallas_call(
        kern, grid=(ns // S, n_inner), in_specs=in_specs, out_specs=out_specs,
        out_shape=out_shape, scratch_shapes=scratch, name="layer_sample",
        compiler_params=pltpu.CompilerParams(
            dimension_semantics=("arbitrary", "arbitrary"), vmem_limit_bytes=VMEM_LIMIT_BYTES),
    )(x, mod, *states, *weights)


def _block_sum_matrix(ql):
    r = jnp.arange(CHUNK)[:, None]
    c = jnp.arange(CHUNK)[None, :]
    same = (r // ql) == (c // ql)
    return jnp.concatenate([same & (c <= r), same], axis=0).astype(BF16)


def _expand_matrix():
    k = jnp.arange(LANES)[:, None]
    c = jnp.arange(D_INNER)[None, :] // HEAD_DIM
    return ((k % N_HEADS == c) & (k < 3 * N_HEADS)).astype(BF16)


def kernel(x_prompt, x_sample, state_pool, state_conv, state_ssm, c_prompt, c_sample, w_ada, b_ada, norm_g, w_in, conv_w, conv_b, dt_bias, a_log, d_skip, ssm_norm_g, pool_w, pool_scale, w_pool_out, w_ssm_out, w_o, final_g):
    depth = w_ada.shape[0]
    assert depth == 1, "kernel is written for a single layer"
    nb, seq, _ = x_prompt.shape
    ns, dec_seq, _ = x_sample.shape
    l = 0

    def head_lanes(v):
        return jnp.pad(jnp.tile(v, (1, 3)), ((0, 0), (0, LANES - 3 * N_HEADS)))

    c_all = jnp.concatenate([c_prompt, c_sample], axis=0)
    mod = pl.pallas_call(
        _adaln_kernel, out_shape=jax.ShapeDtypeStruct((nb + ns, 3 * D_MODEL), F32), name="adaln",
        compiler_params=pltpu.CompilerParams(vmem_limit_bytes=VMEM_LIMIT_BYTES),
    )(c_all, w_ada[l].astype(BF16), b_ada[l][None, :])
    mod = mod[:, None, :]

    e3 = _expand_matrix()
    w_cat = jnp.concatenate(
        [w_in[l][:, :COL_DT], head_lanes(w_in[l][:, COL_DT:COL_DT + N_HEADS]),
         w_in[l][:, COL_DT + N_HEADS:]], axis=1).astype(BF16)

    def weights(ql):
        return [
            norm_g[l][None, :],
            w_cat,
            conv_w[l], conv_b[l][None, :],
            head_lanes(dt_bias[l][None, :]),
            head_lanes(a_log[l][None, :]),
            jnp.repeat(d_skip[l], HEAD_DIM)[None, :],
            ssm_norm_g[l][None, :],
            pool_w[l].astype(BF16), pool_scale[l][None, :],
            w_pool_out[l].astype(BF16), w_ssm_out[l].astype(BF16), w_o[l].astype(BF16),
            final_g[None, :],
            _block_sum_matrix(ql), e3,
        ]

    y_p, pool_p, conv_p, ssm_p = _prompt_call(x_prompt, mod[:nb], weights(CHUNK), L=2 * CHUNK)

    states = (state_pool[l], state_conv[l], state_ssm[l].reshape(ns, D_INNER, D_STATE))
    y_s, pool_s, conv_s, ssm_s = _sample_call(
        x_sample, mod[nb:], states, weights(dec_seq), S=CHUNK // dec_seq, seq_per_inner=2)

    shp = (N_HEADS, HEAD_DIM, D_STATE)
    return (y_p, y_s, pool_p[None], conv_p[None], ssm_p.reshape((1, nb) + shp),
            pool_s[None], conv_s[None], ssm_s.reshape((1, ns) + shp))
```

```python
import functools
import types

import jax
import jax.numpy as jnp
from jax import lax
from jax.experimental import pallas as pl
from jax.experimental.pallas import tpu as pltpu

F32 = jnp.float32
BF16 = jnp.bfloat16

EPS = 1e-6
D_MODEL = 1024
POOL_WINDOWS = (2, 4, 8, 16)
POOL_GROUP = 128
D_POOL = POOL_GROUP * len(POOL_WINDOWS)
POOL_BUF = max(POOL_WINDOWS) - 1
D_INNER = 2048
HEAD_DIM = 64
N_HEADS = D_INNER // HEAD_DIM
N_BC_GROUPS = 4
HEADS_PER_GROUP = N_HEADS // N_BC_GROUPS
GROUP_COLS = HEADS_PER_GROUP * HEAD_DIM
D_STATE = 128
CONV_W = 4
D_XBC = D_INNER + 2 * N_BC_GROUPS * D_STATE
PAST_LEN = 16384

LANES = 128
CHUNK = 128
POOL_HDR = 16
CONV_HDR = 8
DOT_COLS = 512
ROW_BLOCK = 16
CONV_ROW_BLOCK = 32
VMEM_LIMIT_BYTES = 58 * 1024 * 1024

COL_UP = 0
COL_ZP = D_POOL
COL_ZS = 2 * D_POOL
COL_XBC = COL_ZS + D_INNER
COL_DT = COL_XBC + D_XBC
COL_G = COL_DT + LANES
N_WEIGHTS = 16


def _dot(a, b):
    return jnp.dot(a, b, preferred_element_type=F32)


def _dot_nt(a, b):
    return lax.dot_general(a, b, (((1,), (1,)), ((), ())), preferred_element_type=F32)


def _pack3(v, lane):
    hi = v.astype(BF16).astype(F32)
    r = v - hi
    mid = r.astype(BF16).astype(F32)
    lo = r - mid
    packed = jnp.where(lane < N_HEADS, hi,
                       jnp.where(lane < 2 * N_HEADS, mid,
                                 jnp.where(lane < 3 * N_HEADS, lo, 0.0)))
    return packed.astype(BF16)


def _silu(v):
    return v / (1.0 + jnp.exp(-v))


def _sigmoid(v):
    return 1.0 / (1.0 + jnp.exp(-v))


def _softplus(v):
    return jnp.maximum(v, 0.0) + jnp.log1p(jnp.exp(-jnp.abs(v)))


def _adaln_kernel(c_ref, w_ref, b_ref, o_ref):
    c = c_ref[...]
    o_ref[...] = _dot(_silu(c).astype(BF16), w_ref[...]) + b_ref[...]


def _weight_refs(it):
    names = ("normg", "wcat", "convw", "convb", "dtb", "alog", "dskip", "ssmg", "poolw",
             "pscale", "wpo", "wso", "wo", "fg", "ta", "e3")
    assert len(names) == N_WEIGHTS
    return types.SimpleNamespace(**{n: next(it) for n in names})


def _row_blocks(S, L, nrows):
    if L >= nrows:
        return [(slice(s, s + 1), slice(l0, l0 + nrows), slice(s * L + l0, s * L + l0 + nrows))
                for s in range(S) for l0 in range(0, L, nrows)]
    sb = nrows // L
    return [(slice(s0, s0 + sb), slice(0, L), slice(s0 * L, (s0 + sb) * L))
            for s0 in range(0, S, sb)]


def _norm_modulate(w, x_ref, mod_of, h_ref, S, L):
    for ss, ls, rs in _row_blocks(S, L, ROW_BLOCK):
        x3 = x_ref[ss, ls, :]
        mod3 = mod_of(ss)
        ms = jnp.mean(x3 * x3, axis=-1, keepdims=True)
        hn = x3 * lax.rsqrt(ms + EPS) * w.normg[...]
        hm = hn * (1.0 + mod3[:, :, D_MODEL:2 * D_MODEL]) + mod3[:, :, 0:D_MODEL]
        h_ref[rs, :] = hm.reshape(rs.stop - rs.start, D_MODEL).astype(BF16)


def _proj_chunks(col0, ncols):
    return [(col0 + j, min(DOT_COLS, ncols - j), j) for j in range(0, ncols, DOT_COLS)]


def _pool_windows(up3, t, pos0, S, L):
    for ss, ls, rs in _row_blocks(S, L, ROW_BLOCK):
        nseq, ntok = ss.stop - ss.start, ls.stop - ls.start
        pos = pos0 + ls.start + lax.broadcasted_iota(jnp.int32, (nseq, ntok, POOL_GROUP), 1)
        for gi, win in enumerate(POOL_WINDOWS):
            sl = slice(gi * POOL_GROUP, (gi + 1) * POOL_GROUP)
            u = up3[ss, POOL_HDR + ls.start:POOL_HDR + ls.stop, sl]
            s = u
            for k in range(1, win):
                s = s + up3[ss, POOL_HDR + ls.start - k:POOL_HDR + ls.stop - k, sl]
            cnt = jnp.minimum(pos + 1, win).astype(F32)
            t.pm[rs, sl] = (s / cnt - u).reshape(nseq * ntok, POOL_GROUP).astype(BF16)


def _pool_mix(w, zp, t):
    for gi in range(len(POOL_WINDOWS)):
        sl = slice(gi * POOL_GROUP, (gi + 1) * POOL_GROUP)
        yg = _dot(t.pm[:, sl], w.poolw[gi]) * w.pscale[:, sl]
        t.pm[:, sl] = (yg * _silu(zp[:, sl])).astype(BF16)
    for j in range(0, D_MODEL, DOT_COLS):
        t.p[:, j:j + DOT_COLS] = _dot(t.pm[...], w.wpo[:, j:j + DOT_COLS])


def _conv_chunk(w, xbc3, t, j, S, L):
    sl = slice(j, j + DOT_COLS)
    for ss, ls, rs in _row_blocks(S, L, CONV_ROW_BLOCK):
        acc = w.convb[:, sl]
        for k in range(CONV_W):
            r0 = CONV_HDR - (CONV_W - 1) + k
            acc = acc + xbc3[ss, r0 + ls.start:r0 + ls.stop, sl] * w.convw[k:k + 1, sl]
        v = _silu(acc).reshape(rs.stop - rs.start, DOT_COLS)
        if j < D_INNER:
            t.xs[rs, sl] = v
        elif j < D_INNER + N_BC_GROUPS * D_STATE:
            t.b[rs, :] = v
        else:
            t.c[rs, :] = v


def _ssd_prep(w, t, dt_ref, c, prompt):
    rows = pl.ds(c * CHUNK, CHUNK)
    lane = lax.broadcasted_iota(jnp.int32, (CHUNK, LANES), 1)
    neg_a = jnp.where(lane[0:1, :] < 3 * N_HEADS, -jnp.exp(w.alog[...]), 0.0)
    dtc = dt_ref[rows, :]
    a = dtc * neg_a
    cs = _dot(w.ta[...], _pack3(a, lane))
    cs = cs + pltpu.roll(cs, 2 * N_HEADS, 1)
    cs = cs + pltpu.roll(cs, N_HEADS, 1)
    acs = cs[0:CHUNK, :]
    last = cs[CHUNK:2 * CHUNK, :]
    exp_acs = jnp.exp(acs)
    wgt = jnp.exp(last - acs) * dtc
    t.ea[rows, :] = _dot(_pack3(exp_acs, lane), w.e3[...])
    t.we[rows, :] = _dot(_pack3(wgt, lane), w.e3[...])
    if not prompt:
        t.aux[...] = jnp.exp(last).T
    t.hd[c, 0] = acs
    t.hd[c, 1] = acs.T
    t.hd[c, 2] = dtc.T


def _ssd_intra(w, t, c, QL, prompt, fill=lambda n: None):
    row_start = c * CHUNK
    rows = pl.ds(row_start, CHUNK)
    lane = lax.broadcasted_iota(jnp.int32, (CHUNK, LANES), 1)
    row = lax.broadcasted_iota(jnp.int32, (CHUNK, LANES), 0)
    acs, acs_t, dt_t = t.hd[c, 0], t.hd[c, 1], t.hd[c, 2]
    ql_bits = QL.bit_length() - 1
    mask = ((lax.shift_right_logical(row, ql_bits) == lax.shift_right_logical(lane, ql_bits))
            & (lane <= row))
    for g in range(N_BC_GROUPS):
        gs = slice(g * D_STATE, (g + 1) * D_STATE)
        gc = slice(g * GROUP_COLS, (g + 1) * GROUP_COLS)
        cg = t.c[rows, gs].astype(BF16)
        cb = _dot_nt(cg, t.b[rows, gs].astype(BF16))
        if prompt:
            ht = t.aux[:, gc]
            t.yoff[...] = _dot(cg, ht.astype(BF16))
            xw = (t.xs[rows, gc] * t.we[rows, gc]).astype(BF16)
            upd = _dot(t.b[rows, gs].T.astype(BF16), xw)
            t.aux[:, gc] = ht * t.ea[row_start + CHUNK - 1:row_start + CHUNK, gc] + upd
        for k in range(HEADS_PER_GROUP // 2):
            atts = []
            for h in (g * HEADS_PER_GROUP + 2 * k, g * HEADS_PER_GROUP + 2 * k + 1):
                seg = acs[:, h:h + 1] - acs_t[h:h + 1, :]
                decay = jnp.exp(jnp.where(mask, seg, -jnp.inf))
                atts.append((cb * decay * dt_t[h:h + 1, :]).astype(BF16))
            ps = slice((g * HEADS_PER_GROUP + 2 * k) * HEAD_DIM,
                       (g * HEADS_PER_GROUP + 2 * k + 2) * HEAD_DIM)
            xp = t.xs[rows, ps]
            rhs = jnp.concatenate([jnp.where(lane < HEAD_DIM, xp, 0.0).astype(BF16),
                                   jnp.where(lane >= HEAD_DIM, xp, 0.0).astype(BF16)], axis=0)
            if g % 2 == 0 and k == HEADS_PER_GROUP // 2 - 1:
                fill(1)
            yp = _dot(jnp.concatenate(atts, axis=1), rhs) + w.dskip[:, ps] * xp
            if prompt:
                yp = yp + t.yoff[:, 2 * k * HEAD_DIM:(2 * k + 2) * HEAD_DIM] * t.ea[rows, ps]
            t.yssd[rows, ps] = yp


def _sample_state(t, r0, QL, hin, hout):
    rows = pl.ds(0, CHUNK)
    lane = lax.broadcasted_iota(jnp.int32, (CHUNK, LANES), 1)
    col = jnp.sum(jnp.where(lane == r0, t.aux[...], 0.0), axis=1, keepdims=True)
    rb = jnp.broadcast_to(col, (CHUNK, LANES))
    rowi = lax.broadcasted_iota(jnp.int32, (CHUNK, GROUP_COLS), 0)
    rmask = (rowi >= r0) & (rowi < r0 + QL)
    for g in range(N_BC_GROUPS):
        gs = slice(g * D_STATE, (g + 1) * D_STATE)
        cs = slice(g * GROUP_COLS, (g + 1) * GROUP_COLS)
        hg = hin[cs, :]
        yoff = _dot_nt(t.c[rows, gs].astype(BF16), hg.astype(BF16)) * t.ea[:, cs]
        t.yssd[rows, cs] = t.yssd[rows, cs] + jnp.where(rmask, yoff, 0.0)
        xw = jnp.where(rmask, t.xs[rows, cs] * t.we[:, cs], 0.0)
        upd = _dot(xw.T.astype(BF16), t.b[rows, gs].astype(BF16))
        for e in range(HEADS_PER_GROUP):
            h = g * HEADS_PER_GROUP + e
            hs = slice(h * HEAD_DIM, (h + 1) * HEAD_DIM)
            es = slice(e * HEAD_DIM, (e + 1) * HEAD_DIM)
            hout[hs, :] = hg[es, :] * rb[h:h + 1, :] + upd[es, :]


def _dense_out(w, t, x_ref, gate_of, y_ref, S, L, fill=lambda n: None):
    R = S * L
    col_chunks = [slice(j, j + DOT_COLS) for j in range(0, D_INNER, DOT_COLS)]
    for r in range(0, R, ROW_BLOCK):
        rs = slice(r, r + ROW_BLOCK)
        ys = [t.yssd[rs, sl] * _silu(t.zs[rs, sl]) for sl in col_chunks]
        ssq = sum(jnp.sum(y * y, axis=-1, keepdims=True) for y in ys)
        inv = lax.rsqrt(ssq * (1.0 / D_INNER) + EPS)
        for y, sl in zip(ys, col_chunks):
            t.yn[rs, sl] = (y * inv * w.ssmg[:, sl]).astype(BF16)
    fill(2)
    for j in range(0, D_MODEL, DOT_COLS):
        t.s[:, j:j + DOT_COLS] = _dot(t.yn[...], w.wso[:, j:j + DOT_COLS])
    for r in range(0, R, ROW_BLOCK):
        rs = slice(r, r + ROW_BLOCK)
        for j in range(0, D_MODEL, DOT_COLS):
            sl = slice(j, j + DOT_COLS)
            sl2 = slice(D_MODEL + j, D_MODEL + j + DOT_COLS)
            t.yn[rs, sl] = (_sigmoid(t.g[rs, sl]) * t.p[rs, sl]
                            + _sigmoid(t.g[rs, sl2]) * t.s[rs, sl]).astype(BF16)
    fill(2)
    for j in range(0, D_MODEL, DOT_COLS):
        t.s[:, j:j + DOT_COLS] = _dot(t.yn[:, 0:D_MODEL], w.wo[:, j:j + DOT_COLS])
    for ss, ls, rs in _row_blocks(S, L, ROW_BLOCK):
        o3 = t.s[rs, :].reshape(ss.stop - ss.start, ls.stop - ls.start, D_MODEL)
        xn = x_ref[ss, ls, :] + gate_of(ss) * o3
        ms = jnp.mean(xn * xn, axis=-1, keepdims=True)
        y_ref[ss, ls, :] = xn * lax.rsqrt(ms + EPS) * w.fg[...]
    fill(1)


def _ssd_refs(it):
    names = ("xs", "b", "c", "yssd", "pm", "p", "ea", "we", "hd", "aux", "yn", "s", "zs")
    return types.SimpleNamespace(**{n: next(it) for n in names})


def _prompt_kernel(*refs, L, NT, n_tiles):
    S = 1
    it = iter(refs)
    x_ref, xprev_ref, mod_ref = next(it), next(it), next(it)
    w = _weight_refs(it)
    y_ref, pool_out_ref, conv_out_ref, ssm_out_ref = [next(it) for _ in range(4)]
    h_scr, hn_scr, up_scr, zp_scr, xbc_scr, dt_scr = [next(it) for _ in range(6)]
    t = _ssd_refs(it)
    t.yoff, t.g = next(it), next(it)

    i = pl.program_id(0)
    tile_a = jnp.minimum(i, n_tiles - 1)
    tile_b = jnp.maximum(i - 1, 0)
    batch_a = tile_a // NT
    batch_b = tile_b // NT
    t_b = tile_b % NT

    def always(k):
        return True

    @pl.when(i == 0)
    def _():
        for ref in (h_scr, up_scr, zp_scr, xbc_scr, dt_scr):
            ref[...] = jnp.zeros(ref.shape, ref.dtype)

    @pl.when(t_b == 0)
    def _():
        up_scr[:, 0:POOL_HDR, :] = jnp.zeros((S, POOL_HDR, D_POOL), F32)
        xbc_scr[:, 0:CONV_HDR, :] = jnp.zeros((S, CONV_HDR, D_XBC), F32)
        t.aux[...] = jnp.zeros(t.aux.shape, F32)

    def store_up(j, wd, v):
        up_scr[:, POOL_HDR:POOL_HDR + L, j:j + wd] = v.reshape(S, L, wd)

    def store_zp(j, wd, v):
        zp_scr[:, j:j + wd] = v

    def store_xbc(j, wd, v):
        xbc_scr[:, CONV_HDR:CONV_HDR + L, j:j + wd] = v.reshape(S, L, wd)

    def store_dt(j, wd, v):
        dt_scr[...] = _softplus(v + w.dtb[...])

    def store_zs(j, wd, v):
        t.zs[:, j:j + wd] = v

    def store_g(j, wd, v):
        t.g[:, j:j + wd] = v

    def chunks(h_ref, col0, ncols, store):
        return [(h_ref, c, store) for c in _proj_chunks(col0, ncols)]

    def run(chunk):
        h_ref, (col, wd, j), store = chunk
        store(j, wd, _dot(h_ref[...], w.wcat[:, col:col + wd]))

    zs_chunks = chunks(h_scr, COL_ZS, D_INNER, store_zs)
    xbc_chunks = chunks(hn_scr, COL_XBC, D_XBC, store_xbc)
    fill = (zs_chunks[:3] + xbc_chunks + zs_chunks[3:] + chunks(h_scr, COL_G, 2 * D_MODEL, store_g)
            + chunks(hn_scr, COL_UP, D_POOL, store_up) + chunks(hn_scr, COL_ZP, D_POOL, store_zp)
            + chunks(hn_scr, COL_DT, LANES, store_dt))

    def run_fill(n):
        for _ in range(n):
            if fill:
                run(fill.pop(0))

    n_chunks = L // CHUNK

    @pl.when(always(0))
    def _():
        _norm_modulate(w, x_ref, lambda ss: mod_ref[pl.ds(batch_a, 1)], hn_scr, S, L)
        _pool_windows(up_scr, t, t_b * L, S, L)
        pool_out_ref[...] = up_scr[:, L + 1:L + POOL_HDR, :]
        up_scr[:, 0:POOL_HDR, :] = up_scr[:, L:L + POOL_HDR, :]
        run_fill(2)

    @pl.when(always(100))
    def _():
        _pool_mix(w, zp_scr, t)
        for c in range(n_chunks):
            _ssd_prep(w, t, dt_scr, c, True)

    for j in range(0, D_XBC, DOT_COLS):
        @pl.when(always(1 + j // DOT_COLS))
        def _(j=j):
            sl = slice(j, j + DOT_COLS)
            _conv_chunk(w, xbc_scr, t, j, S, L)
            conv_out_ref[:, :, sl] = xbc_scr[:, L + CONV_HDR - (CONV_W - 1):L + CONV_HDR, sl]
            xbc_scr[:, 0:CONV_HDR, sl] = xbc_scr[:, L:L + CONV_HDR, sl]
            run_fill(1)

    for c in range(n_chunks):
        @pl.when(always(1 + D_XBC // DOT_COLS + c))
        def _(c=c):
            _ssd_intra(w, t, c, CHUNK, True, run_fill)

    @pl.when(always(1 + D_XBC // DOT_COLS + n_chunks))
    def _():
        _dense_out(w, t, xprev_ref,
                   lambda ss: mod_ref[pl.ds(batch_b, 1)][:, :, 2 * D_MODEL:3 * D_MODEL],
                   y_ref, S, L, run_fill)
        assert not fill
        h_scr[...] = hn_scr[...]

    @pl.when(t_b == NT - 1)
    def _():
        for n in range(D_INNER // LANES):
            cols = slice(n * LANES, (n + 1) * LANES)
            ssm_out_ref[0, cols, :] = t.aux[:, cols].T


def _sample_kernel(*refs, S, L, n_inner, seq_per_inner):
    it = iter(refs)
    x_ref, mod_ref, pool_in_ref, conv_in_ref, ssm_in_ref = [next(it) for _ in range(5)]
    w = _weight_refs(it)
    y_ref, pool_out_ref, conv_out_ref, ssm_out_ref = [next(it) for _ in range(4)]
    h_scr, up_scr, zp_scr, xbc_scr, dt_scr = [next(it) for _ in range(5)]
    t = _ssd_refs(it)
    t.g = next(it)
    step = pl.program_id(1)

    @pl.when(step == 0)
    def _():
        _norm_modulate(w, x_ref, lambda ss: mod_ref[ss], h_scr, S, L)

        def store_up(j, wd, v):
            up_scr[:, POOL_HDR:POOL_HDR + L, j:j + wd] = v.reshape(S, L, wd)

        def store_zp(j, wd, v):
            zp_scr[:, j:j + wd] = v

        def store_zs(j, wd, v):
            t.zs[:, j:j + wd] = v

        def store_xbc(j, wd, v):
            xbc_scr[:, CONV_HDR:CONV_HDR + L, j:j + wd] = v.reshape(S, L, wd)

        def store_dt(j, wd, v):
            dt_scr[...] = _softplus(v + w.dtb[...])

        def store_g(j, wd, v):
            t.g[:, j:j + wd] = v

        for col0, ncols, store in ((COL_UP, D_POOL, store_up), (COL_ZP, D_POOL, store_zp),
                                   (COL_ZS, D_INNER, store_zs), (COL_XBC, D_XBC, store_xbc),
                                   (COL_DT, LANES, store_dt), (COL_G, 2 * D_MODEL, store_g)):
            for col, wd, j in _proj_chunks(col0, ncols):
                store(j, wd, _dot(h_scr[...], w.wcat[:, col:col + wd]))

        up_scr[:, 1:POOL_HDR, :] = pool_in_ref[...]
        _pool_windows(up_scr, t, PAST_LEN, S, L)
        _pool_mix(w, zp_scr, t)
        pool_out_ref[...] = up_scr[:, L + 1:L + POOL_HDR, :]

        xbc_scr[:, CONV_HDR - (CONV_W - 1):CONV_HDR, :] = conv_in_ref[...]
        for j in range(0, D_XBC, DOT_COLS):
            _conv_chunk(w, xbc_scr, t, j, S, L)
        conv_out_ref[...] = xbc_scr[:, L + CONV_HDR - (CONV_W - 1):L + CONV_HDR, :]
        _ssd_prep(w, t, dt_scr, 0, False)
        _ssd_intra(w, t, 0, L, False)

    for q in range(seq_per_inner):
        r0 = (step * seq_per_inner + q) * L
        _sample_state(t, r0, L, ssm_in_ref.at[q], ssm_out_ref.at[q])

    @pl.when(step == n_inner - 1)
    def _():
        _dense_out(w, t, x_ref, lambda ss: mod_ref[ss][:, :, 2 * D_MODEL:3 * D_MODEL],
                   y_ref, S, L)


def _resident(shape):
    zeros = (0,) * len(shape)
    return pl.BlockSpec(shape, lambda *_: zeros, pipeline_mode=pl.Buffered(1))


def _ssd_scratch(R, aux_shape):
    return [
        pltpu.VMEM((R, D_INNER), F32),
        pltpu.VMEM((R, N_BC_GROUPS * D_STATE), F32),
        pltpu.VMEM((R, N_BC_GROUPS * D_STATE), F32),
        pltpu.VMEM((R, D_INNER), F32),
        pltpu.VMEM((R, D_POOL), BF16),
        pltpu.VMEM((R, D_MODEL), F32),
        pltpu.VMEM((R, D_INNER), F32),
        pltpu.VMEM((R, D_INNER), F32),
        pltpu.VMEM((R // CHUNK, 3, CHUNK, LANES), F32),
        pltpu.VMEM(aux_shape, F32),
        pltpu.VMEM((R, D_INNER), BF16),
        pltpu.VMEM((R, D_MODEL), F32),
        pltpu.VMEM((R, D_INNER), F32),
    ]


def _prompt_call(x, mod, weights, *, L):
    nb, seq, _ = x.shape
    NT = seq // L
    n_tiles = nb * NT
    tile_a = lambda i: jnp.minimum(i, n_tiles - 1)
    tile_b = lambda i: jnp.maximum(i - 1, 0)
    x_spec_a = pl.BlockSpec((1, L, D_MODEL), lambda i: (tile_a(i) // NT, tile_a(i) % NT, 0))
    x_spec_b = pl.BlockSpec((1, L, D_MODEL), lambda i: (tile_b(i) // NT, tile_b(i) % NT, 0))
    seq_map_b = lambda i: (tile_b(i) // NT, 0, 0)
    in_specs = [x_spec_a, x_spec_b, _resident(mod.shape)] + [_resident(v.shape) for v in weights]
    out_shape = (jax.ShapeDtypeStruct(x.shape, F32),
                 jax.ShapeDtypeStruct((nb, POOL_BUF, D_POOL), F32),
                 jax.ShapeDtypeStruct((nb, CONV_W - 1, D_XBC), F32),
                 jax.ShapeDtypeStruct((nb, D_INNER, D_STATE), F32))
    out_specs = (x_spec_b,
                 pl.BlockSpec((1, POOL_BUF, D_POOL), seq_map_b),
                 pl.BlockSpec((1, CONV_W - 1, D_XBC), seq_map_b),
                 pl.BlockSpec((1, D_INNER, D_STATE), seq_map_b))
    scratch = [
        pltpu.VMEM((L, D_MODEL), BF16),
        pltpu.VMEM((L, D_MODEL), BF16),
        pltpu.VMEM((1, POOL_HDR + L, D_POOL), F32),
        pltpu.VMEM((L, D_POOL), F32),
        pltpu.VMEM((1, CONV_HDR + L, D_XBC), F32),
        pltpu.VMEM((L, LANES), F32),
    ] + _ssd_scratch(L, (D_STATE, D_INNER)) + [
        pltpu.VMEM((CHUNK, GROUP_COLS), F32),
        pltpu.VMEM((L, 2 * D_MODEL), F32),
    ]
    kern = functools.partial(_prompt_kernel, L=L, NT=NT, n_tiles=n_tiles)
    return pl.pallas_call(
        kern, grid=(n_tiles + 1,), in_specs=in_specs, out_specs=out_specs, out_shape=out_shape,
        scratch_shapes=scratch, name="layer_prompt",
        compiler_params=pltpu.CompilerParams(
            dimension_semantics=("arbitrary",), vmem_limit_bytes=VMEM_LIMIT_BYTES),
    )(x, x, mod, *weights)


def _sample_call(x, mod, states, weights, *, S, seq_per_inner):
    ns, L, _ = x.shape
    R = S * L
    n_inner = S // seq_per_inner
    seq_map = lambda o, i: (o, 0, 0)
    st_map = lambda o, i: (o * n_inner + i, 0, 0)
    x_spec = pl.BlockSpec((S, L, D_MODEL), seq_map)
    pool_spec = pl.BlockSpec((S, POOL_BUF, D_POOL), seq_map)
    conv_spec = pl.BlockSpec((S, CONV_W - 1, D_XBC), seq_map)
    st_spec = pl.BlockSpec((seq_per_inner, D_INNER, D_STATE), st_map)
    in_specs = ([x_spec, pl.BlockSpec((S, 1, 3 * D_MODEL), seq_map), pool_spec, conv_spec, st_spec]
                + [_resident(v.shape) for v in weights])
    out_shape = (jax.ShapeDtypeStruct(x.shape, F32),
                 jax.ShapeDtypeStruct((ns, POOL_BUF, D_POOL), F32),
                 jax.ShapeDtypeStruct((ns, CONV_W - 1, D_XBC), F32),
                 jax.ShapeDtypeStruct((ns, D_INNER, D_STATE), F32))
    out_specs = (x_spec, pool_spec, conv_spec, st_spec)
    scratch = [
        pltpu.VMEM((R, D_MODEL), BF16),
        pltpu.VMEM((S, POOL_HDR + L, D_POOL), F32),
        pltpu.VMEM((R, D_POOL), F32),
        pltpu.VMEM((S, CONV_HDR + L, D_XBC), F32),
        pltpu.VMEM((R, LANES), F32),
    ] + _ssd_scratch(R, (LANES, CHUNK)) + [
        pltpu.VMEM((R, 2 * D_MODEL), F32),
    ]
    kern = functools.partial(_sample_kernel, S=S, L=L, n_inner=n_inner,
                             seq_per_inner=seq_per_inner)
    return pl.pallas_call(
        kern, grid=(ns // S, n_inner), in_specs=in_specs, out_specs=out_specs,
        out_shape=out_shape, scratch_shapes=scratch, name="layer_sample",
        compiler_params=pltpu.CompilerParams(
            dimension_semantics=("arbitrary", "arbitrary"), vmem_limit_bytes=VMEM_LIMIT_BYTES),
    )(x, mod, *states, *weights)


def _block_sum_matrix(ql):
    r = jnp.arange(CHUNK)[:, None]
    c = jnp.arange(CHUNK)[None, :]
    same = (r // ql) == (c // ql)
    return jnp.concatenate([same & (c <= r), same], axis=0).astype(BF16)


def _expand_matrix():
    k = jnp.arange(LANES)[:, None]
    c = jnp.arange(D_INNER)[None, :] // HEAD_DIM
    return ((k % N_HEADS == c) & (k < 3 * N_HEADS)).astype(BF16)


def kernel(x_prompt, x_sample, state_pool, state_conv, state_ssm, c_prompt, c_sample, w_ada, b_ada, norm_g, w_in, conv_w, conv_b, dt_bias, a_log, d_skip, ssm_norm_g, pool_w, pool_scale, w_pool_out, w_ssm_out, w_o, final_g):
    depth = w_ada.shape[0]
    assert depth == 1, "kernel is written for a single layer"
    nb, seq, _ = x_prompt.shape
    ns, dec_seq, _ = x_sample.shape
    l = 0

    def head_lanes(v):
        return jnp.pad(jnp.tile(v, (1, 3)), ((0, 0), (0, LANES - 3 * N_HEADS)))

    c_all = jnp.concatenate([c_prompt, c_sample], axis=0)
    mod = pl.pallas_call(
        _adaln_kernel, out_shape=jax.ShapeDtypeStruct((nb + ns, 3 * D_MODEL), F32), name="adaln",
        compiler_params=pltpu.CompilerParams(vmem_limit_bytes=VMEM_LIMIT_BYTES),
    )(c_all, w_ada[l].astype(BF16), b_ada[l][None, :])
    mod = mod[:, None, :]

    e3 = _expand_matrix()
    w_cat = jnp.concatenate(
        [w_in[l][:, :COL_DT], head_lanes(w_in[l][:, COL_DT:COL_DT + N_HEADS]),
         w_in[l][:, COL_DT + N_HEADS:]], axis=1).astype(BF16)

    def weights(ql):
        return [
            norm_g[l][None, :],
            w_cat,
            conv_w[l], conv_b[l][None, :],
            head_lanes(dt_bias[l][None, :]),
            head_lanes(a_log[l][None, :]),
            jnp.repeat(d_skip[l], HEAD_DIM)[None, :],
            ssm_norm_g[l][None, :],
            pool_w[l].astype(BF16), pool_scale[l][None, :],
            w_pool_out[l].astype(BF16), w_ssm_out[l].astype(BF16), w_o[l].astype(BF16),
            final_g[None, :],
            _block_sum_matrix(ql), e3,
        ]

    y_p, pool_p, conv_p, ssm_p = _prompt_call(x_prompt, mod[:nb], weights(CHUNK), L=2 * CHUNK)

    states = (state_pool[l], state_conv[l], state_ssm[l].reshape(ns, D_INNER, D_STATE))
    y_s, pool_s, conv_s, ssm_s = _sample_call(
        x_sample, mod[nb:], states, weights(dec_seq), S=CHUNK // dec_seq, seq_per_inner=2)

    shp = (N_HEADS, HEAD_DIM, D_STATE)
    return (y_p, y_s, pool_p[None], conv_p[None], ssm_p.reshape((1, nb) + shp),
            pool_s[None], conv_s[None], ssm_s.reshape((1, ns) + shp))
```

```python
import functools
import types

import jax
import jax.numpy as jnp
from jax import lax
from jax.experimental import pallas as pl
from jax.experimental.pallas import tpu as pltpu

F32 = jnp.float32
BF16 = jnp.bfloat16

EPS = 1e-6
D_MODEL = 1024
POOL_WINDOWS = (2, 4, 8, 16)
POOL_GROUP = 128
D_POOL = POOL_GROUP * len(POOL_WINDOWS)
POOL_BUF = max(POOL_WINDOWS) - 1
D_INNER = 2048
HEAD_DIM = 64
N_HEADS = D_INNER // HEAD_DIM
N_BC_GROUPS = 4
HEADS_PER_GROUP = N_HEADS // N_BC_GROUPS
GROUP_COLS = HEADS_PER_GROUP * HEAD_DIM
D_STATE = 128
CONV_W = 4
D_XBC = D_INNER + 2 * N_BC_GROUPS * D_STATE
PAST_LEN = 16384

LANES = 128
CHUNK = 128
POOL_HDR = 16
CONV_HDR = 8
DOT_COLS = 512
ROW_BLOCK = 16
CONV_ROW_BLOCK = 32
VMEM_LIMIT_BYTES = 58 * 1024 * 1024

COL_UP = 0
COL_ZP = D_POOL
COL_ZS = 2 * D_POOL
COL_XBC = COL_ZS + D_INNER
COL_DT = COL_XBC + D_XBC
COL_G = COL_DT + LANES
N_WEIGHTS = 16


def _dot(a, b):
    return jnp.dot(a, b, preferred_element_type=F32)


def _dot_nt(a, b):
    return lax.dot_general(a, b, (((1,), (1,)), ((), ())), preferred_element_type=F32)


def _pack3(v, lane):
    hi = v.astype(BF16).astype(F32)
    r = v - hi
    mid = r.astype(BF16).astype(F32)
    lo = r - mid
    packed = jnp.where(lane < N_HEADS, hi,
                       jnp.where(lane < 2 * N_HEADS, mid,
                                 jnp.where(lane < 3 * N_HEADS, lo, 0.0)))
    return packed.astype(BF16)


def _silu(v):
    return v / (1.0 + jnp.exp(-v))


def _sigmoid(v):
    return 1.0 / (1.0 + jnp.exp(-v))


def _softplus(v):
    return jnp.maximum(v, 0.0) + jnp.log1p(jnp.exp(-jnp.abs(v)))


def _adaln_kernel(c_ref, w_ref, b_ref, o_ref):
    c = c_ref[...]
    o_ref[...] = _dot(_silu(c).astype(BF16), w_ref[...]) + b_ref[...]


def _weight_refs(it):
    names = ("normg", "wcat", "convw", "convb", "dtb", "alog", "dskip", "ssmg", "poolw",
             "pscale", "wpo", "wso", "wo", "fg", "ta", "e3")
    assert len(names) == N_WEIGHTS
    return types.SimpleNamespace(**{n: next(it) for n in names})


def _row_blocks(S, L, nrows):
    if L >= nrows:
        return [(slice(s, s + 1), slice(l0, l0 + nrows), slice(s * L + l0, s * L + l0 + nrows))
                for s in range(S) for l0 in range(0, L, nrows)]
    sb = nrows // L
    return [(slice(s0, s0 + sb), slice(0, L), slice(s0 * L, (s0 + sb) * L))
            for s0 in range(0, S, sb)]


def _norm_modulate(w, x_ref, mod_of, h_ref, S, L):
    for ss, ls, rs in _row_blocks(S, L, ROW_BLOCK):
        x3 = x_ref[ss, ls, :]
        mod3 = mod_of(ss)
        ms = jnp.mean(x3 * x3, axis=-1, keepdims=True)
        hn = x3 * lax.rsqrt(ms + EPS) * w.normg[...]
        hm = hn * (1.0 + mod3[:, :, D_MODEL:2 * D_MODEL]) + mod3[:, :, 0:D_MODEL]
        h_ref[rs, :] = hm.reshape(rs.stop - rs.start, D_MODEL).astype(BF16)


def _proj_chunks(col0, ncols):
    return [(col0 + j, min(DOT_COLS, ncols - j), j) for j in range(0, ncols, DOT_COLS)]


def _pool_windows(up3, t, pos0, S, L):
    for ss, ls, rs in _row_blocks(S, L, ROW_BLOCK):
        nseq, ntok = ss.stop - ss.start, ls.stop - ls.start
        pos = pos0 + ls.start + lax.broadcasted_iota(jnp.int32, (nseq, ntok, POOL_GROUP), 1)
        for gi, win in enumerate(POOL_WINDOWS):
            sl = slice(gi * POOL_GROUP, (gi + 1) * POOL_GROUP)
            u = up3[ss, POOL_HDR + ls.start:POOL_HDR + ls.stop, sl]
            s = u
            for k in range(1, win):
                s = s + up3[ss, POOL_HDR + ls.start - k:POOL_HDR + ls.stop - k, sl]
            cnt = jnp.minimum(pos + 1, win).astype(F32)
            t.pm[rs, sl] = (s / cnt - u).reshape(nseq * ntok, POOL_GROUP).astype(BF16)


def _pool_mix(w, zp, t):
    for gi in range(len(POOL_WINDOWS)):
        sl = slice(gi * POOL_GROUP, (gi + 1) * POOL_GROUP)
        yg = _dot(t.pm[:, sl], w.poolw[gi]) * w.pscale[:, sl]
        t.pm[:, sl] = (yg * _silu(zp[:, sl])).astype(BF16)
    for j in range(0, D_MODEL, DOT_COLS):
        t.p[:, j:j + DOT_COLS] = _dot(t.pm[...], w.wpo[:, j:j + DOT_COLS])


def _conv_chunk(w, xbc3, t, j, S, L):
    sl = slice(j, j + DOT_COLS)
    for ss, ls, rs in _row_blocks(S, L, CONV_ROW_BLOCK):
        acc = w.convb[:, sl]
        for k in range(CONV_W):
            r0 = CONV_HDR - (CONV_W - 1) + k
            acc = acc + xbc3[ss, r0 + ls.start:r0 + ls.stop, sl] * w.convw[k:k + 1, sl]
        v = _silu(acc).reshape(rs.stop - rs.start, DOT_COLS)
        if j < D_INNER:
            t.xs[rs, sl] = v
        elif j < D_INNER + N_BC_GROUPS * D_STATE:
            t.b[rs, :] = v
        else:
            t.c[rs, :] = v


def _ssd_prep(w, t, dt_ref, c, prompt):
    rows = pl.ds(c * CHUNK, CHUNK)
    lane = lax.broadcasted_iota(jnp.int32, (CHUNK, LANES), 1)
    neg_a = jnp.where(lane[0:1, :] < 3 * N_HEADS, -jnp.exp(w.alog[...]), 0.0)
    dtc = dt_ref[rows, :]
    a = dtc * neg_a
    cs = _dot(w.ta[...], _pack3(a, lane))
    cs = cs + pltpu.roll(cs, 2 * N_HEADS, 1)
    cs = cs + pltpu.roll(cs, N_HEADS, 1)
    acs = cs[0:CHUNK, :]
    last = cs[CHUNK:2 * CHUNK, :]
    exp_acs = jnp.exp(acs)
    wgt = jnp.exp(last - acs) * dtc
    t.ea[rows, :] = _dot(_pack3(exp_acs, lane), w.e3[...])
    t.we[rows, :] = _dot(_pack3(wgt, lane), w.e3[...])
    if not prompt:
        t.aux[c] = jnp.exp(last).T
    t.hd[c, 0] = acs
    t.hd[c, 1] = acs.T
    t.hd[c, 2] = dtc.T


def _ssd_intra(w, t, c, QL, prompt):
    row_start = c * CHUNK
    rows = pl.ds(row_start, CHUNK)
    lane = lax.broadcasted_iota(jnp.int32, (CHUNK, LANES), 1)
    row = lax.broadcasted_iota(jnp.int32, (CHUNK, LANES), 0)
    acs, acs_t, dt_t = t.hd[c, 0], t.hd[c, 1], t.hd[c, 2]
    ql_bits = QL.bit_length() - 1
    mask = ((lax.shift_right_logical(row, ql_bits) == lax.shift_right_logical(lane, ql_bits))
            & (lane <= row))
    for g in range(N_BC_GROUPS):
        gs = slice(g * D_STATE, (g + 1) * D_STATE)
        gc = slice(g * GROUP_COLS, (g + 1) * GROUP_COLS)
        cg = t.c[rows, gs].astype(BF16)
        cb = _dot_nt(cg, t.b[rows, gs].astype(BF16))
        if prompt:
            ht = t.aux[:, gc]
            t.yoff[...] = _dot(cg, ht.astype(BF16))
            xw = (t.xs[rows, gc] * t.we[rows, gc]).astype(BF16)
            upd = _dot(t.b[rows, gs].T.astype(BF16), xw)
            t.aux[:, gc] = ht * t.ea[row_start + CHUNK - 1:row_start + CHUNK, gc] + upd
        for k in range(HEADS_PER_GROUP // 2):
            atts = []
            for h in (g * HEADS_PER_GROUP + 2 * k, g * HEADS_PER_GROUP + 2 * k + 1):
                seg = acs[:, h:h + 1] - acs_t[h:h + 1, :]
                decay = jnp.exp(jnp.where(mask, seg, -jnp.inf))
                atts.append((cb * decay * dt_t[h:h + 1, :]).astype(BF16))
            ps = slice((g * HEADS_PER_GROUP + 2 * k) * HEAD_DIM,
                       (g * HEADS_PER_GROUP + 2 * k + 2) * HEAD_DIM)
            xp = t.xs[rows, ps]
            rhs = jnp.concatenate([jnp.where(lane < HEAD_DIM, xp, 0.0).astype(BF16),
                                   jnp.where(lane >= HEAD_DIM, xp, 0.0).astype(BF16)], axis=0)
            yp = _dot(jnp.concatenate(atts, axis=1), rhs) + w.dskip[:, ps] * xp
            if prompt:
                yp = yp + t.yoff[:, 2 * k * HEAD_DIM:(2 * k + 2) * HEAD_DIM] * t.ea[rows, ps]
            t.yssd[rows, ps] = yp


def _sample_state(t, r, QL, hin, hout):
    c = r // CHUNK
    r0 = r % CHUNK
    rows = pl.ds(pl.multiple_of(c * CHUNK, CHUNK), CHUNK)
    lane = lax.broadcasted_iota(jnp.int32, (CHUNK, LANES), 1)
    col = jnp.sum(jnp.where(lane == r0, t.aux[c], 0.0), axis=1, keepdims=True)
    rb = jnp.broadcast_to(col, (CHUNK, LANES))
    rowi = lax.broadcasted_iota(jnp.int32, (CHUNK, GROUP_COLS), 0)
    rmask = (rowi >= r0) & (rowi < r0 + QL)
    for g in range(N_BC_GROUPS):
        gs = slice(g * D_STATE, (g + 1) * D_STATE)
        cs = slice(g * GROUP_COLS, (g + 1) * GROUP_COLS)
        hg = hin[cs, :]
        yoff = _dot_nt(t.c[rows, gs].astype(BF16), hg.astype(BF16)) * t.ea[rows, cs]
        t.yssd[rows, cs] = t.yssd[rows, cs] + jnp.where(rmask, yoff, 0.0)
        xw = jnp.where(rmask, t.xs[rows, cs] * t.we[rows, cs], 0.0)
        upd = _dot(xw.T.astype(BF16), t.b[rows, gs].astype(BF16))
        for e in range(HEADS_PER_GROUP):
            h = g * HEADS_PER_GROUP + e
            hs = slice(h * HEAD_DIM, (h + 1) * HEAD_DIM)
            es = slice(e * HEAD_DIM, (e + 1) * HEAD_DIM)
            hout[hs, :] = hg[es, :] * rb[h:h + 1, :] + upd[es, :]


def _dense_out(w, t, x_ref, gate_of, y_ref, S, L):
    R = S * L
    col_chunks = [slice(j, j + DOT_COLS) for j in range(0, D_INNER, DOT_COLS)]
    for r in range(0, R, ROW_BLOCK):
        rs = slice(r, r + ROW_BLOCK)
        ys = [t.yssd[rs, sl] * _silu(t.zs[rs, sl]) for sl in col_chunks]
        ssq = sum(jnp.sum(y * y, axis=-1, keepdims=True) for y in ys)
        inv = lax.rsqrt(ssq * (1.0 / D_INNER) + EPS)
        for y, sl in zip(ys, col_chunks):
            t.yn[rs, sl] = (y * inv * w.ssmg[:, sl]).astype(BF16)
    for j in range(0, D_MODEL, DOT_COLS):
        t.s[:, j:j + DOT_COLS] = _dot(t.yn[...], w.wso[:, j:j + DOT_COLS])
    for r in range(0, R, ROW_BLOCK):
        rs = slice(r, r + ROW_BLOCK)
        for j in range(0, D_MODEL, DOT_COLS):
            sl = slice(j, j + DOT_COLS)
            sl2 = slice(D_MODEL + j, D_MODEL + j + DOT_COLS)
            t.yn[rs, sl] = (_sigmoid(t.g[rs, sl]) * t.p[rs, sl]
                            + _sigmoid(t.g[rs, sl2]) * t.s[rs, sl]).astype(BF16)
    for j in range(0, D_MODEL, DOT_COLS):
        t.s[:, j:j + DOT_COLS] = _dot(t.yn[:, 0:D_MODEL], w.wo[:, j:j + DOT_COLS])
    for ss, ls, rs in _row_blocks(S, L, ROW_BLOCK):
        o3 = t.s[rs, :].reshape(ss.stop - ss.start, ls.stop - ls.start, D_MODEL)
        xn = x_ref[ss, ls, :] + gate_of(ss) * o3
        ms = jnp.mean(xn * xn, axis=-1, keepdims=True)
        y_ref[ss, ls, :] = xn * lax.rsqrt(ms + EPS) * w.fg[...]


def _ssd_refs(it):
    names = ("xs", "b", "c", "yssd", "pm", "p", "ea", "we", "hd", "aux", "yn", "s", "zs", "g")
    return types.SimpleNamespace(**{n: next(it) for n in names})


def _input_projection(w, h_ref, stores):
    for col0, (ncols, store) in stores.items():
        for col, wd, j in _proj_chunks(col0, ncols):
            store(j, wd, _dot(h_ref[...], w.wcat[:, col:col + wd]))


def _prompt_kernel(*refs, L):
    S = 1
    it = iter(refs)
    x_ref, mod_ref = next(it), next(it)
    w = _weight_refs(it)
    y_ref, pool_out_ref, conv_out_ref, ssm_out_ref = [next(it) for _ in range(4)]
    h_scr, up_scr, zp_scr, xbc_scr, dt_scr = [next(it) for _ in range(5)]
    t = _ssd_refs(it)
    t.yoff = next(it)
    step = pl.program_id(1)

    @pl.when(step == 0)
    def _():
        up_scr[:, 0:POOL_HDR, :] = jnp.zeros((S, POOL_HDR, D_POOL), F32)
        xbc_scr[:, 0:CONV_HDR, :] = jnp.zeros((S, CONV_HDR, D_XBC), F32)
        t.aux[...] = jnp.zeros(t.aux.shape, F32)

    _norm_modulate(w, x_ref, lambda ss: mod_ref[...], h_scr, S, L)

    def store_up(j, wd, v):
        up_scr[:, POOL_HDR:POOL_HDR + L, j:j + wd] = v.reshape(S, L, wd)

    def store_zp(j, wd, v):
        zp_scr[:, j:j + wd] = v

    def store_zs(j, wd, v):
        t.zs[:, j:j + wd] = v

    def store_xbc(j, wd, v):
        xbc_scr[:, CONV_HDR:CONV_HDR + L, j:j + wd] = v.reshape(S, L, wd)

    def store_dt(j, wd, v):
        dt_scr[...] = _softplus(v + w.dtb[...])

    def store_g(j, wd, v):
        t.g[:, j:j + wd] = v

    _input_projection(w, h_scr, {
        COL_UP: (D_POOL, store_up), COL_ZP: (D_POOL, store_zp), COL_DT: (LANES, store_dt),
        COL_XBC: (D_XBC, store_xbc), COL_ZS: (D_INNER, store_zs), COL_G: (2 * D_MODEL, store_g)})

    _pool_windows(up_scr, t, step * L, S, L)
    _pool_mix(w, zp_scr, t)
    pool_out_ref[...] = up_scr[:, L + 1:L + POOL_HDR, :]
    up_scr[:, 0:POOL_HDR, :] = up_scr[:, L:L + POOL_HDR, :]

    for j in range(0, D_XBC, DOT_COLS):
        _conv_chunk(w, xbc_scr, t, j, S, L)
    conv_out_ref[...] = xbc_scr[:, L + CONV_HDR - (CONV_W - 1):L + CONV_HDR, :]
    xbc_scr[:, 0:CONV_HDR, :] = xbc_scr[:, L:L + CONV_HDR, :]

    for c in range(L // CHUNK):
        _ssd_prep(w, t, dt_scr, c, True)
    for c in range(L // CHUNK):
        _ssd_intra(w, t, c, CHUNK, True)

    _dense_out(w, t, x_ref, lambda ss: mod_ref[...][:, :, 2 * D_MODEL:3 * D_MODEL], y_ref, S, L)

    @pl.when(step == pl.num_programs(1) - 1)
    def _():
        for n in range(D_INNER // LANES):
            cols = slice(n * LANES, (n + 1) * LANES)
            ssm_out_ref[0, cols, :] = t.aux[:, cols].T


def _sample_kernel(*refs, S, L, n_inner, seq_per_inner):
    R = S * L
    it = iter(refs)
    x_ref, mod_ref, pool_in_ref, conv_in_ref, ssm_in_ref = [next(it) for _ in range(5)]
    w = _weight_refs(it)
    y_ref, pool_out_ref, conv_out_ref, ssm_out_ref = [next(it) for _ in range(4)]
    h_scr, up_scr, zp_scr, xbc_scr, dt_scr = [next(it) for _ in range(5)]
    t = _ssd_refs(it)
    step = pl.program_id(1)

    @pl.when(step == 0)
    def _():
        _norm_modulate(w, x_ref, lambda ss: mod_ref[ss], h_scr, S, L)

        def store_up(j, wd, v):
            up_scr[:, POOL_HDR:POOL_HDR + L, j:j + wd] = v.reshape(S, L, wd)

        def store_zp(j, wd, v):
            zp_scr[:, j:j + wd] = v

        def store_zs(j, wd, v):
            t.zs[:, j:j + wd] = v

        def store_xbc(j, wd, v):
            xbc_scr[:, CONV_HDR:CONV_HDR + L, j:j + wd] = v.reshape(S, L, wd)

        def store_dt(j, wd, v):
            dt_scr[...] = _softplus(v + w.dtb[...])

        def store_g(j, wd, v):
            t.g[:, j:j + wd] = v

        _input_projection(w, h_scr, {
            COL_UP: (D_POOL, store_up), COL_ZP: (D_POOL, store_zp), COL_DT: (LANES, store_dt),
            COL_XBC: (D_XBC, store_xbc), COL_ZS: (D_INNER, store_zs),
            COL_G: (2 * D_MODEL, store_g)})

        up_scr[:, 1:POOL_HDR, :] = pool_in_ref[...]
        _pool_windows(up_scr, t, PAST_LEN, S, L)
        _pool_mix(w, zp_scr, t)
        pool_out_ref[...] = up_scr[:, L + 1:L + POOL_HDR, :]

        xbc_scr[:, CONV_HDR - (CONV_W - 1):CONV_HDR, :] = conv_in_ref[...]
        for j in range(0, D_XBC, DOT_COLS):
            _conv_chunk(w, xbc_scr, t, j, S, L)
        conv_out_ref[...] = xbc_scr[:, L + CONV_HDR - (CONV_W - 1):L + CONV_HDR, :]
        for c in range(R // CHUNK):
            _ssd_prep(w, t, dt_scr, c, False)
        for c in range(R // CHUNK):
            _ssd_intra(w, t, c, L, False)

    for q in range(seq_per_inner):
        _sample_state(t, (step * seq_per_inner + q) * L, L, ssm_in_ref.at[q], ssm_out_ref.at[q])

    @pl.when(step == n_inner - 1)
    def _():
        _dense_out(w, t, x_ref, lambda ss: mod_ref[ss][:, :, 2 * D_MODEL:3 * D_MODEL],
                   y_ref, S, L)


def _resident(shape):
    zeros = (0,) * len(shape)
    return pl.BlockSpec(shape, lambda *_: zeros, pipeline_mode=pl.Buffered(1))


def _ssd_scratch(R, aux_shape):
    return [
        pltpu.VMEM((R, D_INNER), F32),
        pltpu.VMEM((R, N_BC_GROUPS * D_STATE), F32),
        pltpu.VMEM((R, N_BC_GROUPS * D_STATE), F32),
        pltpu.VMEM((R, D_INNER), F32),
        pltpu.VMEM((R, D_POOL), BF16),
        pltpu.VMEM((R, D_MODEL), F32),
        pltpu.VMEM((R, D_INNER), F32),
        pltpu.VMEM((R, D_INNER), F32),
        pltpu.VMEM((R // CHUNK, 3, CHUNK, LANES), F32),
        pltpu.VMEM(aux_shape, F32),
        pltpu.VMEM((R, D_INNER), BF16),
        pltpu.VMEM((R, D_MODEL), F32),
        pltpu.VMEM((R, D_INNER), F32),
        pltpu.VMEM((R, 2 * D_MODEL), F32),
    ]


def _stage_scratch(S, L):
    return [
        pltpu.VMEM((S * L, D_MODEL), BF16),
        pltpu.VMEM((S, POOL_HDR + L, D_POOL), F32),
        pltpu.VMEM((S * L, D_POOL), F32),
        pltpu.VMEM((S, CONV_HDR + L, D_XBC), F32),
        pltpu.VMEM((S * L, LANES), F32),
    ]


def _prompt_call(x, mod, weights, *, L):
    nb, seq, _ = x.shape
    x_spec = pl.BlockSpec((1, L, D_MODEL), lambda b, t: (b, t, 0))
    seq_map = lambda b, t: (b, 0, 0)
    in_specs = ([x_spec, pl.BlockSpec((1, 1, 3 * D_MODEL), seq_map)]
                + [_resident(v.shape) for v in weights])
    out_shape = (jax.ShapeDtypeStruct(x.shape, F32),
                 jax.ShapeDtypeStruct((nb, POOL_BUF, D_POOL), F32),
                 jax.ShapeDtypeStruct((nb, CONV_W - 1, D_XBC), F32),
                 jax.ShapeDtypeStruct((nb, D_INNER, D_STATE), F32))
    out_specs = (x_spec,
                 pl.BlockSpec((1, POOL_BUF, D_POOL), seq_map),
                 pl.BlockSpec((1, CONV_W - 1, D_XBC), seq_map),
                 pl.BlockSpec((1, D_INNER, D_STATE), seq_map))
    scratch = (_stage_scratch(1, L) + _ssd_scratch(L, (D_STATE, D_INNER))
               + [pltpu.VMEM((CHUNK, GROUP_COLS), F32)])
    kern = functools.partial(_prompt_kernel, L=L)
    return pl.pallas_call(
        kern, grid=(nb, seq // L), in_specs=in_specs, out_specs=out_specs, out_shape=out_shape,
        scratch_shapes=scratch, name="layer_prompt",
        compiler_params=pltpu.CompilerParams(
            dimension_semantics=("arbitrary", "arbitrary"), vmem_limit_bytes=VMEM_LIMIT_BYTES),
    )(x, mod, *weights)


def _sample_call(x, mod, states, weights, *, S, seq_per_inner):
    ns, L, _ = x.shape
    R = S * L
    n_inner = S // seq_per_inner
    seq_map = lambda o, i: (o, 0, 0)
    st_map = lambda o, i: (o * n_inner + i, 0, 0)
    x_spec = pl.BlockSpec((S, L, D_MODEL), seq_map)
    pool_spec = pl.BlockSpec((S, POOL_BUF, D_POOL), seq_map)
    conv_spec = pl.BlockSpec((S, CONV_W - 1, D_XBC), seq_map)
    st_spec = pl.BlockSpec((seq_per_inner, D_INNER, D_STATE), st_map)
    in_specs = ([x_spec, pl.BlockSpec((S, 1, 3 * D_MODEL), seq_map), pool_spec, conv_spec, st_spec]
                + [_resident(v.shape) for v in weights])
    out_shape = (jax.ShapeDtypeStruct(x.shape, F32),
                 jax.ShapeDtypeStruct((ns, POOL_BUF, D_POOL), F32),
                 jax.ShapeDtypeStruct((ns, CONV_W - 1, D_XBC), F32),
                 jax.ShapeDtypeStruct((ns, D_INNER, D_STATE), F32))
    out_specs = (x_spec, pool_spec, conv_spec, st_spec)
    scratch = _stage_scratch(S, L) + _ssd_scratch(R, (R // CHUNK, LANES, CHUNK))
    kern = functools.partial(_sample_kernel, S=S, L=L, n_inner=n_inner,
                             seq_per_inner=seq_per_inner)
    return pl.pallas_call(
        kern, grid=(ns // S, n_inner), in_specs=in_specs, out_specs=out_specs,
        out_shape=out_shape, scratch_shapes=scratch, name="layer_sample",
        compiler_params=pltpu.CompilerParams(
            dimension_semantics=("arbitrary", "arbitrary"), vmem_limit_bytes=VMEM_LIMIT_BYTES),
    )(x, mod, *states, *weights)


def _block_sum_matrix(ql):
    r = jnp.arange(CHUNK)[:, None]
    c = jnp.arange(CHUNK)[None, :]
    same = (r // ql) == (c // ql)
    return jnp.concatenate([same & (c <= r), same], axis=0).astype(BF16)


def _expand_matrix():
    k = jnp.arange(LANES)[:, None]
    c = jnp.arange(D_INNER)[None, :] // HEAD_DIM
    return ((k % N_HEADS == c) & (k < 3 * N_HEADS)).astype(BF16)


def kernel(x_prompt, x_sample, state_pool, state_conv, state_ssm, c_prompt, c_sample, w_ada, b_ada, norm_g, w_in, conv_w, conv_b, dt_bias, a_log, d_skip, ssm_norm_g, pool_w, pool_scale, w_pool_out, w_ssm_out, w_o, final_g):
    depth = w_ada.shape[0]
    assert depth == 1, "kernel is written for a single layer"
    nb, seq, _ = x_prompt.shape
    ns, dec_seq, _ = x_sample.shape
    l = 0

    def head_lanes(v):
        return jnp.pad(jnp.tile(v, (1, 3)), ((0, 0), (0, LANES - 3 * N_HEADS)))

    c_all = jnp.concatenate([c_prompt, c_sample], axis=0)
    mod = pl.pallas_call(
        _adaln_kernel, out_shape=jax.ShapeDtypeStruct((nb + ns, 3 * D_MODEL), F32), name="adaln",
        compiler_params=pltpu.CompilerParams(vmem_limit_bytes=VMEM_LIMIT_BYTES),
    )(c_all, w_ada[l].astype(BF16), b_ada[l][None, :])
    mod = mod[:, None, :]

    e3 = _expand_matrix()
    w_cat = jnp.concatenate(
        [w_in[l][:, :COL_DT], head_lanes(w_in[l][:, COL_DT:COL_DT + N_HEADS]),
         w_in[l][:, COL_DT + N_HEADS:]], axis=1).astype(BF16)

    def weights(ql):
        return [
            norm_g[l][None, :],
            w_cat,
            conv_w[l], conv_b[l][None, :],
            head_lanes(dt_bias[l][None, :]),
            head_lanes(a_log[l][None, :]),
            jnp.repeat(d_skip[l], HEAD_DIM)[None, :],
            ssm_norm_g[l][None, :],
            pool_w[l].astype(BF16), pool_scale[l][None, :],
            w_pool_out[l].astype(BF16), w_ssm_out[l].astype(BF16), w_o[l].astype(BF16),
            final_g[None, :],
            _block_sum_matrix(ql), e3,
        ]

    tile_rows = 2 * CHUNK
    y_p, pool_p, conv_p, ssm_p = _prompt_call(x_prompt, mod[:nb], weights(CHUNK), L=tile_rows)

    states = (state_pool[l], state_conv[l], state_ssm[l].reshape(ns, D_INNER, D_STATE))
    y_s, pool_s, conv_s, ssm_s = _sample_call(
        x_sample, mod[nb:], states, weights(dec_seq), S=CHUNK // dec_seq, seq_per_inner=2)

    shp = (N_HEADS, HEAD_DIM, D_STATE)
    return (y_p, y_s, pool_p[None], conv_p[None], ssm_p.reshape((1, nb) + shp),
            pool_s[None], conv_s[None], ssm_s.reshape((1, ns) + shp))
```

```python
import functools
import types

import jax
import jax.numpy as jnp
from jax import lax
from jax.experimental import pallas as pl
from jax.experimental.pallas import tpu as pltpu

F32 = jnp.float32
BF16 = jnp.bfloat16

EPS = 1e-6
D_MODEL = 1024
POOL_WINDOWS = (2, 4, 8, 16)
POOL_GROUP = 128
D_POOL = POOL_GROUP * len(POOL_WINDOWS)
POOL_BUF = max(POOL_WINDOWS) - 1
D_INNER = 2048
HEAD_DIM = 64
N_HEADS = D_INNER // HEAD_DIM
N_BC_GROUPS = 4
HEADS_PER_GROUP = N_HEADS // N_BC_GROUPS
GROUP_COLS = HEADS_PER_GROUP * HEAD_DIM
D_STATE = 128
CONV_W = 4
D_XBC = D_INNER + 2 * N_BC_GROUPS * D_STATE
PAST_LEN = 16384

LANES = 128
CHUNK = 128
POOL_HDR = 16
CONV_HDR = 8
DOT_COLS = 512
ROW_BLOCK = 16
CONV_ROW_BLOCK = 32
VMEM_LIMIT_BYTES = 58 * 1024 * 1024

COL_UP = 0
COL_ZP = D_POOL
COL_ZS = 2 * D_POOL
COL_XBC = COL_ZS + D_INNER
COL_DT = COL_XBC + D_XBC
COL_G = COL_DT + LANES
N_WEIGHTS = 17
LOG2E = 1.4426950408889634


def _dot(a, b):
    return jnp.dot(a, b, preferred_element_type=F32)


def _dot_nt(a, b):
    return lax.dot_general(a, b, (((1,), (1,)), ((), ())), preferred_element_type=F32)


def _pack3(v, lane):
    hi = v.astype(BF16).astype(F32)
    r = v - hi
    mid = r.astype(BF16).astype(F32)
    lo = r - mid
    packed = jnp.where(lane < N_HEADS, hi,
                       jnp.where(lane < 2 * N_HEADS, mid,
                                 jnp.where(lane < 3 * N_HEADS, lo, 0.0)))
    return packed.astype(BF16)


def _silu(v):
    return v / (1.0 + jnp.exp2(v * -LOG2E))


def _sigmoid(v):
    return 1.0 / (1.0 + jnp.exp2(v * -LOG2E))


def _softplus(v):
    return jnp.maximum(v, 0.0) + jnp.log1p(jnp.exp(-jnp.abs(v)))


def _repack_main_kernel(x_ref, o_ref):
    o_ref[...] = x_ref[...].T.astype(o_ref.dtype)


def _repack_tail_kernel(dt_ref, g_ref, o_ref, *, g_row0):
    dt = dt_ref[...]
    heads = jnp.concatenate([dt, dt, dt, jnp.zeros((LANES - 3 * N_HEADS, D_MODEL), F32)], axis=0)
    o_ref[:, 0:LANES] = heads.T.astype(o_ref.dtype)
    for j in range(0, 2 * D_MODEL, DOT_COLS):
        o_ref[:, LANES + j:LANES + j + DOT_COLS] = (
            g_ref[g_row0 + j:g_row0 + j + DOT_COLS, :].T.astype(o_ref.dtype))


def _repack_w_in(w_in_t):
    n_in, d = w_in_t.shape
    assert d == D_MODEL and n_in == COL_DT + N_HEADS + 2 * D_MODEL
    w_main = pl.pallas_call(
        _repack_main_kernel, grid=(COL_DT // DOT_COLS,),
        in_specs=[pl.BlockSpec((DOT_COLS, D_MODEL), lambda j: (j, 0))],
        out_specs=pl.BlockSpec((D_MODEL, DOT_COLS), lambda j: (0, j)),
        out_shape=jax.ShapeDtypeStruct((D_MODEL, COL_DT), BF16), name="repack_w_main",
    )(w_in_t)
    g_block = n_in // 4
    g_row = COL_DT + N_HEADS
    g_row0 = g_row % g_block
    assert n_in % g_block == 0 and g_block % 8 == 0 and g_row0 % 8 == 0
    assert g_row0 + 2 * D_MODEL <= g_block and COL_DT % N_HEADS == 0
    w_tail = pl.pallas_call(
        functools.partial(_repack_tail_kernel, g_row0=g_row0), grid=(1,),
        in_specs=[pl.BlockSpec((N_HEADS, D_MODEL), lambda i: (COL_DT // N_HEADS, 0)),
                  pl.BlockSpec((g_block, D_MODEL), lambda i: (g_row // g_block, 0))],
        out_specs=pl.BlockSpec((D_MODEL, LANES + 2 * D_MODEL), lambda i: (0, 0)),
        out_shape=jax.ShapeDtypeStruct((D_MODEL, LANES + 2 * D_MODEL), BF16), name="repack_w_tail",
        compiler_params=pltpu.CompilerParams(vmem_limit_bytes=VMEM_LIMIT_BYTES),
    )(w_in_t, w_in_t)
    return w_main, w_tail


def _adaln_kernel(c_ref, w_ref, b_ref, o_ref):
    c = c_ref[...]
    o_ref[...] = _dot(_silu(c).astype(BF16), w_ref[...]) + b_ref[...]


def _weight_refs(it):
    names = ("normg", "wmain", "wtail", "convw", "convb", "dtb", "alog", "dskip", "ssmg", "poolw",
             "pscale", "wpo", "wso", "wo", "fg", "ta", "e3")
    assert len(names) == N_WEIGHTS
    return types.SimpleNamespace(**{n: next(it) for n in names})


def _row_blocks(S, L, nrows):
    if L >= nrows:
        return [(slice(s, s + 1), slice(l0, l0 + nrows), slice(s * L + l0, s * L + l0 + nrows))
                for s in range(S) for l0 in range(0, L, nrows)]
    sb = nrows // L
    return [(slice(s0, s0 + sb), slice(0, L), slice(s0 * L, (s0 + sb) * L))
            for s0 in range(0, S, sb)]


def _norm_modulate(w, x_ref, mod_of, h_ref, S, L):
    for ss, ls, rs in _row_blocks(S, L, ROW_BLOCK):
        x3 = x_ref[ss, ls, :]
        mod3 = mod_of(ss)
        ms = jnp.mean(x3 * x3, axis=-1, keepdims=True)
        hn = x3 * lax.rsqrt(ms + EPS) * w.normg[...]
        hm = hn * (1.0 + mod3[:, :, D_MODEL:2 * D_MODEL]) + mod3[:, :, 0:D_MODEL]
        h_ref[rs, :] = hm.reshape(rs.stop - rs.start, D_MODEL).astype(BF16)


def _proj_chunks(col0, ncols):
    return [(col0 + j, min(DOT_COLS, ncols - j), j) for j in range(0, ncols, DOT_COLS)]


def _pool_windows(up3, t, pos0, S, L):
    for ss, ls, rs in _row_blocks(S, L, ROW_BLOCK):
        nseq, ntok = ss.stop - ss.start, ls.stop - ls.start
        pos = pos0 + ls.start + lax.broadcasted_iota(jnp.int32, (nseq, ntok, POOL_GROUP), 1)
        for gi, win in enumerate(POOL_WINDOWS):
            sl = slice(gi * POOL_GROUP, (gi + 1) * POOL_GROUP)
            u = up3[ss, POOL_HDR + ls.start:POOL_HDR + ls.stop, sl]
            s = u
            for k in range(1, win):
                s = s + up3[ss, POOL_HDR + ls.start - k:POOL_HDR + ls.stop - k, sl]
            cnt = jnp.minimum(pos + 1, win).astype(F32)
            t.pm[rs, sl] = (s / cnt - u).reshape(nseq * ntok, POOL_GROUP).astype(BF16)


def _pool_mix(w, zp, t):
    for gi in range(len(POOL_WINDOWS)):
        sl = slice(gi * POOL_GROUP, (gi + 1) * POOL_GROUP)
        yg = _dot(t.pm[:, sl], w.poolw[gi]) * w.pscale[:, sl]
        t.pm[:, sl] = (yg * _silu(zp[:, sl])).astype(BF16)
    for j in range(0, D_MODEL, DOT_COLS):
        t.p[:, j:j + DOT_COLS] = _dot(t.pm[...], w.wpo[:, j:j + DOT_COLS])


def _conv_chunk(w, xbc3, t, j, S, L):
    sl = slice(j, j + DOT_COLS)
    for ss, ls, rs in _row_blocks(S, L, CONV_ROW_BLOCK):
        acc = w.convb[:, sl]
        for k in range(CONV_W):
            r0 = CONV_HDR - (CONV_W - 1) + k
            acc = acc + xbc3[ss, r0 + ls.start:r0 + ls.stop, sl] * w.convw[k:k + 1, sl]
        v = _silu(acc).reshape(rs.stop - rs.start, DOT_COLS)
        if j < D_INNER:
            t.xs[rs, sl] = v
        elif j < D_INNER + N_BC_GROUPS * D_STATE:
            t.b[rs, :] = v
        else:
            t.c[rs, :] = v


def _ssd_prep(w, t, dt_ref, c, prompt):
    rows = pl.ds(c * CHUNK, CHUNK)
    lane = lax.broadcasted_iota(jnp.int32, (CHUNK, LANES), 1)
    neg_a = jnp.where(lane[0:1, :] < 3 * N_HEADS, -jnp.exp(w.alog[...]), 0.0)
    dtc = dt_ref[rows, :]
    a = dtc * neg_a
    cs = _dot(w.ta[...], _pack3(a, lane))
    cs = cs + pltpu.roll(cs, 2 * N_HEADS, 1)
    cs = cs + pltpu.roll(cs, N_HEADS, 1)
    acs = cs[0:CHUNK, :]
    last = cs[CHUNK:2 * CHUNK, :]
    exp_acs = jnp.exp(acs)
    wgt = jnp.exp(last - acs) * dtc
    t.ea[rows, :] = _dot(_pack3(exp_acs, lane), w.e3[...])
    t.we[rows, :] = _dot(_pack3(wgt, lane), w.e3[...])
    if not prompt:
        t.aux[c] = jnp.exp(last).T
    acs2 = acs * LOG2E
    t.hd[c, 0] = acs2
    t.hd[c, 1] = acs2.T
    t.hd[c, 2] = dtc.T


def _ssd_intra(w, t, c, QL, prompt):
    row_start = c * CHUNK
    rows = pl.ds(row_start, CHUNK)
    lane = lax.broadcasted_iota(jnp.int32, (CHUNK, LANES), 1)
    row = lax.broadcasted_iota(jnp.int32, (CHUNK, LANES), 0)
    acs2, acs2_t, dt_t = t.hd[c, 0], t.hd[c, 1], t.hd[c, 2]
    ql_bits = QL.bit_length() - 1
    mask = ((lax.shift_right_logical(row, ql_bits) == lax.shift_right_logical(lane, ql_bits))
            & (lane <= row))
    for g in range(N_BC_GROUPS):
        gs = slice(g * D_STATE, (g + 1) * D_STATE)
        gc = slice(g * GROUP_COLS, (g + 1) * GROUP_COLS)
        cg = t.c[rows, gs].astype(BF16)
        cb = _dot_nt(cg, t.b[rows, gs].astype(BF16))
        if prompt:
            ht = t.aux[:, gc]
            t.yoff[...] = _dot(cg, ht.astype(BF16))
            xw = (t.xs[rows, gc] * t.we[rows, gc]).astype(BF16)
            upd = _dot(t.b[rows, gs].T.astype(BF16), xw)
            t.aux[:, gc] = ht * t.ea[row_start + CHUNK - 1:row_start + CHUNK, gc] + upd
        for k in range(HEADS_PER_GROUP // 2):
            atts = []
            for h in (g * HEADS_PER_GROUP + 2 * k, g * HEADS_PER_GROUP + 2 * k + 1):
                seg2 = acs2[:, h:h + 1] - acs2_t[h:h + 1, :]
                decay = jnp.exp2(jnp.where(mask, seg2, -jnp.inf))
                atts.append((cb * decay * dt_t[h:h + 1, :]).astype(BF16))
            ps = slice((g * HEADS_PER_GROUP + 2 * k) * HEAD_DIM,
                       (g * HEADS_PER_GROUP + 2 * k + 2) * HEAD_DIM)
            xp = t.xs[rows, ps]
            rhs = jnp.concatenate([jnp.where(lane < HEAD_DIM, xp, 0.0).astype(BF16),
                                   jnp.where(lane >= HEAD_DIM, xp, 0.0).astype(BF16)], axis=0)
            yp = _dot(jnp.concatenate(atts, axis=1), rhs) + w.dskip[:, ps] * xp
            if prompt:
                yp = yp + t.yoff[:, 2 * k * HEAD_DIM:(2 * k + 2) * HEAD_DIM] * t.ea[rows, ps]
            t.yssd[rows, ps] = yp


def _sample_state(t, r, QL, hin, hout):
    c = r // CHUNK
    r0 = r % CHUNK
    rows = pl.ds(pl.multiple_of(c * CHUNK, CHUNK), CHUNK)
    lane = lax.broadcasted_iota(jnp.int32, (CHUNK, LANES), 1)
    col = jnp.sum(jnp.where(lane == r0, t.aux[c], 0.0), axis=1, keepdims=True)
    rb = jnp.broadcast_to(col, (CHUNK, LANES))
    rowi = lax.broadcasted_iota(jnp.int32, (CHUNK, GROUP_COLS), 0)
    rmask = (rowi >= r0) & (rowi < r0 + QL)
    for g in range(N_BC_GROUPS):
        gs = slice(g * D_STATE, (g + 1) * D_STATE)
        cs = slice(g * GROUP_COLS, (g + 1) * GROUP_COLS)
        hg = hin[cs, :]
        yoff = _dot_nt(t.c[rows, gs].astype(BF16), hg.astype(BF16)) * t.ea[rows, cs]
        t.yssd[rows, cs] = t.yssd[rows, cs] + jnp.where(rmask, yoff, 0.0)
        xw = jnp.where(rmask, t.xs[rows, cs] * t.we[rows, cs], 0.0)
        upd = _dot(xw.T.astype(BF16), t.b[rows, gs].astype(BF16))
        for e in range(HEADS_PER_GROUP):
            h = g * HEADS_PER_GROUP + e
            hs = slice(h * HEAD_DIM, (h + 1) * HEAD_DIM)
            es = slice(e * HEAD_DIM, (e + 1) * HEAD_DIM)
            hout[hs, :] = hg[es, :] * rb[h:h + 1, :] + upd[es, :]


def _dense_out(w, t, x_ref, gate_of, y_ref, S, L):
    R = S * L
    col_chunks = [slice(j, j + DOT_COLS) for j in range(0, D_INNER, DOT_COLS)]
    for r in range(0, R, ROW_BLOCK):
        rs = slice(r, r + ROW_BLOCK)
        ys = [t.yssd[rs, sl] * _silu(t.zs[rs, sl]) for sl in col_chunks]
        ssq = sum(jnp.sum(y * y, axis=-1, keepdims=True) for y in ys)
        inv = lax.rsqrt(ssq * (1.0 / D_INNER) + EPS)
        for y, sl in zip(ys, col_chunks):
            t.yn[rs, sl] = (y * inv * w.ssmg[:, sl]).astype(BF16)
    for j in range(0, D_MODEL, DOT_COLS):
        t.s[:, j:j + DOT_COLS] = _dot(t.yn[...], w.wso[:, j:j + DOT_COLS])
    for r in range(0, R, ROW_BLOCK):
        rs = slice(r, r + ROW_BLOCK)
        for j in range(0, D_MODEL, DOT_COLS):
            sl = slice(j, j + DOT_COLS)
            sl2 = slice(D_MODEL + j, D_MODEL + j + DOT_COLS)
            t.yn[rs, sl] = (_sigmoid(t.g[rs, sl]) * t.p[rs, sl]
                            + _sigmoid(t.g[rs, sl2]) * t.s[rs, sl]).astype(BF16)
    for j in range(0, D_MODEL, DOT_COLS):
        t.s[:, j:j + DOT_COLS] = _dot(t.yn[:, 0:D_MODEL], w.wo[:, j:j + DOT_COLS])
    for ss, ls, rs in _row_blocks(S, L, ROW_BLOCK):
        o3 = t.s[rs, :].reshape(ss.stop - ss.start, ls.stop - ls.start, D_MODEL)
        xn = x_ref[ss, ls, :] + gate_of(ss) * o3
        ms = jnp.mean(xn * xn, axis=-1, keepdims=True)
        y_ref[ss, ls, :] = xn * lax.rsqrt(ms + EPS) * w.fg[...]


def _ssd_refs(it):
    names = ("xs", "b", "c", "yssd", "pm", "p", "ea", "we", "hd", "aux", "yn", "s", "zs", "g")
    return types.SimpleNamespace(**{n: next(it) for n in names})


def _input_projection(w, h_ref, stores):
    for col0, (ncols, store) in stores.items():
        for col, wd, j in _proj_chunks(col0, ncols):
            if col < COL_DT:
                wc = w.wmain[:, col:col + wd]
            else:
                wc = w.wtail[:, col - COL_DT:col - COL_DT + wd]
            store(j, wd, _dot(h_ref[...], wc))


def _prompt_kernel(*refs, L):
    S = 1
    it = iter(refs)
    x_ref, mod_ref = next(it), next(it)
    w = _weight_refs(it)
    y_ref, pool_out_ref, conv_out_ref, ssm_out_ref = [next(it) for _ in range(4)]
    h_scr, up_scr, zp_scr, xbc_scr, dt_scr = [next(it) for _ in range(5)]
    t = _ssd_refs(it)
    t.yoff = next(it)
    step = pl.program_id(1)

    @pl.when(step == 0)
    def _():
        up_scr[:, 0:POOL_HDR, :] = jnp.zeros((S, POOL_HDR, D_POOL), F32)
        xbc_scr[:, 0:CONV_HDR, :] = jnp.zeros((S, CONV_HDR, D_XBC), F32)
        t.aux[...] = jnp.zeros(t.aux.shape, F32)

    _norm_modulate(w, x_ref, lambda ss: mod_ref[...], h_scr, S, L)

    def store_up(j, wd, v):
        up_scr[:, POOL_HDR:POOL_HDR + L, j:j + wd] = v.reshape(S, L, wd)

    def store_zp(j, wd, v):
        zp_scr[:, j:j + wd] = v

    def store_zs(j, wd, v):
        t.zs[:, j:j + wd] = v

    def store_xbc(j, wd, v):
        xbc_scr[:, CONV_HDR:CONV_HDR + L, j:j + wd] = v.reshape(S, L, wd)

    def store_dt(j, wd, v):
        dt_scr[...] = _softplus(v + w.dtb[...])

    def store_g(j, wd, v):
        t.g[:, j:j + wd] = v

    _input_projection(w, h_scr, {
        COL_UP: (D_POOL, store_up), COL_ZP: (D_POOL, store_zp), COL_DT: (LANES, store_dt),
        COL_XBC: (D_XBC, store_xbc), COL_ZS: (D_INNER, store_zs), COL_G: (2 * D_MODEL, store_g)})

    _pool_windows(up_scr, t, step * L, S, L)
    _pool_mix(w, zp_scr, t)
    up_scr[:, 0:POOL_HDR, :] = up_scr[:, L:L + POOL_HDR, :]

    for j in range(0, D_XBC, DOT_COLS):
        _conv_chunk(w, xbc_scr, t, j, S, L)
    xbc_scr[:, 0:CONV_HDR, :] = xbc_scr[:, L:L + CONV_HDR, :]

    for c in range(L // CHUNK):
        _ssd_prep(w, t, dt_scr, c, True)
    for c in range(L // CHUNK):
        _ssd_intra(w, t, c, CHUNK, True)

    _dense_out(w, t, x_ref, lambda ss: mod_ref[...][:, :, 2 * D_MODEL:3 * D_MODEL], y_ref, S, L)

    @pl.when(step == pl.num_programs(1) - 1)
    def _():
        for n in range(D_INNER // LANES):
            cols = slice(n * LANES, (n + 1) * LANES)
            ssm_out_ref[0, cols, :] = t.aux[:, cols].T
        batch = pl.ds(pl.program_id(0), 1)
        for r in range(POOL_BUF):
            pool_out_ref[r, batch, :] = up_scr[0, 1 + r:2 + r, :]
        for r in range(CONV_W - 1):
            r0 = CONV_HDR - (CONV_W - 1) + r
            conv_out_ref[r, batch, :] = xbc_scr[0, r0:r0 + 1, :]


def _sample_kernel(*refs, S, L, n_inner, seq_per_inner):
    R = S * L
    it = iter(refs)
    x_ref, mod_ref, pool_in_ref, conv_in_ref, ssm_in_ref = [next(it) for _ in range(5)]
    w = _weight_refs(it)
    y_ref, pool_out_ref, conv_out_ref, ssm_out_ref = [next(it) for _ in range(4)]
    h_scr, up_scr, zp_scr, xbc_scr, dt_scr = [next(it) for _ in range(5)]
    t = _ssd_refs(it)
    step = pl.program_id(1)

    @pl.when(step == 0)
    def _():
        _norm_modulate(w, x_ref, lambda ss: mod_ref[ss], h_scr, S, L)

        def store_up(j, wd, v):
            up_scr[:, POOL_HDR:POOL_HDR + L, j:j + wd] = v.reshape(S, L, wd)

        def store_zp(j, wd, v):
            zp_scr[:, j:j + wd] = v

        def store_zs(j, wd, v):
            t.zs[:, j:j + wd] = v

        def store_xbc(j, wd, v):
            xbc_scr[:, CONV_HDR:CONV_HDR + L, j:j + wd] = v.reshape(S, L, wd)

        def store_dt(j, wd, v):
            dt_scr[...] = _softplus(v + w.dtb[...])

        def store_g(j, wd, v):
            t.g[:, j:j + wd] = v

        _input_projection(w, h_scr, {
            COL_UP: (D_POOL, store_up), COL_ZP: (D_POOL, store_zp), COL_DT: (LANES, store_dt),
            COL_XBC: (D_XBC, store_xbc), COL_ZS: (D_INNER, store_zs),
            COL_G: (2 * D_MODEL, store_g)})

        for r in range(POOL_BUF):
            up_scr[:, 1 + r, :] = pool_in_ref[r]
        _pool_windows(up_scr, t, PAST_LEN, S, L)
        _pool_mix(w, zp_scr, t)
        for r in range(POOL_BUF):
            pool_out_ref[r] = up_scr[:, L + 1 + r, :]

        for r in range(CONV_W - 1):
            xbc_scr[:, CONV_HDR - (CONV_W - 1) + r, :] = conv_in_ref[r]
        for j in range(0, D_XBC, DOT_COLS):
            _conv_chunk(w, xbc_scr, t, j, S, L)
        for r in range(CONV_W - 1):
            conv_out_ref[r] = xbc_scr[:, L + CONV_HDR - (CONV_W - 1) + r, :]
        for c in range(R // CHUNK):
            _ssd_prep(w, t, dt_scr, c, False)
        for c in range(R // CHUNK):
            _ssd_intra(w, t, c, L, False)

    for q in range(seq_per_inner):
        _sample_state(t, (step * seq_per_inner + q) * L, L, ssm_in_ref.at[q], ssm_out_ref.at[q])

    @pl.when(step == n_inner - 1)
    def _():
        _dense_out(w, t, x_ref, lambda ss: mod_ref[ss][:, :, 2 * D_MODEL:3 * D_MODEL],
                   y_ref, S, L)


def _resident(shape):
    zeros = (0,) * len(shape)
    return pl.BlockSpec(shape, lambda *_: zeros, pipeline_mode=pl.Buffered(1))


def _ssd_scratch(R, aux_shape):
    return [
        pltpu.VMEM((R, D_INNER), F32),
        pltpu.VMEM((R, N_BC_GROUPS * D_STATE), F32),
        pltpu.VMEM((R, N_BC_GROUPS * D_STATE), F32),
        pltpu.VMEM((R, D_INNER), F32),
        pltpu.VMEM((R, D_POOL), BF16),
        pltpu.VMEM((R, D_MODEL), F32),
        pltpu.VMEM((R, D_INNER), F32),
        pltpu.VMEM((R, D_INNER), F32),
        pltpu.VMEM((R // CHUNK, 3, CHUNK, LANES), F32),
        pltpu.VMEM(aux_shape, F32),
        pltpu.VMEM((R, D_INNER), BF16),
        pltpu.VMEM((R, D_MODEL), F32),
        pltpu.VMEM((R, D_INNER), F32),
        pltpu.VMEM((R, 2 * D_MODEL), F32),
    ]


def _stage_scratch(S, L):
    return [
        pltpu.VMEM((S * L, D_MODEL), BF16),
        pltpu.VMEM((S, POOL_HDR + L, D_POOL), F32),
        pltpu.VMEM((S * L, D_POOL), F32),
        pltpu.VMEM((S, CONV_HDR + L, D_XBC), F32),
        pltpu.VMEM((S * L, LANES), F32),
    ]


def _prompt_call(x, mod, weights, *, L):
    nb, seq, _ = x.shape
    x_spec = pl.BlockSpec((1, L, D_MODEL), lambda b, t: (b, t, 0))
    seq_map = lambda b, t: (b, 0, 0)
    in_specs = ([x_spec, pl.BlockSpec((1, 1, 3 * D_MODEL), seq_map)]
                + [_resident(v.shape) for v in weights])
    whole = lambda b, t: (0, 0, 0)
    out_shape = (jax.ShapeDtypeStruct(x.shape, F32),
                 jax.ShapeDtypeStruct((POOL_BUF, nb, D_POOL), F32),
                 jax.ShapeDtypeStruct((CONV_W - 1, nb, D_XBC), F32),
                 jax.ShapeDtypeStruct((nb, D_INNER, D_STATE), F32))
    out_specs = (x_spec,
                 pl.BlockSpec((POOL_BUF, nb, D_POOL), whole),
                 pl.BlockSpec((CONV_W - 1, nb, D_XBC), whole),
                 pl.BlockSpec((1, D_INNER, D_STATE), seq_map))
    scratch = (_stage_scratch(1, L) + _ssd_scratch(L, (D_STATE, D_INNER))
               + [pltpu.VMEM((CHUNK, GROUP_COLS), F32)])
    kern = functools.partial(_prompt_kernel, L=L)
    return pl.pallas_call(
        kern, grid=(nb, seq // L), in_specs=in_specs, out_specs=out_specs, out_shape=out_shape,
        scratch_shapes=scratch, name="layer_prompt",
        compiler_params=pltpu.CompilerParams(
            dimension_semantics=("arbitrary", "arbitrary"), vmem_limit_bytes=VMEM_LIMIT_BYTES),
    )(x, mod, *weights)


def _sample_call(x, mod, states, weights, *, S, seq_per_inner):
    ns, L, _ = x.shape
    R = S * L
    n_inner = S // seq_per_inner
    seq_map = lambda o, i: (o, 0, 0)
    st_map = lambda o, i: (o * n_inner + i, 0, 0)
    time_major = lambda o, i: (0, o, 0)
    x_spec = pl.BlockSpec((S, L, D_MODEL), seq_map)
    pool_spec = pl.BlockSpec((POOL_BUF, S, D_POOL), time_major)
    conv_spec = pl.BlockSpec((CONV_W - 1, S, D_XBC), time_major)
    st_spec = pl.BlockSpec((seq_per_inner, D_INNER, D_STATE), st_map)
    in_specs = ([x_spec, pl.BlockSpec((S, 1, 3 * D_MODEL), seq_map), pool_spec, conv_spec, st_spec]
                + [_resident(v.shape) for v in weights])
    out_shape = (jax.ShapeDtypeStruct(x.shape, F32),
                 jax.ShapeDtypeStruct((POOL_BUF, ns, D_POOL), F32),
                 jax.ShapeDtypeStruct((CONV_W - 1, ns, D_XBC), F32),
                 jax.ShapeDtypeStruct((ns, D_INNER, D_STATE), F32))
    out_specs = (x_spec, pool_spec, conv_spec, st_spec)
    scratch = _stage_scratch(S, L) + _ssd_scratch(R, (R // CHUNK, LANES, CHUNK))
    kern = functools.partial(_sample_kernel, S=S, L=L, n_inner=n_inner,
                             seq_per_inner=seq_per_inner)
    return pl.pallas_call(
        kern, grid=(ns // S, n_inner), in_specs=in_specs, out_specs=out_specs,
        out_shape=out_shape, scratch_shapes=scratch, name="layer_sample",
        compiler_params=pltpu.CompilerParams(
            dimension_semantics=("arbitrary", "arbitrary"), vmem_limit_bytes=VMEM_LIMIT_BYTES),
    )(x, mod, *states, *weights)


def _block_sum_matrix(ql):
    r = jnp.arange(CHUNK)[:, None]
    c = jnp.arange(CHUNK)[None, :]
    same = (r // ql) == (c // ql)
    return jnp.concatenate([same & (c <= r), same], axis=0).astype(BF16)


def _expand_matrix():
    k = jnp.arange(LANES)[:, None]
    c = jnp.arange(D_INNER)[None, :] // HEAD_DIM
    return ((k % N_HEADS == c) & (k < 3 * N_HEADS)).astype(BF16)


def kernel(x_prompt, x_sample, state_pool, state_conv, state_ssm, c_prompt, c_sample, w_ada, b_ada, norm_g, w_in, conv_w, conv_b, dt_bias, a_log, d_skip, ssm_norm_g, pool_w, pool_scale, w_pool_out, w_ssm_out, w_o, final_g):
    depth = w_ada.shape[0]
    assert depth == 1, "kernel is written for a single layer"
    nb, seq, _ = x_prompt.shape
    ns, dec_seq, _ = x_sample.shape
    l = 0

    def head_lanes(v):
        return jnp.pad(jnp.tile(v, (1, 3)), ((0, 0), (0, LANES - 3 * N_HEADS)))

    c_all = jnp.concatenate([c_prompt, c_sample], axis=0)
    mod = pl.pallas_call(
        _adaln_kernel, out_shape=jax.ShapeDtypeStruct((nb + ns, 3 * D_MODEL), F32), name="adaln",
        compiler_params=pltpu.CompilerParams(vmem_limit_bytes=VMEM_LIMIT_BYTES),
    )(c_all, w_ada[l].astype(BF16), b_ada[l][None, :])
    mod = mod[:, None, :]

    e3 = _expand_matrix()
    w_main, w_tail = _repack_w_in(jnp.transpose(w_in[l]))

    def weights(ql):
        return [
            norm_g[l][None, :],
            w_main, w_tail,
            conv_w[l], conv_b[l][None, :],
            head_lanes(dt_bias[l][None, :]),
            head_lanes(a_log[l][None, :]),
            jnp.repeat(d_skip[l], HEAD_DIM)[None, :],
            ssm_norm_g[l][None, :],
            pool_w[l].astype(BF16), pool_scale[l][None, :],
            w_pool_out[l].astype(BF16), w_ssm_out[l].astype(BF16), w_o[l].astype(BF16),
            final_g[None, :],
            _block_sum_matrix(ql), e3,
        ]

    tile_rows = 2 * CHUNK
    y_p, pool_p, conv_p, ssm_p = _prompt_call(x_prompt, mod[:nb], weights(CHUNK), L=tile_rows)

    swap = lambda v: jnp.transpose(v, (1, 0, 2))
    states = (swap(state_pool[l]), swap(state_conv[l]),
              state_ssm[l].reshape(ns, D_INNER, D_STATE))
    y_s, pool_s, conv_s, ssm_s = _sample_call(
        x_sample, mod[nb:], states, weights(dec_seq), S=CHUNK // dec_seq, seq_per_inner=2)

    shp = (N_HEADS, HEAD_DIM, D_STATE)
    return (y_p, y_s, swap(pool_p)[None], swap(conv_p)[None], ssm_p.reshape((1, nb) + shp),
            swap(pool_s)[None], swap(conv_s)[None], ssm_s.reshape((1, ns) + shp))
```

```python
import functools
import types

import jax
import jax.numpy as jnp
from jax import lax
from jax.experimental import pallas as pl
from jax.experimental.pallas import tpu as pltpu

F32 = jnp.float32
BF16 = jnp.bfloat16

EPS = 1e-6
D_MODEL = 1024
POOL_WINDOWS = (2, 4, 8, 16)
POOL_GROUP = 128
D_POOL = POOL_GROUP * len(POOL_WINDOWS)
POOL_BUF = max(POOL_WINDOWS) - 1
D_INNER = 2048
HEAD_DIM = 64
N_HEADS = D_INNER // HEAD_DIM
N_BC_GROUPS = 4
HEADS_PER_GROUP = N_HEADS // N_BC_GROUPS
GROUP_COLS = HEADS_PER_GROUP * HEAD_DIM
D_STATE = 128
CONV_W = 4
D_XBC = D_INNER + 2 * N_BC_GROUPS * D_STATE
PAST_LEN = 16384

LANES = 128
CHUNK = 128
POOL_HDR = 16
CONV_HDR = 8
DOT_COLS = 512
ROW_BLOCK = 16
CONV_ROW_BLOCK = 32
VMEM_LIMIT_BYTES = 58 * 1024 * 1024

COL_UP = 0
COL_ZP = D_POOL
COL_ZS = 2 * D_POOL
COL_XBC = COL_ZS + D_INNER
COL_DT = COL_XBC + D_XBC
COL_G = COL_DT + LANES
N_WEIGHTS = 17
LOG2E = 1.4426950408889634


def _dot(a, b):
    return jnp.dot(a, b, preferred_element_type=F32)


def _dot_nt(a, b):
    return lax.dot_general(a, b, (((1,), (1,)), ((), ())), preferred_element_type=F32)


def _pack3(v, lane):
    hi = v.astype(BF16).astype(F32)
    r = v - hi
    mid = r.astype(BF16).astype(F32)
    lo = r - mid
    packed = jnp.where(lane < N_HEADS, hi,
                       jnp.where(lane < 2 * N_HEADS, mid,
                                 jnp.where(lane < 3 * N_HEADS, lo, 0.0)))
    return packed.astype(BF16)


def _silu(v):
    return v / (1.0 + jnp.exp2(v * -LOG2E))


def _sigmoid(v):
    return 1.0 / (1.0 + jnp.exp2(v * -LOG2E))


def _softplus(v):
    return jnp.maximum(v, 0.0) + jnp.log1p(jnp.exp(-jnp.abs(v)))


def _repack_main_kernel(x_ref, o_ref):
    o_ref[...] = x_ref[...].T.astype(o_ref.dtype)


def _repack_tail_kernel(dt_ref, g_ref, o_ref, *, g_row0):
    dt = dt_ref[...]
    heads = jnp.concatenate([dt, dt, dt, jnp.zeros((LANES - 3 * N_HEADS, D_MODEL), F32)], axis=0)
    o_ref[:, 0:LANES] = heads.T.astype(o_ref.dtype)
    for j in range(0, 2 * D_MODEL, DOT_COLS):
        o_ref[:, LANES + j:LANES + j + DOT_COLS] = (
            g_ref[g_row0 + j:g_row0 + j + DOT_COLS, :].T.astype(o_ref.dtype))


def _repack_w_in(w_in_t):
    n_in, d = w_in_t.shape
    assert d == D_MODEL and n_in == COL_DT + N_HEADS + 2 * D_MODEL
    w_main = pl.pallas_call(
        _repack_main_kernel, grid=(COL_DT // DOT_COLS,),
        in_specs=[pl.BlockSpec((DOT_COLS, D_MODEL), lambda j: (j, 0))],
        out_specs=pl.BlockSpec((D_MODEL, DOT_COLS), lambda j: (0, j)),
        out_shape=jax.ShapeDtypeStruct((D_MODEL, COL_DT), BF16), name="repack_w_main",
    )(w_in_t)
    g_block = n_in // 4
    g_row = COL_DT + N_HEADS
    g_row0 = g_row % g_block
    assert n_in % g_block == 0 and g_block % 8 == 0 and g_row0 % 8 == 0
    assert g_row0 + 2 * D_MODEL <= g_block and COL_DT % N_HEADS == 0
    w_tail = pl.pallas_call(
        functools.partial(_repack_tail_kernel, g_row0=g_row0), grid=(1,),
        in_specs=[pl.BlockSpec((N_HEADS, D_MODEL), lambda i: (COL_DT // N_HEADS, 0)),
                  pl.BlockSpec((g_block, D_MODEL), lambda i: (g_row // g_block, 0))],
        out_specs=pl.BlockSpec((D_MODEL, LANES + 2 * D_MODEL), lambda i: (0, 0)),
        out_shape=jax.ShapeDtypeStruct((D_MODEL, LANES + 2 * D_MODEL), BF16), name="repack_w_tail",
        compiler_params=pltpu.CompilerParams(vmem_limit_bytes=VMEM_LIMIT_BYTES),
    )(w_in_t, w_in_t)
    return w_main, w_tail


def _adaln_kernel(c_ref, w_ref, b_ref, o_ref):
    c = c_ref[...]
    o_ref[...] = _dot(_silu(c).astype(BF16), w_ref[...]) + b_ref[...]


def _weight_refs(it):
    names = ("normg", "wmain", "wtail", "convw", "convb", "dtb", "alog", "dskip", "ssmg", "poolw",
             "pscale", "wpo", "wso", "wo", "fg", "ta", "e3")
    assert len(names) == N_WEIGHTS
    return types.SimpleNamespace(**{n: next(it) for n in names})


def _row_blocks(S, L, nrows):
    if L >= nrows:
        return [(slice(s, s + 1), slice(l0, l0 + nrows), slice(s * L + l0, s * L + l0 + nrows))
                for s in range(S) for l0 in range(0, L, nrows)]
    sb = nrows // L
    return [(slice(s0, s0 + sb), slice(0, L), slice(s0 * L, (s0 + sb) * L))
            for s0 in range(0, S, sb)]


def _norm_modulate(w, x_ref, mod_of, h_ref, S, L):
    for ss, ls, rs in _row_blocks(S, L, ROW_BLOCK):
        x3 = x_ref[ss, ls, :]
        mod3 = mod_of(ss)
        ms = jnp.mean(x3 * x3, axis=-1, keepdims=True)
        hn = x3 * lax.rsqrt(ms + EPS) * w.normg[...]
        hm = hn * (1.0 + mod3[:, :, D_MODEL:2 * D_MODEL]) + mod3[:, :, 0:D_MODEL]
        h_ref[rs, :] = hm.reshape(rs.stop - rs.start, D_MODEL).astype(BF16)


def _proj_chunks(col0, ncols):
    return [(col0 + j, min(DOT_COLS, ncols - j), j) for j in range(0, ncols, DOT_COLS)]


def _pool_windows(up3, t, pos0, S, L):
    for ss, ls, rs in _row_blocks(S, L, ROW_BLOCK):
        nseq, ntok = ss.stop - ss.start, ls.stop - ls.start
        pos = pos0 + ls.start + lax.broadcasted_iota(jnp.int32, (nseq, ntok, POOL_GROUP), 1)
        for gi, win in enumerate(POOL_WINDOWS):
            sl = slice(gi * POOL_GROUP, (gi + 1) * POOL_GROUP)
            s = up3[ss, ls.start:POOL_HDR + ls.stop, sl]
            u = s[:, POOL_HDR:, :]
            shift = 1
            while shift < win:
                s = s + pltpu.roll(s, shift, 1)
                shift *= 2
            cnt = jnp.minimum(pos + 1, win).astype(F32)
            t.pm[rs, sl] = (s[:, POOL_HDR:, :] / cnt - u).reshape(
                nseq * ntok, POOL_GROUP).astype(BF16)


def _pool_mix(w, zp, t):
    for gi in range(len(POOL_WINDOWS)):
        sl = slice(gi * POOL_GROUP, (gi + 1) * POOL_GROUP)
        yg = _dot(t.pm[:, sl], w.poolw[gi]) * w.pscale[:, sl]
        t.pm[:, sl] = (yg * _silu(zp[:, sl])).astype(BF16)
    for j in range(0, D_MODEL, DOT_COLS):
        t.p[:, j:j + DOT_COLS] = _dot(t.pm[...], w.wpo[:, j:j + DOT_COLS])


def _conv_chunk(w, xbc3, t, j, S, L):
    sl = slice(j, j + DOT_COLS)
    for ss, ls, rs in _row_blocks(S, L, CONV_ROW_BLOCK):
        ext = xbc3[ss, ls.start:CONV_HDR + ls.stop, sl]
        acc = w.convb[:, sl]
        for k in range(CONV_W):
            back = CONV_W - 1 - k
            tap = ext if back == 0 else pltpu.roll(ext, back, 1)
            acc = acc + tap[:, CONV_HDR:, :] * w.convw[k:k + 1, sl]
        v = _silu(acc).reshape(rs.stop - rs.start, DOT_COLS)
        if j < D_INNER:
            t.xs[rs, sl] = v
        elif j < D_INNER + N_BC_GROUPS * D_STATE:
            t.b[rs, :] = v
        else:
            t.c[rs, :] = v


def _ssd_prep(w, t, dt_ref, c, prompt):
    rows = pl.ds(c * CHUNK, CHUNK)
    lane = lax.broadcasted_iota(jnp.int32, (CHUNK, LANES), 1)
    neg_a = jnp.where(lane[0:1, :] < 3 * N_HEADS, -jnp.exp(w.alog[...]), 0.0)
    dtc = dt_ref[rows, :]
    a = dtc * neg_a
    cs = _dot(w.ta[...], _pack3(a, lane))
    cs = cs + pltpu.roll(cs, 2 * N_HEADS, 1)
    cs = cs + pltpu.roll(cs, N_HEADS, 1)
    acs = cs[0:CHUNK, :]
    last = cs[CHUNK:2 * CHUNK, :]
    exp_acs = jnp.exp(acs)
    wgt = jnp.exp(last - acs) * dtc
    t.ea[rows, :] = _dot(_pack3(exp_acs, lane), w.e3[...])
    t.we[rows, :] = _dot(_pack3(wgt, lane), w.e3[...])
    if not prompt:
        t.aux[c] = jnp.exp(last).T
    acs2 = acs * LOG2E
    t.hd[c, 0] = acs2
    t.hd[c, 1] = acs2.T
    t.hd[c, 2] = dtc.T


def _ssd_intra(w, t, c, QL, prompt):
    row_start = c * CHUNK
    rows = pl.ds(row_start, CHUNK)
    lane = lax.broadcasted_iota(jnp.int32, (CHUNK, LANES), 1)
    row = lax.broadcasted_iota(jnp.int32, (CHUNK, LANES), 0)
    acs2, acs2_t, dt_t = t.hd[c, 0], t.hd[c, 1], t.hd[c, 2]
    ql_bits = QL.bit_length() - 1
    mask = ((lax.shift_right_logical(row, ql_bits) == lax.shift_right_logical(lane, ql_bits))
            & (lane <= row))
    for g in range(N_BC_GROUPS):
        gs = slice(g * D_STATE, (g + 1) * D_STATE)
        gc = slice(g * GROUP_COLS, (g + 1) * GROUP_COLS)
        cg = t.c[rows, gs].astype(BF16)
        cb = _dot_nt(cg, t.b[rows, gs].astype(BF16))
        if prompt:
            ht = t.aux[:, gc]
            t.yoff[...] = _dot(cg, ht.astype(BF16))
            xw = (t.xs[rows, gc] * t.we[rows, gc]).astype(BF16)
            upd = _dot(t.b[rows, gs].T.astype(BF16), xw)
            t.aux[:, gc] = ht * t.ea[row_start + CHUNK - 1:row_start + CHUNK, gc] + upd
        for k in range(HEADS_PER_GROUP // 2):
            atts = []
            for h in (g * HEADS_PER_GROUP + 2 * k, g * HEADS_PER_GROUP + 2 * k + 1):
                seg2 = acs2[:, h:h + 1] - acs2_t[h:h + 1, :]
                decay = jnp.exp2(jnp.where(mask, seg2, -jnp.inf))
                atts.append((cb * decay * dt_t[h:h + 1, :]).astype(BF16))
            ps = slice((g * HEADS_PER_GROUP + 2 * k) * HEAD_DIM,
                       (g * HEADS_PER_GROUP + 2 * k + 2) * HEAD_DIM)
            xp = t.xs[rows, ps]
            rhs = jnp.concatenate([jnp.where(lane < HEAD_DIM, xp, 0.0).astype(BF16),
                                   jnp.where(lane >= HEAD_DIM, xp, 0.0).astype(BF16)], axis=0)
            yp = _dot(jnp.concatenate(atts, axis=1), rhs) + w.dskip[:, ps] * xp
            if prompt:
                yp = yp + t.yoff[:, 2 * k * HEAD_DIM:(2 * k + 2) * HEAD_DIM] * t.ea[rows, ps]
            t.yssd[rows, ps] = yp


def _sample_state_pair(t, r, QL, hins, houts):
    n = 2 * QL
    c = r // CHUNK
    r0 = r % CHUNK
    rows = pl.ds(pl.multiple_of(r, n), n)
    lane = lax.broadcasted_iota(jnp.int32, (CHUNK, LANES), 1)
    rowi = lax.broadcasted_iota(jnp.int32, (n, GROUP_COLS), 0)
    total_t = t.aux[c]
    for q in range(2):
        col = jnp.sum(jnp.where(lane == r0 + q * QL, total_t, 0.0), axis=1, keepdims=True)
        rb = jnp.broadcast_to(col, (CHUNK, LANES))
        mine = (rowi >= q * QL) & (rowi < (q + 1) * QL)
        for g in range(N_BC_GROUPS):
            gs = slice(g * D_STATE, (g + 1) * D_STATE)
            cs = slice(g * GROUP_COLS, (g + 1) * GROUP_COLS)
            hg = hins[q][cs, :]
            yoff = _dot_nt(t.c[rows, gs].astype(BF16), hg.astype(BF16)) * t.ea[rows, cs]
            t.yssd[rows, cs] = t.yssd[rows, cs] + jnp.where(mine, yoff, 0.0)
            xw = jnp.where(mine, t.xs[rows, cs] * t.we[rows, cs], 0.0)
            upd = _dot(xw.T.astype(BF16), t.b[rows, gs].astype(BF16))
            for e in range(HEADS_PER_GROUP):
                h = g * HEADS_PER_GROUP + e
                hs = slice(h * HEAD_DIM, (h + 1) * HEAD_DIM)
                es = slice(e * HEAD_DIM, (e + 1) * HEAD_DIM)
                houts[q][hs, :] = hg[es, :] * rb[h:h + 1, :] + upd[es, :]


def _dense_out(w, t, x_ref, gate_of, y_ref, S, L):
    R = S * L
    col_chunks = [slice(j, j + DOT_COLS) for j in range(0, D_INNER, DOT_COLS)]
    for r in range(0, R, ROW_BLOCK):
        rs = slice(r, r + ROW_BLOCK)
        ys = [t.yssd[rs, sl] * _silu(t.zs[rs, sl]) for sl in col_chunks]
        ssq = sum(jnp.sum(y * y, axis=-1, keepdims=True) for y in ys)
        inv = lax.rsqrt(ssq * (1.0 / D_INNER) + EPS)
        for y, sl in zip(ys, col_chunks):
            t.yn[rs, sl] = (y * inv * w.ssmg[:, sl]).astype(BF16)
    for j in range(0, D_MODEL, DOT_COLS):
        t.s[:, j:j + DOT_COLS] = _dot(t.yn[...], w.wso[:, j:j + DOT_COLS])
    for r in range(0, R, ROW_BLOCK):
        rs = slice(r, r + ROW_BLOCK)
        for j in range(0, D_MODEL, DOT_COLS):
            sl = slice(j, j + DOT_COLS)
            sl2 = slice(D_MODEL + j, D_MODEL + j + DOT_COLS)
            t.yn[rs, sl] = (_sigmoid(t.g[rs, sl]) * t.p[rs, sl]
                            + _sigmoid(t.g[rs, sl2]) * t.s[rs, sl]).astype(BF16)
    for j in range(0, D_MODEL, DOT_COLS):
        t.s[:, j:j + DOT_COLS] = _dot(t.yn[:, 0:D_MODEL], w.wo[:, j:j + DOT_COLS])
    for ss, ls, rs in _row_blocks(S, L, ROW_BLOCK):
        o3 = t.s[rs, :].reshape(ss.stop - ss.start, ls.stop - ls.start, D_MODEL)
        xn = x_ref[ss, ls, :] + gate_of(ss) * o3
        ms = jnp.mean(xn * xn, axis=-1, keepdims=True)
        y_ref[ss, ls, :] = xn * lax.rsqrt(ms + EPS) * w.fg[...]


def _ssd_refs(it):
    names = ("xs", "b", "c", "yssd", "pm", "p", "ea", "we", "hd", "aux", "yn", "s", "zs", "g")
    return types.SimpleNamespace(**{n: next(it) for n in names})


def _input_projection(w, h_ref, stores):
    for col0, (ncols, store) in stores.items():
        for col, wd, j in _proj_chunks(col0, ncols):
            if col < COL_DT:
                wc = w.wmain[:, col:col + wd]
            else:
                wc = w.wtail[:, col - COL_DT:col - COL_DT + wd]
            store(j, wd, _dot(h_ref[...], wc))


def _prompt_kernel(*refs, L):
    S = 1
    it = iter(refs)
    x_ref, mod_ref = next(it), next(it)
    w = _weight_refs(it)
    y_ref, pool_out_ref, conv_out_ref, ssm_out_ref = [next(it) for _ in range(4)]
    h_scr, up_scr, zp_scr, xbc_scr, dt_scr = [next(it) for _ in range(5)]
    t = _ssd_refs(it)
    t.yoff = next(it)
    step = pl.program_id(1)

    @pl.when(step == 0)
    def _():
        up_scr[:, 0:POOL_HDR, :] = jnp.zeros((S, POOL_HDR, D_POOL), F32)
        xbc_scr[:, 0:CONV_HDR, :] = jnp.zeros((S, CONV_HDR, D_XBC), F32)
        t.aux[...] = jnp.zeros(t.aux.shape, F32)

    _norm_modulate(w, x_ref, lambda ss: mod_ref[...], h_scr, S, L)

    def store_up(j, wd, v):
        up_scr[:, POOL_HDR:POOL_HDR + L, j:j + wd] = v.reshape(S, L, wd)

    def store_zp(j, wd, v):
        zp_scr[:, j:j + wd] = v

    def store_zs(j, wd, v):
        t.zs[:, j:j + wd] = v

    def store_xbc(j, wd, v):
        xbc_scr[:, CONV_HDR:CONV_HDR + L, j:j + wd] = v.reshape(S, L, wd)

    def store_dt(j, wd, v):
        dt_scr[...] = _softplus(v + w.dtb[...])

    def store_g(j, wd, v):
        t.g[:, j:j + wd] = v

    _input_projection(w, h_scr, {
        COL_UP: (D_POOL, store_up), COL_ZP: (D_POOL, store_zp), COL_DT: (LANES, store_dt),
        COL_XBC: (D_XBC, store_xbc), COL_ZS: (D_INNER, store_zs), COL_G: (2 * D_MODEL, store_g)})

    _pool_windows(up_scr, t, step * L, S, L)
    _pool_mix(w, zp_scr, t)
    up_scr[:, 0:POOL_HDR, :] = up_scr[:, L:L + POOL_HDR, :]

    for j in range(0, D_XBC, DOT_COLS):
        _conv_chunk(w, xbc_scr, t, j, S, L)
    xbc_scr[:, 0:CONV_HDR, :] = xbc_scr[:, L:L + CONV_HDR, :]

    for c in range(L // CHUNK):
        _ssd_prep(w, t, dt_scr, c, True)
    for c in range(L // CHUNK):
        _ssd_intra(w, t, c, CHUNK, True)

    _dense_out(w, t, x_ref, lambda ss: mod_ref[...][:, :, 2 * D_MODEL:3 * D_MODEL], y_ref, S, L)

    @pl.when(step == pl.num_programs(1) - 1)
    def _():
        for n in range(D_INNER // LANES):
            cols = slice(n * LANES, (n + 1) * LANES)
            ssm_out_ref[0, cols, :] = t.aux[:, cols].T
        batch = pl.ds(pl.program_id(0), 1)
        for r in range(POOL_BUF):
            pool_out_ref[r, batch, :] = up_scr[0, 1 + r:2 + r, :]
        for r in range(CONV_W - 1):
            r0 = CONV_HDR - (CONV_W - 1) + r
            conv_out_ref[r, batch, :] = xbc_scr[0, r0:r0 + 1, :]


def _sample_kernel(*refs, S, L, n_inner, seq_per_inner):
    R = S * L
    it = iter(refs)
    x_ref, mod_ref, pool_in_ref, conv_in_ref, ssm_in_ref = [next(it) for _ in range(5)]
    w = _weight_refs(it)
    y_ref, pool_out_ref, conv_out_ref, ssm_out_ref = [next(it) for _ in range(4)]
    h_scr, up_scr, zp_scr, xbc_scr, dt_scr = [next(it) for _ in range(5)]
    t = _ssd_refs(it)
    step = pl.program_id(1)

    @pl.when(step == 0)
    def _():
        _norm_modulate(w, x_ref, lambda ss: mod_ref[ss], h_scr, S, L)

        def store_up(j, wd, v):
            up_scr[:, POOL_HDR:POOL_HDR + L, j:j + wd] = v.reshape(S, L, wd)

        def store_zp(j, wd, v):
            zp_scr[:, j:j + wd] = v

        def store_zs(j, wd, v):
            t.zs[:, j:j + wd] = v

        def store_xbc(j, wd, v):
            xbc_scr[:, CONV_HDR:CONV_HDR + L, j:j + wd] = v.reshape(S, L, wd)

        def store_dt(j, wd, v):
            dt_scr[...] = _softplus(v + w.dtb[...])

        def store_g(j, wd, v):
            t.g[:, j:j + wd] = v

        _input_projection(w, h_scr, {
            COL_UP: (D_POOL, store_up), COL_ZP: (D_POOL, store_zp), COL_DT: (LANES, store_dt),
            COL_XBC: (D_XBC, store_xbc), COL_ZS: (D_INNER, store_zs),
            COL_G: (2 * D_MODEL, store_g)})

        for r in range(POOL_BUF):
            up_scr[:, 1 + r, :] = pool_in_ref[r]
        _pool_windows(up_scr, t, PAST_LEN, S, L)
        _pool_mix(w, zp_scr, t)
        for r in range(POOL_BUF):
            pool_out_ref[r] = up_scr[:, L + 1 + r, :]

        for r in range(CONV_W - 1):
            xbc_scr[:, CONV_HDR - (CONV_W - 1) + r, :] = conv_in_ref[r]
        for j in range(0, D_XBC, DOT_COLS):
            _conv_chunk(w, xbc_scr, t, j, S, L)
        for r in range(CONV_W - 1):
            conv_out_ref[r] = xbc_scr[:, L + CONV_HDR - (CONV_W - 1) + r, :]
        for c in range(R // CHUNK):
            _ssd_prep(w, t, dt_scr, c, False)
        for c in range(R // CHUNK):
            _ssd_intra(w, t, c, L, False)

    for q in range(0, seq_per_inner, 2):
        _sample_state_pair(t, (step * seq_per_inner + q) * L, L,
                           (ssm_in_ref.at[q], ssm_in_ref.at[q + 1]),
                           (ssm_out_ref.at[q], ssm_out_ref.at[q + 1]))

    @pl.when(step == n_inner - 1)
    def _():
        _dense_out(w, t, x_ref, lambda ss: mod_ref[ss][:, :, 2 * D_MODEL:3 * D_MODEL],
                   y_ref, S, L)


def _resident(shape):
    zeros = (0,) * len(shape)
    return pl.BlockSpec(shape, lambda *_: zeros, pipeline_mode=pl.Buffered(1))


def _ssd_scratch(R, aux_shape):
    return [
        pltpu.VMEM((R, D_INNER), F32),
        pltpu.VMEM((R, N_BC_GROUPS * D_STATE), F32),
        pltpu.VMEM((R, N_BC_GROUPS * D_STATE), F32),
        pltpu.VMEM((R, D_INNER), F32),
        pltpu.VMEM((R, D_POOL), BF16),
        pltpu.VMEM((R, D_MODEL), F32),
        pltpu.VMEM((R, D_INNER), F32),
        pltpu.VMEM((R, D_INNER), F32),
        pltpu.VMEM((R // CHUNK, 3, CHUNK, LANES), F32),
        pltpu.VMEM(aux_shape, F32),
        pltpu.VMEM((R, D_INNER), BF16),
        pltpu.VMEM((R, D_MODEL), F32),
        pltpu.VMEM((R, D_INNER), F32),
        pltpu.VMEM((R, 2 * D_MODEL), F32),
    ]


def _stage_scratch(S, L):
    return [
        pltpu.VMEM((S * L, D_MODEL), BF16),
        pltpu.VMEM((S, POOL_HDR + L, D_POOL), F32),
        pltpu.VMEM((S * L, D_POOL), F32),
        pltpu.VMEM((S, CONV_HDR + L, D_XBC), F32),
        pltpu.VMEM((S * L, LANES), F32),
    ]


def _prompt_call(x, mod, weights, *, L):
    nb, seq, _ = x.shape
    x_spec = pl.BlockSpec((1, L, D_MODEL), lambda b, t: (b, t, 0))
    seq_map = lambda b, t: (b, 0, 0)
    in_specs = ([x_spec, pl.BlockSpec((1, 1, 3 * D_MODEL), seq_map)]
                + [_resident(v.shape) for v in weights])
    whole = lambda b, t: (0, 0, 0)
    out_shape = (jax.ShapeDtypeStruct(x.shape, F32),
                 jax.ShapeDtypeStruct((POOL_BUF, nb, D_POOL), F32),
                 jax.ShapeDtypeStruct((CONV_W - 1, nb, D_XBC), F32),
                 jax.ShapeDtypeStruct((nb, D_INNER, D_STATE), F32))
    out_specs = (x_spec,
                 pl.BlockSpec((POOL_BUF, nb, D_POOL), whole),
                 pl.BlockSpec((CONV_W - 1, nb, D_XBC), whole),
                 pl.BlockSpec((1, D_INNER, D_STATE), seq_map))
    scratch = (_stage_scratch(1, L) + _ssd_scratch(L, (D_STATE, D_INNER))
               + [pltpu.VMEM((CHUNK, GROUP_COLS), F32)])
    kern = functools.partial(_prompt_kernel, L=L)
    return pl.pallas_call(
        kern, grid=(nb, seq // L), in_specs=in_specs, out_specs=out_specs, out_shape=out_shape,
        scratch_shapes=scratch, name="layer_prompt",
        compiler_params=pltpu.CompilerParams(
            dimension_semantics=("arbitrary", "arbitrary"), vmem_limit_bytes=VMEM_LIMIT_BYTES),
    )(x, mod, *weights)


def _sample_call(x, mod, states, weights, *, S, seq_per_inner):
    ns, L, _ = x.shape
    R = S * L
    n_inner = S // seq_per_inner
    seq_map = lambda o, i: (o, 0, 0)
    st_map = lambda o, i: (o * n_inner + i, 0, 0)
    time_major = lambda o, i: (0, o, 0)
    x_spec = pl.BlockSpec((S, L, D_MODEL), seq_map)
    pool_spec = pl.BlockSpec((POOL_BUF, S, D_POOL), time_major)
    conv_spec = pl.BlockSpec((CONV_W - 1, S, D_XBC), time_major)
    st_spec = pl.BlockSpec((seq_per_inner, D_INNER, D_STATE), st_map)
    in_specs = ([x_spec, pl.BlockSpec((S, 1, 3 * D_MODEL), seq_map), pool_spec, conv_spec, st_spec]
                + [_resident(v.shape) for v in weights])
    out_shape = (jax.ShapeDtypeStruct(x.shape, F32),
                 jax.ShapeDtypeStruct((POOL_BUF, ns, D_POOL), F32),
                 jax.ShapeDtypeStruct((CONV_W - 1, ns, D_XBC), F32),
                 jax.ShapeDtypeStruct((ns, D_INNER, D_STATE), F32))
    out_specs = (x_spec, pool_spec, conv_spec, st_spec)
    scratch = _stage_scratch(S, L) + _ssd_scratch(R, (R // CHUNK, LANES, CHUNK))
    kern = functools.partial(_sample_kernel, S=S, L=L, n_inner=n_inner,
                             seq_per_inner=seq_per_inner)
    return pl.pallas_call(
        kern, grid=(ns // S, n_inner), in_specs=in_specs, out_specs=out_specs,
        out_shape=out_shape, scratch_shapes=scratch, name="layer_sample",
        compiler_params=pltpu.CompilerParams(
            dimension_semantics=("arbitrary", "arbitrary"), vmem_limit_bytes=VMEM_LIMIT_BYTES),
    )(x, mod, *states, *weights)


def _block_sum_matrix(ql):
    r = jnp.arange(CHUNK)[:, None]
    c = jnp.arange(CHUNK)[None, :]
    same = (r // ql) == (c // ql)
    return jnp.concatenate([same & (c <= r), same], axis=0).astype(BF16)


def _expand_matrix():
    k = jnp.arange(LANES)[:, None]
    c = jnp.arange(D_INNER)[None, :] // HEAD_DIM
    return ((k % N_HEADS == c) & (k < 3 * N_HEADS)).astype(BF16)


def kernel(x_prompt, x_sample, state_pool, state_conv, state_ssm, c_prompt, c_sample, w_ada, b_ada, norm_g, w_in, conv_w, conv_b, dt_bias, a_log, d_skip, ssm_norm_g, pool_w, pool_scale, w_pool_out, w_ssm_out, w_o, final_g):
    depth = w_ada.shape[0]
    assert depth == 1, "kernel is written for a single layer"
    nb, seq, _ = x_prompt.shape
    ns, dec_seq, _ = x_sample.shape
    l = 0

    def head_lanes(v):
        return jnp.pad(jnp.tile(v, (1, 3)), ((0, 0), (0, LANES - 3 * N_HEADS)))

    c_all = jnp.concatenate([c_prompt, c_sample], axis=0)
    mod = pl.pallas_call(
        _adaln_kernel, out_shape=jax.ShapeDtypeStruct((nb + ns, 3 * D_MODEL), F32), name="adaln",
        compiler_params=pltpu.CompilerParams(vmem_limit_bytes=VMEM_LIMIT_BYTES),
    )(c_all, w_ada[l].astype(BF16), b_ada[l][None, :])
    mod = mod[:, None, :]

    e3 = _expand_matrix()
    w_main, w_tail = _repack_w_in(jnp.transpose(w_in[l]))

    def weights(ql):
        return [
            norm_g[l][None, :],
            w_main, w_tail,
            conv_w[l], conv_b[l][None, :],
            head_lanes(dt_bias[l][None, :]),
            head_lanes(a_log[l][None, :]),
            jnp.repeat(d_skip[l], HEAD_DIM)[None, :],
            ssm_norm_g[l][None, :],
            pool_w[l].astype(BF16), pool_scale[l][None, :],
            w_pool_out[l].astype(BF16), w_ssm_out[l].astype(BF16), w_o[l].astype(BF16),
            final_g[None, :],
            _block_sum_matrix(ql), e3,
        ]

    tile_rows = 2 * CHUNK
    y_p, pool_p, conv_p, ssm_p = _prompt_call(x_prompt, mod[:nb], weights(CHUNK), L=tile_rows)

    swap = lambda v: jnp.transpose(v, (1, 0, 2))
    states = (swap(state_pool[l]), swap(state_conv[l]),
              state_ssm[l].reshape(ns, D_INNER, D_STATE))
    y_s, pool_s, conv_s, ssm_s = _sample_call(
        x_sample, mod[nb:], states, weights(dec_seq), S=CHUNK // dec_seq, seq_per_inner=2)

    shp = (N_HEADS, HEAD_DIM, D_STATE)
    return (y_p, y_s, swap(pool_p)[None], swap(conv_p)[None], ssm_p.reshape((1, nb) + shp),
            swap(pool_s)[None], swap(conv_s)[None], ssm_s.reshape((1, ns) + shp))
```

```python
import functools
import types

import jax
import jax.numpy as jnp
from jax import lax
from jax.experimental import pallas as pl
from jax.experimental.pallas import tpu as pltpu

F32 = jnp.float32
BF16 = jnp.bfloat16

EPS = 1e-6
D_MODEL = 1024
POOL_WINDOWS = (2, 4, 8, 16)
POOL_GROUP = 128
D_POOL = POOL_GROUP * len(POOL_WINDOWS)
POOL_BUF = max(POOL_WINDOWS) - 1
D_INNER = 2048
HEAD_DIM = 64
N_HEADS = D_INNER // HEAD_DIM
N_BC_GROUPS = 4
HEADS_PER_GROUP = N_HEADS // N_BC_GROUPS
GROUP_COLS = HEADS_PER_GROUP * HEAD_DIM
D_STATE = 128
CONV_W = 4
D_XBC = D_INNER + 2 * N_BC_GROUPS * D_STATE
PAST_LEN = 16384

LANES = 128
CHUNK = 128
POOL_HDR = 16
CONV_HDR = 8
DOT_COLS = 512
ROW_BLOCK = 16
CONV_ROW_BLOCK = 32
VMEM_LIMIT_BYTES = 58 * 1024 * 1024

COL_UP = 0
COL_ZP = D_POOL
COL_ZS = 2 * D_POOL
COL_XBC = COL_ZS + D_INNER
COL_DT = COL_XBC + D_XBC
COL_G = COL_DT + LANES
N_WEIGHTS = 17
LOG2E = 1.4426950408889634


def _dot(a, b):
    return jnp.dot(a, b, preferred_element_type=F32)


def _dot_nt(a, b):
    return lax.dot_general(a, b, (((1,), (1,)), ((), ())), preferred_element_type=F32)


def _pack3(v, lane):
    hi = v.astype(BF16).astype(F32)
    r = v - hi
    mid = r.astype(BF16).astype(F32)
    lo = r - mid
    packed = jnp.where(lane < N_HEADS, hi,
                       jnp.where(lane < 2 * N_HEADS, mid,
                                 jnp.where(lane < 3 * N_HEADS, lo, 0.0)))
    return packed.astype(BF16)


def _silu(v):
    return v / (1.0 + jnp.exp2(v * -LOG2E))


def _sigmoid(v):
    return 1.0 / (1.0 + jnp.exp2(v * -LOG2E))


def _softplus(v):
    return jnp.maximum(v, 0.0) + jnp.log1p(jnp.exp(-jnp.abs(v)))


def _repack_main_kernel(x_ref, o_ref):
    o_ref[...] = x_ref[...].T.astype(o_ref.dtype)


def _repack_tail_kernel(dt_ref, g_ref, o_ref, *, g_row0):
    dt = dt_ref[...]
    heads = jnp.concatenate([dt, dt, dt, jnp.zeros((LANES - 3 * N_HEADS, D_MODEL), F32)], axis=0)
    o_ref[:, 0:LANES] = heads.T.astype(o_ref.dtype)
    for j in range(0, 2 * D_MODEL, DOT_COLS):
        o_ref[:, LANES + j:LANES + j + DOT_COLS] = (
            g_ref[g_row0 + j:g_row0 + j + DOT_COLS, :].T.astype(o_ref.dtype))


def _repack_w_in(w_in_t):
    n_in, d = w_in_t.shape
    assert d == D_MODEL and n_in == COL_DT + N_HEADS + 2 * D_MODEL
    w_main = pl.pallas_call(
        _repack_main_kernel, grid=(COL_DT // DOT_COLS,),
        in_specs=[pl.BlockSpec((DOT_COLS, D_MODEL), lambda j: (j, 0))],
        out_specs=pl.BlockSpec((D_MODEL, DOT_COLS), lambda j: (0, j)),
        out_shape=jax.ShapeDtypeStruct((D_MODEL, COL_DT), BF16), name="repack_w_main",
    )(w_in_t)
    g_block = n_in // 4
    g_row = COL_DT + N_HEADS
    g_row0 = g_row % g_block
    assert n_in % g_block == 0 and g_block % 8 == 0 and g_row0 % 8 == 0
    assert g_row0 + 2 * D_MODEL <= g_block and COL_DT % N_HEADS == 0
    w_tail = pl.pallas_call(
        functools.partial(_repack_tail_kernel, g_row0=g_row0), grid=(1,),
        in_specs=[pl.BlockSpec((N_HEADS, D_MODEL), lambda i: (COL_DT // N_HEADS, 0)),
                  pl.BlockSpec((g_block, D_MODEL), lambda i: (g_row // g_block, 0))],
        out_specs=pl.BlockSpec((D_MODEL, LANES + 2 * D_MODEL), lambda i: (0, 0)),
        out_shape=jax.ShapeDtypeStruct((D_MODEL, LANES + 2 * D_MODEL), BF16), name="repack_w_tail",
        compiler_params=pltpu.CompilerParams(vmem_limit_bytes=VMEM_LIMIT_BYTES),
    )(w_in_t, w_in_t)
    return w_main, w_tail


def _adaln_kernel(c_ref, w_ref, b_ref, o_ref):
    c = c_ref[...]
    o_ref[...] = _dot(_silu(c).astype(BF16), w_ref[...]) + b_ref[...]


def _weight_refs(it):
    names = ("normg", "wmain", "wtail", "convw", "convb", "dtb", "alog", "dskip", "ssmg", "poolw",
             "pscale", "wpo", "wso", "wo", "fg", "ta", "e3")
    assert len(names) == N_WEIGHTS
    return types.SimpleNamespace(**{n: next(it) for n in names})


def _row_blocks(S, L, nrows):
    if L >= nrows:
        return [(slice(s, s + 1), slice(l0, l0 + nrows), slice(s * L + l0, s * L + l0 + nrows))
                for s in range(S) for l0 in range(0, L, nrows)]
    sb = nrows // L
    return [(slice(s0, s0 + sb), slice(0, L), slice(s0 * L, (s0 + sb) * L))
            for s0 in range(0, S, sb)]


def _norm_modulate(w, x_ref, mod_of, h_ref, S, L):
    for ss, ls, rs in _row_blocks(S, L, ROW_BLOCK):
        x3 = x_ref[ss, ls, :]
        mod3 = mod_of(ss)
        ms = jnp.mean(x3 * x3, axis=-1, keepdims=True)
        hn = x3 * lax.rsqrt(ms + EPS) * w.normg[...]
        hm = hn * (1.0 + mod3[:, :, D_MODEL:2 * D_MODEL]) + mod3[:, :, 0:D_MODEL]
        h_ref[rs, :] = hm.reshape(rs.stop - rs.start, D_MODEL).astype(BF16)


def _roll_rows(x, d, drop_first=False):
    n, rows, lanes = x.shape
    tiles = [x[:, j:j + 8, :] for j in range(0, rows, 8)]
    first = 1 if drop_first else 0
    if d == 8:
        out = [tiles[j - 1] for j in range(first, len(tiles))]
    else:
        rolled = [pltpu.roll(tl, d, 1) for tl in tiles]
        sub = lax.broadcasted_iota(jnp.int32, (n, 8, lanes), 1)
        out = [jnp.where(sub < d, rolled[j - 1], rolled[j]) for j in range(first, len(tiles))]
    return jnp.concatenate(out, axis=1)


def _proj_chunks(col0, ncols):
    return [(col0 + j, min(DOT_COLS, ncols - j), j) for j in range(0, ncols, DOT_COLS)]


def _pool_windows(up3, t, pos0, S, L):
    for ss, ls, rs in _row_blocks(S, L, ROW_BLOCK):
        nseq, ntok = ss.stop - ss.start, ls.stop - ls.start
        pos = pos0 + ls.start + lax.broadcasted_iota(jnp.int32, (nseq, ntok, POOL_GROUP), 1)
        for gi, win in enumerate(POOL_WINDOWS):
            sl = slice(gi * POOL_GROUP, (gi + 1) * POOL_GROUP)
            s = up3[ss, ls.start:POOL_HDR + ls.stop, sl]
            u = s[:, POOL_HDR:, :]
            shift = 1
            while shift < win:
                s = s + _roll_rows(s, shift)
                shift *= 2
            cnt = jnp.minimum(pos + 1, win).astype(F32)
            t.pm[rs, sl] = (s[:, POOL_HDR:, :] / cnt - u).reshape(
                nseq * ntok, POOL_GROUP).astype(BF16)


def _pool_mix(w, zp, t):
    for gi in range(len(POOL_WINDOWS)):
        sl = slice(gi * POOL_GROUP, (gi + 1) * POOL_GROUP)
        yg = _dot(t.pm[:, sl], w.poolw[gi]) * w.pscale[:, sl]
        t.pm[:, sl] = (yg * _silu(zp[:, sl])).astype(BF16)
    for j in range(0, D_MODEL, DOT_COLS):
        t.p[:, j:j + DOT_COLS] = _dot(t.pm[...], w.wpo[:, j:j + DOT_COLS])


def _conv_chunk(w, xbc3, t, j, S, L):
    sl = slice(j, j + DOT_COLS)
    tile_of = lambda row: jnp.broadcast_to(row[None], (1, 8, DOT_COLS))
    taps_w = [tile_of(w.convw[k:k + 1, sl]) for k in range(CONV_W)]
    bias = tile_of(w.convb[:, sl])
    for ss, ls, rs in _row_blocks(S, L, CONV_ROW_BLOCK):
        ext = xbc3[ss, ls.start:CONV_HDR + ls.stop, sl]
        taps = [ext[:, CONV_HDR:, :] if k == CONV_W - 1
                else _roll_rows(ext, CONV_W - 1 - k, drop_first=True) for k in range(CONV_W)]
        out_tiles = []
        for r in range(0, ls.stop - ls.start, 8):
            acc = bias
            for k in range(CONV_W):
                acc = acc + taps[k][:, r:r + 8, :] * taps_w[k]
            out_tiles.append(_silu(acc))
        v = jnp.concatenate(out_tiles, axis=1).reshape(rs.stop - rs.start, DOT_COLS)
        if j < D_INNER:
            t.xs[rs, sl] = v
        elif j < D_INNER + N_BC_GROUPS * D_STATE:
            t.b[rs, :] = v
        else:
            t.c[rs, :] = v


def _ssd_prep(w, t, dt_ref, c, prompt):
    rows = pl.ds(c * CHUNK, CHUNK)
    lane = lax.broadcasted_iota(jnp.int32, (CHUNK, LANES), 1)
    neg_a = jnp.where(lane[0:1, :] < 3 * N_HEADS, -jnp.exp(w.alog[...]), 0.0)
    dtc = dt_ref[rows, :]
    a = dtc * neg_a
    cs = _dot(w.ta[...], _pack3(a, lane))
    cs = cs + pltpu.roll(cs, 2 * N_HEADS, 1)
    cs = cs + pltpu.roll(cs, N_HEADS, 1)
    acs = cs[0:CHUNK, :]
    last = cs[CHUNK:2 * CHUNK, :]
    exp_acs = jnp.exp(acs)
    wgt = jnp.exp(last - acs) * dtc
    t.ea[rows, :] = _dot(_pack3(exp_acs, lane), w.e3[...])
    t.we[rows, :] = _dot(_pack3(wgt, lane), w.e3[...])
    if not prompt:
        t.aux[c] = jnp.exp(last).T
    acs2 = acs * LOG2E
    t.hd[c, 0] = acs2
    t.hd[c, 1] = (acs2 - jnp.log2(dtc)).T


def _ssd_intra(w, t, c, QL, prompt, side_matmul=lambda: None):
    row_start = c * CHUNK
    rows = pl.ds(row_start, CHUNK)
    lane = lax.broadcasted_iota(jnp.int32, (CHUNK, LANES), 1)
    row = lax.broadcasted_iota(jnp.int32, (CHUNK, LANES), 0)
    acs2, shifted_t = t.hd[c, 0], t.hd[c, 1]
    ql_bits = QL.bit_length() - 1
    mask = ((lax.shift_right_logical(row, ql_bits) == lax.shift_right_logical(lane, ql_bits))
            & (lane <= row))
    for g in range(N_BC_GROUPS):
        gs = slice(g * D_STATE, (g + 1) * D_STATE)
        gc = slice(g * GROUP_COLS, (g + 1) * GROUP_COLS)
        cg = t.c[rows, gs].astype(BF16)
        cb = _dot_nt(cg, t.b[rows, gs].astype(BF16))
        if prompt:
            ht = t.aux[:, gc]
            t.yoff[...] = _dot(cg, ht.astype(BF16))
            xw = (t.xs[rows, gc] * t.we[rows, gc]).astype(BF16)
            upd = _dot(t.b[rows, gs].T.astype(BF16), xw)
            t.aux[:, gc] = ht * t.ea[row_start + CHUNK - 1:row_start + CHUNK, gc] + upd
        for k in range(HEADS_PER_GROUP // 2):
            atts = []
            for h in (g * HEADS_PER_GROUP + 2 * k, g * HEADS_PER_GROUP + 2 * k + 1):
                seg2 = acs2[:, h:h + 1] - shifted_t[h:h + 1, :]
                decay_dt = jnp.exp2(jnp.where(mask, seg2, -jnp.inf))
                atts.append((cb * decay_dt).astype(BF16))
            ps = slice((g * HEADS_PER_GROUP + 2 * k) * HEAD_DIM,
                       (g * HEADS_PER_GROUP + 2 * k + 2) * HEAD_DIM)
            xp = t.xs[rows, ps]
            rhs = jnp.concatenate([jnp.where(lane < HEAD_DIM, xp, 0.0).astype(BF16),
                                   jnp.where(lane >= HEAD_DIM, xp, 0.0).astype(BF16)], axis=0)
            if k == HEADS_PER_GROUP // 2 - 1:
                side_matmul()
            yp = _dot(jnp.concatenate(atts, axis=1), rhs) + w.dskip[:, ps] * xp
            if prompt:
                yp = yp + t.yoff[:, 2 * k * HEAD_DIM:(2 * k + 2) * HEAD_DIM] * t.ea[rows, ps]
            t.yssd[rows, ps] = yp


def _sample_state_pair(t, r, QL, hins, houts):
    n = 2 * QL
    c = r // CHUNK
    r0 = r % CHUNK
    rows = pl.ds(pl.multiple_of(r, n), n)
    lane = lax.broadcasted_iota(jnp.int32, (CHUNK, LANES), 1)
    rowi = lax.broadcasted_iota(jnp.int32, (n, GROUP_COLS), 0)
    total_t = t.aux[c]
    for q in range(2):
        col = jnp.sum(jnp.where(lane == r0 + q * QL, total_t, 0.0), axis=1, keepdims=True)
        rb = jnp.broadcast_to(col, (CHUNK, LANES))
        mine = (rowi >= q * QL) & (rowi < (q + 1) * QL)
        for g in range(N_BC_GROUPS):
            gs = slice(g * D_STATE, (g + 1) * D_STATE)
            cs = slice(g * GROUP_COLS, (g + 1) * GROUP_COLS)
            hg = hins[q][cs, :]
            yoff = _dot_nt(t.c[rows, gs].astype(BF16), hg.astype(BF16)) * t.ea[rows, cs]
            t.yssd[rows, cs] = t.yssd[rows, cs] + jnp.where(mine, yoff, 0.0)
            xw = jnp.where(mine, t.xs[rows, cs] * t.we[rows, cs], 0.0)
            upd = _dot(xw.T.astype(BF16), t.b[rows, gs].astype(BF16))
            for e in range(HEADS_PER_GROUP):
                h = g * HEADS_PER_GROUP + e
                hs = slice(h * HEAD_DIM, (h + 1) * HEAD_DIM)
                es = slice(e * HEAD_DIM, (e + 1) * HEAD_DIM)
                houts[q][hs, :] = hg[es, :] * rb[h:h + 1, :] + upd[es, :]


def _dense_out(w, t, x_ref, gate_of, y_ref, S, L):
    R = S * L
    col_chunks = [slice(j, j + DOT_COLS) for j in range(0, D_INNER, DOT_COLS)]
    for r in range(0, R, ROW_BLOCK):
        rs = slice(r, r + ROW_BLOCK)
        ys = [t.yssd[rs, sl] * _silu(t.zs[rs, sl]) for sl in col_chunks]
        ssq = sum(jnp.sum(y * y, axis=-1, keepdims=True) for y in ys)
        inv = lax.rsqrt(ssq * (1.0 / D_INNER) + EPS)
        for y, sl in zip(ys, col_chunks):
            t.yn[rs, sl] = (y * inv * w.ssmg[:, sl]).astype(BF16)
    for j in range(0, D_MODEL, DOT_COLS):
        t.s[:, j:j + DOT_COLS] = _dot(t.yn[...], w.wso[:, j:j + DOT_COLS])
    for r in range(0, R, ROW_BLOCK):
        rs = slice(r, r + ROW_BLOCK)
        for j in range(0, D_MODEL, DOT_COLS):
            sl = slice(j, j + DOT_COLS)
            sl2 = slice(D_MODEL + j, D_MODEL + j + DOT_COLS)
            t.yn[rs, sl] = (_sigmoid(t.g[rs, sl]) * t.p[rs, sl]
                            + _sigmoid(t.g[rs, sl2]) * t.s[rs, sl]).astype(BF16)
    for j in range(0, D_MODEL, DOT_COLS):
        t.s[:, j:j + DOT_COLS] = _dot(t.yn[:, 0:D_MODEL], w.wo[:, j:j + DOT_COLS])
    for ss, ls, rs in _row_blocks(S, L, ROW_BLOCK):
        o3 = t.s[rs, :].reshape(ss.stop - ss.start, ls.stop - ls.start, D_MODEL)
        xn = x_ref[ss, ls, :] + gate_of(ss) * o3
        ms = jnp.mean(xn * xn, axis=-1, keepdims=True)
        y_ref[ss, ls, :] = xn * lax.rsqrt(ms + EPS) * w.fg[...]


def _ssd_refs(it):
    names = ("xs", "b", "c", "yssd", "pm", "p", "ea", "we", "hd", "aux", "yn", "s", "zs", "g")
    return types.SimpleNamespace(**{n: next(it) for n in names})


def _input_projection(w, h_ref, stores):
    for col0, (ncols, store) in stores.items():
        for col, wd, j in _proj_chunks(col0, ncols):
            if col < COL_DT:
                wc = w.wmain[:, col:col + wd]
            else:
                wc = w.wtail[:, col - COL_DT:col - COL_DT + wd]
            store(j, wd, _dot(h_ref[...], wc))


def _prompt_kernel(*refs, L):
    S = 1
    it = iter(refs)
    x_ref, mod_ref = next(it), next(it)
    w = _weight_refs(it)
    y_ref, pool_out_ref, conv_out_ref, ssm_out_ref = [next(it) for _ in range(4)]
    h_scr, up_scr, zp_scr, xbc_scr, dt_scr = [next(it) for _ in range(5)]
    t = _ssd_refs(it)
    t.yoff = next(it)
    step = pl.program_id(1)

    @pl.when(step == 0)
    def _():
        up_scr[:, 0:POOL_HDR, :] = jnp.zeros((S, POOL_HDR, D_POOL), F32)
        xbc_scr[:, 0:CONV_HDR, :] = jnp.zeros((S, CONV_HDR, D_XBC), F32)
        t.aux[...] = jnp.zeros(t.aux.shape, F32)

    _norm_modulate(w, x_ref, lambda ss: mod_ref[...], h_scr, S, L)

    def store_up(j, wd, v):
        up_scr[:, POOL_HDR:POOL_HDR + L, j:j + wd] = v.reshape(S, L, wd)

    def store_zp(j, wd, v):
        zp_scr[:, j:j + wd] = v

    def store_zs(j, wd, v):
        t.zs[:, j:j + wd] = v

    def store_xbc(j, wd, v):
        xbc_scr[:, CONV_HDR:CONV_HDR + L, j:j + wd] = v.reshape(S, L, wd)

    def store_dt(j, wd, v):
        dt_scr[...] = _softplus(v + w.dtb[...])

    def store_g(j, wd, v):
        t.g[:, j:j + wd] = v

    _input_projection(w, h_scr, {
        COL_UP: (D_POOL, store_up), COL_ZP: (D_POOL, store_zp), COL_DT: (LANES, store_dt),
        COL_XBC: (D_XBC, store_xbc)})
    gate_chunks = [(col, wd, j, store)
                   for col0, ncols, store in ((COL_ZS, D_INNER, store_zs),
                                              (COL_G, 2 * D_MODEL, store_g))
                   for col, wd, j in _proj_chunks(col0, ncols)]

    def gate_chunk():
        if gate_chunks:
            col, wd, j, store = gate_chunks.pop(0)
            _input_projection(w, h_scr, {col: (wd, lambda _, wd_, v: store(j, wd_, v))})

    _pool_windows(up_scr, t, step * L, S, L)
    _pool_mix(w, zp_scr, t)
    up_scr[:, 0:POOL_HDR, :] = up_scr[:, L:L + POOL_HDR, :]

    for j in range(0, D_XBC, DOT_COLS):
        _conv_chunk(w, xbc_scr, t, j, S, L)
    xbc_scr[:, 0:CONV_HDR, :] = xbc_scr[:, L:L + CONV_HDR, :]

    for c in range(L // CHUNK):
        _ssd_prep(w, t, dt_scr, c, True)
    for c in range(L // CHUNK):
        _ssd_intra(w, t, c, CHUNK, True, gate_chunk)
    while gate_chunks:
        gate_chunk()

    _dense_out(w, t, x_ref, lambda ss: mod_ref[...][:, :, 2 * D_MODEL:3 * D_MODEL], y_ref, S, L)

    @pl.when(step == pl.num_programs(1) - 1)
    def _():
        for n in range(D_INNER // LANES):
            cols = slice(n * LANES, (n + 1) * LANES)
            ssm_out_ref[0, cols, :] = t.aux[:, cols].T
        batch = pl.ds(pl.program_id(0), 1)
        for r in range(POOL_BUF):
            pool_out_ref[r, batch, :] = up_scr[0, 1 + r:2 + r, :]
        for r in range(CONV_W - 1):
            r0 = CONV_HDR - (CONV_W - 1) + r
            conv_out_ref[r, batch, :] = xbc_scr[0, r0:r0 + 1, :]


def _sample_kernel(*refs, S, L, n_inner, seq_per_inner):
    R = S * L
    it = iter(refs)
    x_ref, mod_ref, pool_in_ref, conv_in_ref, ssm_in_ref = [next(it) for _ in range(5)]
    w = _weight_refs(it)
    y_ref, pool_out_ref, conv_out_ref, ssm_out_ref = [next(it) for _ in range(4)]
    h_scr, up_scr, zp_scr, xbc_scr, dt_scr = [next(it) for _ in range(5)]
    t = _ssd_refs(it)
    step = pl.program_id(1)

    @pl.when(step == 0)
    def _():
        _norm_modulate(w, x_ref, lambda ss: mod_ref[ss], h_scr, S, L)

        def store_up(j, wd, v):
            up_scr[:, POOL_HDR:POOL_HDR + L, j:j + wd] = v.reshape(S, L, wd)

        def store_zp(j, wd, v):
            zp_scr[:, j:j + wd] = v

        def store_zs(j, wd, v):
            t.zs[:, j:j + wd] = v

        def store_xbc(j, wd, v):
            xbc_scr[:, CONV_HDR:CONV_HDR + L, j:j + wd] = v.reshape(S, L, wd)

        def store_dt(j, wd, v):
            dt_scr[...] = _softplus(v + w.dtb[...])

        def store_g(j, wd, v):
            t.g[:, j:j + wd] = v

        _input_projection(w, h_scr, {
            COL_UP: (D_POOL, store_up), COL_ZP: (D_POOL, store_zp), COL_DT: (LANES, store_dt),
            COL_XBC: (D_XBC, store_xbc), COL_ZS: (D_INNER, store_zs),
            COL_G: (2 * D_MODEL, store_g)})

        for r in range(POOL_BUF):
            up_scr[:, 1 + r, :] = pool_in_ref[r]
        _pool_windows(up_scr, t, PAST_LEN, S, L)
        _pool_mix(w, zp_scr, t)
        for r in range(POOL_BUF):
            pool_out_ref[r] = up_scr[:, L + 1 + r, :]

        for r in range(CONV_W - 1):
            xbc_scr[:, CONV_HDR - (CONV_W - 1) + r, :] = conv_in_ref[r]
        for j in range(0, D_XBC, DOT_COLS):
            _conv_chunk(w, xbc_scr, t, j, S, L)
        for r in range(CONV_W - 1):
            conv_out_ref[r] = xbc_scr[:, L + CONV_HDR - (CONV_W - 1) + r, :]
        for c in range(R // CHUNK):
            _ssd_prep(w, t, dt_scr, c, False)
        for c in range(R // CHUNK):
            _ssd_intra(w, t, c, L, False)

    for q in range(0, seq_per_inner, 2):
        _sample_state_pair(t, (step * seq_per_inner + q) * L, L,
                           (ssm_in_ref.at[q], ssm_in_ref.at[q + 1]),
                           (ssm_out_ref.at[q], ssm_out_ref.at[q + 1]))

    @pl.when(step == n_inner - 1)
    def _():
        _dense_out(w, t, x_ref, lambda ss: mod_ref[ss][:, :, 2 * D_MODEL:3 * D_MODEL],
                   y_ref, S, L)


def _resident(shape):
    zeros = (0,) * len(shape)
    return pl.BlockSpec(shape, lambda *_: zeros, pipeline_mode=pl.Buffered(1))


def _ssd_scratch(R, aux_shape):
    return [
        pltpu.VMEM((R, D_INNER), F32),
        pltpu.VMEM((R, N_BC_GROUPS * D_STATE), F32),
        pltpu.VMEM((R, N_BC_GROUPS * D_STATE), F32),
        pltpu.VMEM((R, D_INNER), F32),
        pltpu.VMEM((R, D_POOL), BF16),
        pltpu.VMEM((R, D_MODEL), F32),
        pltpu.VMEM((R, D_INNER), F32),
        pltpu.VMEM((R, D_INNER), F32),
        pltpu.VMEM((R // CHUNK, 2, CHUNK, LANES), F32),
        pltpu.VMEM(aux_shape, F32),
        pltpu.VMEM((R, D_INNER), BF16),
        pltpu.VMEM((R, D_MODEL), F32),
        pltpu.VMEM((R, D_INNER), F32),
        pltpu.VMEM((R, 2 * D_MODEL), F32),
    ]


def _stage_scratch(S, L):
    return [
        pltpu.VMEM((S * L, D_MODEL), BF16),
        pltpu.VMEM((S, POOL_HDR + L, D_POOL), F32),
        pltpu.VMEM((S * L, D_POOL), F32),
        pltpu.VMEM((S, CONV_HDR + L, D_XBC), F32),
        pltpu.VMEM((S * L, LANES), F32),
    ]


def _prompt_call(x, mod, weights, *, L):
    nb, seq, _ = x.shape
    x_spec = pl.BlockSpec((1, L, D_MODEL), lambda b, t: (b, t, 0))
    seq_map = lambda b, t: (b, 0, 0)
    in_specs = ([x_spec, pl.BlockSpec((1, 1, 3 * D_MODEL), seq_map)]
                + [_resident(v.shape) for v in weights])
    whole = lambda b, t: (0, 0, 0)
    out_shape = (jax.ShapeDtypeStruct(x.shape, F32),
                 jax.ShapeDtypeStruct((POOL_BUF, nb, D_POOL), F32),
                 jax.ShapeDtypeStruct((CONV_W - 1, nb, D_XBC), F32),
                 jax.ShapeDtypeStruct((nb, D_INNER, D_STATE), F32))
    out_specs = (x_spec,
                 pl.BlockSpec((POOL_BUF, nb, D_POOL), whole),
                 pl.BlockSpec((CONV_W - 1, nb, D_XBC), whole),
                 pl.BlockSpec((1, D_INNER, D_STATE), seq_map))
    scratch = (_stage_scratch(1, L) + _ssd_scratch(L, (D_STATE, D_INNER))
               + [pltpu.VMEM((CHUNK, GROUP_COLS), F32)])
    kern = functools.partial(_prompt_kernel, L=L)
    return pl.pallas_call(
        kern, grid=(nb, seq // L), in_specs=in_specs, out_specs=out_specs, out_shape=out_shape,
        scratch_shapes=scratch, name="layer_prompt",
        compiler_params=pltpu.CompilerParams(
            dimension_semantics=("arbitrary", "arbitrary"), vmem_limit_bytes=VMEM_LIMIT_BYTES),
    )(x, mod, *weights)


def _sample_call(x, mod, states, weights, *, S, seq_per_inner):
    ns, L, _ = x.shape
    R = S * L
    n_inner = S // seq_per_inner
    seq_map = lambda o, i: (o, 0, 0)
    st_map = lambda o, i: (o * n_inner + i, 0, 0)
    time_major = lambda o, i: (0, o, 0)
    x_spec = pl.BlockSpec((S, L, D_MODEL), seq_map)
    pool_spec = pl.BlockSpec((POOL_BUF, S, D_POOL), time_major)
    conv_spec = pl.BlockSpec((CONV_W - 1, S, D_XBC), time_major)
    st_spec = pl.BlockSpec((seq_per_inner, D_INNER, D_STATE), st_map)
    in_specs = ([x_spec, pl.BlockSpec((S, 1, 3 * D_MODEL), seq_map), pool_spec, conv_spec, st_spec]
                + [_resident(v.shape) for v in weights])
    out_shape = (jax.ShapeDtypeStruct(x.shape, F32),
                 jax.ShapeDtypeStruct((POOL_BUF, ns, D_POOL), F32),
                 jax.ShapeDtypeStruct((CONV_W - 1, ns, D_XBC), F32),
                 jax.ShapeDtypeStruct((ns, D_INNER, D_STATE), F32))
    out_specs = (x_spec, pool_spec, conv_spec, st_spec)
    scratch = _stage_scratch(S, L) + _ssd_scratch(R, (R // CHUNK, LANES, CHUNK))
    kern = functools.partial(_sample_kernel, S=S, L=L, n_inner=n_inner,
                             seq_per_inner=seq_per_inner)
    return pl.pallas_call(
        kern, grid=(ns // S, n_inner), in_specs=in_specs, out_specs=out_specs,
        out_shape=out_shape, scratch_shapes=scratch, name="layer_sample",
        compiler_params=pltpu.CompilerParams(
            dimension_semantics=("arbitrary", "arbitrary"), vmem_limit_bytes=VMEM_LIMIT_BYTES),
    )(x, mod, *states, *weights)


def _block_sum_matrix(ql):
    r = jnp.arange(CHUNK)[:, None]
    c = jnp.arange(CHUNK)[None, :]
    same = (r // ql) == (c // ql)
    return jnp.concatenate([same & (c <= r), same], axis=0).astype(BF16)


def _expand_matrix():
    k = jnp.arange(LANES)[:, None]
    c = jnp.arange(D_INNER)[None, :] // HEAD_DIM
    return ((k % N_HEADS == c) & (k < 3 * N_HEADS)).astype(BF16)


def kernel(x_prompt, x_sample, state_pool, state_conv, state_ssm, c_prompt, c_sample, w_ada, b_ada, norm_g, w_in, conv_w, conv_b, dt_bias, a_log, d_skip, ssm_norm_g, pool_w, pool_scale, w_pool_out, w_ssm_out, w_o, final_g):
    depth = w_ada.shape[0]
    assert depth == 1, "kernel is written for a single layer"
    nb, seq, _ = x_prompt.shape
    ns, dec_seq, _ = x_sample.shape
    l = 0

    def head_lanes(v):
        return jnp.pad(jnp.tile(v, (1, 3)), ((0, 0), (0, LANES - 3 * N_HEADS)))

    c_all = jnp.concatenate([c_prompt, c_sample], axis=0)
    mod = pl.pallas_call(
        _adaln_kernel, out_shape=jax.ShapeDtypeStruct((nb + ns, 3 * D_MODEL), F32), name="adaln",
        compiler_params=pltpu.CompilerParams(vmem_limit_bytes=VMEM_LIMIT_BYTES),
    )(c_all, w_ada[l].astype(BF16), b_ada[l][None, :])
    mod = mod[:, None, :]

    e3 = _expand_matrix()
    w_main, w_tail = _repack_w_in(jnp.transpose(w_in[l]))

    def weights(ql):
        return [
            norm_g[l][None, :],
            w_main, w_tail,
            conv_w[l], conv_b[l][None, :],
            head_lanes(dt_bias[l][None, :]),
            head_lanes(a_log[l][None, :]),
            jnp.repeat(d_skip[l], HEAD_DIM)[None, :],
            ssm_norm_g[l][None, :],
            pool_w[l].astype(BF16), pool_scale[l][None, :],
            w_pool_out[l].astype(BF16), w_ssm_out[l].astype(BF16), w_o[l].astype(BF16),
            final_g[None, :],
            _block_sum_matrix(ql), e3,
        ]

    tile_rows = 2 * CHUNK
    y_p, pool_p, conv_p, ssm_p = _prompt_call(x_prompt, mod[:nb], weights(CHUNK), L=tile_rows)

    swap = lambda v: jnp.transpose(v, (1, 0, 2))
    states = (swap(state_pool[l]), swap(state_conv[l]),
              state_ssm[l].reshape(ns, D_INNER, D_STATE))
    y_s, pool_s, conv_s, ssm_s = _sample_call(
        x_sample, mod[nb:], states, weights(dec_seq), S=CHUNK // dec_seq, seq_per_inner=2)

    shp = (N_HEADS, HEAD_DIM, D_STATE)
    return (y_p, y_s, swap(pool_p)[None], swap(conv_p)[None], ssm_p.reshape((1, nb) + shp),
            swap(pool_s)[None], swap(conv_s)[None], ssm_s.reshape((1, ns) + shp))
```

```python
import functools
import types

import jax
import jax.numpy as jnp
from jax import lax
from jax.experimental import pallas as pl
from jax.experimental.pallas import tpu as pltpu

F32 = jnp.float32
BF16 = jnp.bfloat16

EPS = 1e-6
D_MODEL = 1024
POOL_WINDOWS = (2, 4, 8, 16)
POOL_GROUP = 128
D_POOL = POOL_GROUP * len(POOL_WINDOWS)
POOL_BUF = max(POOL_WINDOWS) - 1
D_INNER = 2048
HEAD_DIM = 64
N_HEADS = D_INNER // HEAD_DIM
N_BC_GROUPS = 4
HEADS_PER_GROUP = N_HEADS // N_BC_GROUPS
GROUP_COLS = HEADS_PER_GROUP * HEAD_DIM
D_STATE = 128
CONV_W = 4
D_XBC = D_INNER + 2 * N_BC_GROUPS * D_STATE
PAST_LEN = 16384

LANES = 128
CHUNK = 128
POOL_HDR = 16
CONV_HDR = 8
DOT_COLS = 512
ROW_BLOCK = 16
CONV_ROW_BLOCK = 32
VMEM_LIMIT_BYTES = 58 * 1024 * 1024

COL_UP = 0
COL_ZP = D_POOL
COL_ZS = 2 * D_POOL
COL_XBC = COL_ZS + D_INNER
COL_DT = COL_XBC + D_XBC
COL_G = COL_DT + LANES
N_WEIGHTS = 17
LOG2E = 1.4426950408889634


def _dot(a, b):
    return jnp.dot(a, b, preferred_element_type=F32)


def _dot_nt(a, b):
    return lax.dot_general(a, b, (((1,), (1,)), ((), ())), preferred_element_type=F32)


def _pack3(v, lane):
    hi = v.astype(BF16).astype(F32)
    r = v - hi
    mid = r.astype(BF16).astype(F32)
    lo = r - mid
    packed = jnp.where(lane < N_HEADS, hi,
                       jnp.where(lane < 2 * N_HEADS, mid,
                                 jnp.where(lane < 3 * N_HEADS, lo, 0.0)))
    return packed.astype(BF16)


def _silu(v):
    return v / (1.0 + jnp.exp2(v * -LOG2E))


def _sigmoid(v):
    return 1.0 / (1.0 + jnp.exp2(v * -LOG2E))


def _softplus(v):
    return jnp.maximum(v, 0.0) + jnp.log1p(jnp.exp(-jnp.abs(v)))


def _repack_main_kernel(x_ref, o_ref):
    o_ref[...] = x_ref[...].T.astype(o_ref.dtype)


def _repack_tail_kernel(dt_ref, g_ref, o_ref, *, g_row0):
    dt = dt_ref[...]
    heads = jnp.concatenate([dt, dt, dt, jnp.zeros((LANES - 3 * N_HEADS, D_MODEL), F32)], axis=0)
    o_ref[:, 0:LANES] = heads.T.astype(o_ref.dtype)
    for j in range(0, 2 * D_MODEL, DOT_COLS):
        o_ref[:, LANES + j:LANES + j + DOT_COLS] = (
            g_ref[g_row0 + j:g_row0 + j + DOT_COLS, :].T.astype(o_ref.dtype))


def _repack_w_in(w_in_t):
    n_in, d = w_in_t.shape
    assert d == D_MODEL and n_in == COL_DT + N_HEADS + 2 * D_MODEL
    w_main = pl.pallas_call(
        _repack_main_kernel, grid=(COL_DT // DOT_COLS,),
        in_specs=[pl.BlockSpec((DOT_COLS, D_MODEL), lambda j: (j, 0))],
        out_specs=pl.BlockSpec((D_MODEL, DOT_COLS), lambda j: (0, j)),
        out_shape=jax.ShapeDtypeStruct((D_MODEL, COL_DT), BF16), name="repack_w_main",
    )(w_in_t)
    g_block = n_in // 4
    g_row = COL_DT + N_HEADS
    g_row0 = g_row % g_block
    assert n_in % g_block == 0 and g_block % 8 == 0 and g_row0 % 8 == 0
    assert g_row0 + 2 * D_MODEL <= g_block and COL_DT % N_HEADS == 0
    w_tail = pl.pallas_call(
        functools.partial(_repack_tail_kernel, g_row0=g_row0), grid=(1,),
        in_specs=[pl.BlockSpec((N_HEADS, D_MODEL), lambda i: (COL_DT // N_HEADS, 0)),
                  pl.BlockSpec((g_block, D_MODEL), lambda i: (g_row // g_block, 0))],
        out_specs=pl.BlockSpec((D_MODEL, LANES + 2 * D_MODEL), lambda i: (0, 0)),
        out_shape=jax.ShapeDtypeStruct((D_MODEL, LANES + 2 * D_MODEL), BF16), name="repack_w_tail",
        compiler_params=pltpu.CompilerParams(vmem_limit_bytes=VMEM_LIMIT_BYTES),
    )(w_in_t, w_in_t)
    return w_main, w_tail


def _adaln_kernel(c_ref, w_ref, b_ref, o_ref):
    c = c_ref[...]
    o_ref[...] = _dot(_silu(c).astype(BF16), w_ref[...]) + b_ref[...]


def _weight_refs(it):
    names = ("normg", "wmain", "wtail", "convw", "convb", "dtb", "alog", "dskip", "ssmg", "poolw",
             "pscale", "wpo", "wso", "wo", "fg", "ta", "e3")
    assert len(names) == N_WEIGHTS
    return types.SimpleNamespace(**{n: next(it) for n in names})


def _row_blocks(S, L, nrows):
    if L >= nrows:
        return [(slice(s, s + 1), slice(l0, l0 + nrows), slice(s * L + l0, s * L + l0 + nrows))
                for s in range(S) for l0 in range(0, L, nrows)]
    sb = nrows // L
    return [(slice(s0, s0 + sb), slice(0, L), slice(s0 * L, (s0 + sb) * L))
            for s0 in range(0, S, sb)]


def _norm_modulate(w, x_ref, mod_of, h_ref, S, L):
    for ss, ls, rs in _row_blocks(S, L, ROW_BLOCK):
        x3 = x_ref[ss, ls, :]
        mod3 = mod_of(ss)
        ms = jnp.mean(x3 * x3, axis=-1, keepdims=True)
        hn = x3 * lax.rsqrt(ms + EPS) * w.normg[...]
        hm = hn * (1.0 + mod3[:, :, D_MODEL:2 * D_MODEL]) + mod3[:, :, 0:D_MODEL]
        h_ref[rs, :] = hm.reshape(rs.stop - rs.start, D_MODEL).astype(BF16)


def _roll_rows(x, d, drop_first=False):
    n, rows, lanes = x.shape
    tiles = [x[:, j:j + 8, :] for j in range(0, rows, 8)]
    first = 1 if drop_first else 0
    if d == 8:
        out = [tiles[j - 1] for j in range(first, len(tiles))]
    else:
        rolled = [pltpu.roll(tl, d, 1) for tl in tiles]
        sub = lax.broadcasted_iota(jnp.int32, (n, 8, lanes), 1)
        out = [jnp.where(sub < d, rolled[j - 1], rolled[j]) for j in range(first, len(tiles))]
    return jnp.concatenate(out, axis=1)


def _proj_chunks(col0, ncols):
    return [(col0 + j, min(DOT_COLS, ncols - j), j) for j in range(0, ncols, DOT_COLS)]


def _pool_windows(up3, t, pos0, S, L):
    for ss, ls, rs in _row_blocks(S, L, ROW_BLOCK):
        nseq, ntok = ss.stop - ss.start, ls.stop - ls.start
        pos = pos0 + ls.start + lax.broadcasted_iota(jnp.int32, (nseq, ntok, POOL_GROUP), 1)
        for gi, win in enumerate(POOL_WINDOWS):
            sl = slice(gi * POOL_GROUP, (gi + 1) * POOL_GROUP)
            s = up3[ss, ls.start:POOL_HDR + ls.stop, sl]
            u = s[:, POOL_HDR:, :]
            shift = 1
            while shift < win:
                s = s + _roll_rows(s, shift)
                shift *= 2
            cnt = jnp.minimum(pos + 1, win).astype(F32)
            t.pm[rs, sl] = (s[:, POOL_HDR:, :] / cnt - u).reshape(
                nseq * ntok, POOL_GROUP).astype(BF16)


def _pool_mix(w, zp, t):
    for gi in range(len(POOL_WINDOWS)):
        sl = slice(gi * POOL_GROUP, (gi + 1) * POOL_GROUP)
        yg = _dot(t.pm[:, sl], w.poolw[gi]) * w.pscale[:, sl]
        t.pm[:, sl] = (yg * _silu(zp[:, sl])).astype(BF16)
    for j in range(0, D_MODEL, DOT_COLS):
        t.p[:, j:j + DOT_COLS] = _dot(t.pm[...], w.wpo[:, j:j + DOT_COLS])


def _conv_chunk(w, xbc3, t, j, S, L):
    sl = slice(j, j + DOT_COLS)
    tile_of = lambda row: jnp.broadcast_to(row[None], (1, 8, DOT_COLS))
    taps_w = [tile_of(w.convw[k:k + 1, sl]) for k in range(CONV_W)]
    bias = tile_of(w.convb[:, sl])
    for ss, ls, rs in _row_blocks(S, L, CONV_ROW_BLOCK):
        ext = xbc3[ss, ls.start:CONV_HDR + ls.stop, sl]
        taps = [ext[:, CONV_HDR:, :] if k == CONV_W - 1
                else _roll_rows(ext, CONV_W - 1 - k, drop_first=True) for k in range(CONV_W)]
        out_tiles = []
        for r in range(0, ls.stop - ls.start, 8):
            acc = bias
            for k in range(CONV_W):
                acc = acc + taps[k][:, r:r + 8, :] * taps_w[k]
            out_tiles.append(_silu(acc))
        v = jnp.concatenate(out_tiles, axis=1).reshape(rs.stop - rs.start, DOT_COLS)
        if j < D_INNER:
            t.xs[rs, sl] = v
        elif j < D_INNER + N_BC_GROUPS * D_STATE:
            t.b[rs, :] = v
        else:
            t.c[rs, :] = v


def _ssd_prep(w, t, dt_ref, c, prompt):
    rows = pl.ds(c * CHUNK, CHUNK)
    lane = lax.broadcasted_iota(jnp.int32, (CHUNK, LANES), 1)
    neg_a = jnp.where(lane[0:1, :] < 3 * N_HEADS, -jnp.exp(w.alog[...]), 0.0)
    dtc = dt_ref[rows, :]
    a = dtc * neg_a
    cs = _dot(w.ta[...], _pack3(a, lane))
    cs = cs + pltpu.roll(cs, 2 * N_HEADS, 1)
    cs = cs + pltpu.roll(cs, N_HEADS, 1)
    acs = cs[0:CHUNK, :]
    last = cs[CHUNK:2 * CHUNK, :]
    exp_acs = jnp.exp(acs)
    wgt = jnp.exp(last - acs) * dtc
    t.ea[rows, :] = _dot(_pack3(exp_acs, lane), w.e3[...])
    t.we[rows, :] = _dot(_pack3(wgt, lane), w.e3[...])
    if not prompt:
        t.aux[c] = jnp.exp(last).T
    acs2 = acs * LOG2E
    t.hd[c, 0] = acs2
    t.hd[c, 1] = (acs2 - jnp.log2(dtc)).T


def _ssd_intra(w, t, c, QL, prompt):
    row_start = c * CHUNK
    rows = pl.ds(row_start, CHUNK)
    lane = lax.broadcasted_iota(jnp.int32, (CHUNK, LANES), 1)
    row = lax.broadcasted_iota(jnp.int32, (CHUNK, LANES), 0)
    acs2, shifted_t = t.hd[c, 0], t.hd[c, 1]
    ql_bits = QL.bit_length() - 1
    mask = ((lax.shift_right_logical(row, ql_bits) == lax.shift_right_logical(lane, ql_bits))
            & (lane <= row))
    for g in range(N_BC_GROUPS):
        gs = slice(g * D_STATE, (g + 1) * D_STATE)
        gc = slice(g * GROUP_COLS, (g + 1) * GROUP_COLS)
        cg = t.c[rows, gs].astype(BF16)
        cb = _dot_nt(cg, t.b[rows, gs].astype(BF16))
        if prompt:
            ht = t.aux[:, gc]
            t.yoff[...] = _dot(cg, ht.astype(BF16))
            xw = (t.xs[rows, gc] * t.we[rows, gc]).astype(BF16)
            upd = _dot(t.b[rows, gs].T.astype(BF16), xw)
            t.aux[:, gc] = ht * t.ea[row_start + CHUNK - 1:row_start + CHUNK, gc] + upd
        for k in range(HEADS_PER_GROUP // 2):
            atts = []
            for h in (g * HEADS_PER_GROUP + 2 * k, g * HEADS_PER_GROUP + 2 * k + 1):
                seg2 = acs2[:, h:h + 1] - shifted_t[h:h + 1, :]
                decay_dt = jnp.exp2(jnp.where(mask, seg2, -jnp.inf))
                atts.append((cb * decay_dt).astype(BF16))
            ps = slice((g * HEADS_PER_GROUP + 2 * k) * HEAD_DIM,
                       (g * HEADS_PER_GROUP + 2 * k + 2) * HEAD_DIM)
            xp = t.xs[rows, ps]
            rhs = jnp.concatenate([jnp.where(lane < HEAD_DIM, xp, 0.0).astype(BF16),
                                   jnp.where(lane >= HEAD_DIM, xp, 0.0).astype(BF16)], axis=0)
            yp = _dot(jnp.concatenate(atts, axis=1), rhs) + w.dskip[:, ps] * xp
            if prompt:
                yp = yp + t.yoff[:, 2 * k * HEAD_DIM:(2 * k + 2) * HEAD_DIM] * t.ea[rows, ps]
            t.yssd[rows, ps] = yp


def _sample_state_pair(t, r, QL, hins, houts):
    n = 2 * QL
    c = r // CHUNK
    r0 = r % CHUNK
    rows = pl.ds(pl.multiple_of(r, n), n)
    lane = lax.broadcasted_iota(jnp.int32, (CHUNK, LANES), 1)
    rowi = lax.broadcasted_iota(jnp.int32, (n, GROUP_COLS), 0)
    total_t = t.aux[c]
    for q in range(2):
        col = jnp.sum(jnp.where(lane == r0 + q * QL, total_t, 0.0), axis=1, keepdims=True)
        rb = jnp.broadcast_to(col, (CHUNK, LANES))
        mine = (rowi >= q * QL) & (rowi < (q + 1) * QL)
        for g in range(N_BC_GROUPS):
            gs = slice(g * D_STATE, (g + 1) * D_STATE)
            cs = slice(g * GROUP_COLS, (g + 1) * GROUP_COLS)
            hg = hins[q][cs, :]
            yoff = _dot_nt(t.c[rows, gs].astype(BF16), hg.astype(BF16)) * t.ea[rows, cs]
            t.yssd[rows, cs] = t.yssd[rows, cs] + jnp.where(mine, yoff, 0.0)
            xw = jnp.where(mine, t.xs[rows, cs] * t.we[rows, cs], 0.0)
            upd = _dot(xw.T.astype(BF16), t.b[rows, gs].astype(BF16))
            for e in range(HEADS_PER_GROUP):
                h = g * HEADS_PER_GROUP + e
                hs = slice(h * HEAD_DIM, (h + 1) * HEAD_DIM)
                es = slice(e * HEAD_DIM, (e + 1) * HEAD_DIM)
                houts[q][hs, :] = hg[es, :] * rb[h:h + 1, :] + upd[es, :]


def _dense_out(w, t, x_ref, gate_of, y_ref, S, L):
    R = S * L
    col_chunks = [slice(j, j + DOT_COLS) for j in range(0, D_INNER, DOT_COLS)]
    for r in range(0, R, ROW_BLOCK):
        rs = slice(r, r + ROW_BLOCK)
        ys = [t.yssd[rs, sl] * _silu(t.zs[rs, sl]) for sl in col_chunks]
        ssq = sum(jnp.sum(y * y, axis=-1, keepdims=True) for y in ys)
        inv = lax.rsqrt(ssq * (1.0 / D_INNER) + EPS)
        for y, sl in zip(ys, col_chunks):
            t.yn[rs, sl] = (y * inv * w.ssmg[:, sl]).astype(BF16)
    for j in range(0, D_MODEL, DOT_COLS):
        t.s[:, j:j + DOT_COLS] = _dot(t.yn[...], w.wso[:, j:j + DOT_COLS])
    for r in range(0, R, ROW_BLOCK):
        rs = slice(r, r + ROW_BLOCK)
        for j in range(0, D_MODEL, DOT_COLS):
            sl = slice(j, j + DOT_COLS)
            sl2 = slice(D_MODEL + j, D_MODEL + j + DOT_COLS)
            t.yn[rs, sl] = (_sigmoid(t.g[rs, sl]) * t.p[rs, sl]
                            + _sigmoid(t.g[rs, sl2]) * t.s[rs, sl]).astype(BF16)
    for j in range(0, D_MODEL, DOT_COLS):
        t.s[:, j:j + DOT_COLS] = _dot(t.yn[:, 0:D_MODEL], w.wo[:, j:j + DOT_COLS])
    for ss, ls, rs in _row_blocks(S, L, ROW_BLOCK):
        o3 = t.s[rs, :].reshape(ss.stop - ss.start, ls.stop - ls.start, D_MODEL)
        xn = x_ref[ss, ls, :] + gate_of(ss) * o3
        ms = jnp.mean(xn * xn, axis=-1, keepdims=True)
        y_ref[ss, ls, :] = xn * lax.rsqrt(ms + EPS) * w.fg[...]


def _ssd_refs(it):
    names = ("xs", "b", "c", "yssd", "pm", "p", "ea", "we", "hd", "aux", "yn", "s", "zs", "g")
    return types.SimpleNamespace(**{n: next(it) for n in names})


def _input_projection(w, h_ref, stores):
    for col0, (ncols, store) in stores.items():
        for col, wd, j in _proj_chunks(col0, ncols):
            if col < COL_DT:
                wc = w.wmain[:, col:col + wd]
            else:
                wc = w.wtail[:, col - COL_DT:col - COL_DT + wd]
            store(j, wd, _dot(h_ref[...], wc))


def _prompt_kernel(*refs, L):
    S = 1
    it = iter(refs)
    x_ref, mod_ref = next(it), next(it)
    w = _weight_refs(it)
    y_ref, pool_out_ref, conv_out_ref, ssm_out_ref = [next(it) for _ in range(4)]
    h_scr, up_scr, zp_scr, xbc_scr, dt_scr = [next(it) for _ in range(5)]
    t = _ssd_refs(it)
    t.yoff = next(it)
    step = pl.program_id(1)

    @pl.when(step == 0)
    def _():
        up_scr[:, 0:POOL_HDR, :] = jnp.zeros((S, POOL_HDR, D_POOL), F32)
        xbc_scr[:, 0:CONV_HDR, :] = jnp.zeros((S, CONV_HDR, D_XBC), F32)
        t.aux[...] = jnp.zeros(t.aux.shape, F32)

    _norm_modulate(w, x_ref, lambda ss: mod_ref[...], h_scr, S, L)

    def store_up(j, wd, v):
        up_scr[:, POOL_HDR:POOL_HDR + L, j:j + wd] = v.reshape(S, L, wd)

    def store_zp(j, wd, v):
        zp_scr[:, j:j + wd] = v

    def store_zs(j, wd, v):
        t.zs[:, j:j + wd] = v

    def store_xbc(j, wd, v):
        xbc_scr[:, CONV_HDR:CONV_HDR + L, j:j + wd] = v.reshape(S, L, wd)

    def store_dt(j, wd, v):
        dt_scr[...] = _softplus(v + w.dtb[...])

    def store_g(j, wd, v):
        t.g[:, j:j + wd] = v

    _input_projection(w, h_scr, {
        COL_UP: (D_POOL, store_up), COL_ZP: (D_POOL, store_zp), COL_DT: (LANES, store_dt),
        COL_XBC: (D_XBC, store_xbc), COL_ZS: (D_INNER, store_zs), COL_G: (2 * D_MODEL, store_g)})

    _pool_windows(up_scr, t, step * L, S, L)
    _pool_mix(w, zp_scr, t)
    up_scr[:, 0:POOL_HDR, :] = up_scr[:, L:L + POOL_HDR, :]

    for j in range(0, D_XBC, DOT_COLS):
        _conv_chunk(w, xbc_scr, t, j, S, L)
    xbc_scr[:, 0:CONV_HDR, :] = xbc_scr[:, L:L + CONV_HDR, :]

    for c in range(L // CHUNK):
        _ssd_prep(w, t, dt_scr, c, True)
    for c in range(L // CHUNK):
        _ssd_intra(w, t, c, CHUNK, True)

    _dense_out(w, t, x_ref, lambda ss: mod_ref[...][:, :, 2 * D_MODEL:3 * D_MODEL], y_ref, S, L)

    @pl.when(step == pl.num_programs(1) - 1)
    def _():
        for n in range(D_INNER // LANES):
            cols = slice(n * LANES, (n + 1) * LANES)
            ssm_out_ref[0, cols, :] = t.aux[:, cols].T
        batch = pl.ds(pl.program_id(0), 1)
        for r in range(POOL_BUF):
            pool_out_ref[r, batch, :] = up_scr[0, 1 + r:2 + r, :]
        for r in range(CONV_W - 1):
            r0 = CONV_HDR - (CONV_W - 1) + r
            conv_out_ref[r, batch, :] = xbc_scr[0, r0:r0 + 1, :]


def _sample_kernel(*refs, S, L, n_inner, seq_per_inner):
    R = S * L
    it = iter(refs)
    x_ref, mod_ref, pool_in_ref, conv_in_ref, ssm_in_ref = [next(it) for _ in range(5)]
    w = _weight_refs(it)
    y_ref, pool_out_ref, conv_out_ref, ssm_out_ref = [next(it) for _ in range(4)]
    h_scr, up_scr, zp_scr, xbc_scr, dt_scr = [next(it) for _ in range(5)]
    t = _ssd_refs(it)
    step = pl.program_id(1)

    @pl.when(step == 0)
    def _():
        _norm_modulate(w, x_ref, lambda ss: mod_ref[ss], h_scr, S, L)

        def store_up(j, wd, v):
            up_scr[:, POOL_HDR:POOL_HDR + L, j:j + wd] = v.reshape(S, L, wd)

        def store_zp(j, wd, v):
            zp_scr[:, j:j + wd] = v

        def store_zs(j, wd, v):
            t.zs[:, j:j + wd] = v

        def store_xbc(j, wd, v):
            xbc_scr[:, CONV_HDR:CONV_HDR + L, j:j + wd] = v.reshape(S, L, wd)

        def store_dt(j, wd, v):
            dt_scr[...] = _softplus(v + w.dtb[...])

        def store_g(j, wd, v):
            t.g[:, j:j + wd] = v

        _input_projection(w, h_scr, {
            COL_UP: (D_POOL, store_up), COL_ZP: (D_POOL, store_zp), COL_DT: (LANES, store_dt),
            COL_XBC: (D_XBC, store_xbc), COL_ZS: (D_INNER, store_zs),
            COL_G: (2 * D_MODEL, store_g)})

        for r in range(POOL_BUF):
            up_scr[:, 1 + r, :] = pool_in_ref[r]
        _pool_windows(up_scr, t, PAST_LEN, S, L)
        _pool_mix(w, zp_scr, t)
        for r in range(POOL_BUF):
            pool_out_ref[r] = up_scr[:, L + 1 + r, :]

        for r in range(CONV_W - 1):
            xbc_scr[:, CONV_HDR - (CONV_W - 1) + r, :] = conv_in_ref[r]
        for j in range(0, D_XBC, DOT_COLS):
            _conv_chunk(w, xbc_scr, t, j, S, L)
        for r in range(CONV_W - 1):
            conv_out_ref[r] = xbc_scr[:, L + CONV_HDR - (CONV_W - 1) + r, :]
        for c in range(R // CHUNK):
            _ssd_prep(w, t, dt_scr, c, False)
        for c in range(R // CHUNK):
            _ssd_intra(w, t, c, L, False)

    for q in range(0, seq_per_inner, 2):
        _sample_state_pair(t, (step * seq_per_inner + q) * L, L,
                           (ssm_in_ref.at[q], ssm_in_ref.at[q + 1]),
                           (ssm_out_ref.at[q], ssm_out_ref.at[q + 1]))

    @pl.when(step == n_inner - 1)
    def _():
        _dense_out(w, t, x_ref, lambda ss: mod_ref[ss][:, :, 2 * D_MODEL:3 * D_MODEL],
                   y_ref, S, L)


def _resident(shape):
    zeros = (0,) * len(shape)
    return pl.BlockSpec(shape, lambda *_: zeros, pipeline_mode=pl.Buffered(1))


def _ssd_scratch(R, aux_shape):
    return [
        pltpu.VMEM((R, D_INNER), F32),
        pltpu.VMEM((R, N_BC_GROUPS * D_STATE), F32),
        pltpu.VMEM((R, N_BC_GROUPS * D_STATE), F32),
        pltpu.VMEM((R, D_INNER), F32),
        pltpu.VMEM((R, D_POOL), BF16),
        pltpu.VMEM((R, D_MODEL), F32),
        pltpu.VMEM((R, D_INNER), F32),
        pltpu.VMEM((R, D_INNER), F32),
        pltpu.VMEM((R // CHUNK, 2, CHUNK, LANES), F32),
        pltpu.VMEM(aux_shape, F32),
        pltpu.VMEM((R, D_INNER), BF16),
        pltpu.VMEM((R, D_MODEL), F32),
        pltpu.VMEM((R, D_INNER), F32),
        pltpu.VMEM((R, 2 * D_MODEL), F32),
    ]


def _stage_scratch(S, L):
    return [
        pltpu.VMEM((S * L, D_MODEL), BF16),
        pltpu.VMEM((S, POOL_HDR + L, D_POOL), F32),
        pltpu.VMEM((S * L, D_POOL), F32),
        pltpu.VMEM((S, CONV_HDR + L, D_XBC), F32),
        pltpu.VMEM((S * L, LANES), F32),
    ]


def _prompt_call(x, mod, weights, *, L):
    nb, seq, _ = x.shape
    x_spec = pl.BlockSpec((1, L, D_MODEL), lambda b, t: (b, t, 0))
    seq_map = lambda b, t: (b, 0, 0)
    in_specs = ([x_spec, pl.BlockSpec((1, 1, 3 * D_MODEL), seq_map)]
                + [_resident(v.shape) for v in weights])
    whole = lambda b, t: (0, 0, 0)
    out_shape = (jax.ShapeDtypeStruct(x.shape, F32),
                 jax.ShapeDtypeStruct((POOL_BUF, nb, D_POOL), F32),
                 jax.ShapeDtypeStruct((CONV_W - 1, nb, D_XBC), F32),
                 jax.ShapeDtypeStruct((nb, D_INNER, D_STATE), F32))
    out_specs = (x_spec,
                 pl.BlockSpec((POOL_BUF, nb, D_POOL), whole),
                 pl.BlockSpec((CONV_W - 1, nb, D_XBC), whole),
                 pl.BlockSpec((1, D_INNER, D_STATE), seq_map))
    scratch = (_stage_scratch(1, L) + _ssd_scratch(L, (D_STATE, D_INNER))
               + [pltpu.VMEM((CHUNK, GROUP_COLS), F32)])
    kern = functools.partial(_prompt_kernel, L=L)
    return pl.pallas_call(
        kern, grid=(nb, seq // L), in_specs=in_specs, out_specs=out_specs, out_shape=out_shape,
        scratch_shapes=scratch, name="layer_prompt",
        compiler_params=pltpu.CompilerParams(
            dimension_semantics=("arbitrary", "arbitrary"), vmem_limit_bytes=VMEM_LIMIT_BYTES),
    )(x, mod, *weights)


def _sample_call(x, mod, states, weights, *, S, seq_per_inner):
    ns, L, _ = x.shape
    R = S * L
    n_inner = S // seq_per_inner
    seq_map = lambda o, i: (o, 0, 0)
    st_map = lambda o, i: (o * n_inner + i, 0, 0)
    time_major = lambda o, i: (0, o, 0)
    x_spec = pl.BlockSpec((S, L, D_MODEL), seq_map)
    pool_spec = pl.BlockSpec((POOL_BUF, S, D_POOL), time_major)
    conv_spec = pl.BlockSpec((CONV_W - 1, S, D_XBC), time_major)
    st_spec = pl.BlockSpec((seq_per_inner, D_INNER, D_STATE), st_map)
    in_specs = ([x_spec, pl.BlockSpec((S, 1, 3 * D_MODEL), seq_map), pool_spec, conv_spec, st_spec]
                + [_resident(v.shape) for v in weights])
    out_shape = (jax.ShapeDtypeStruct(x.shape, F32),
                 jax.ShapeDtypeStruct((POOL_BUF, ns, D_POOL), F32),
                 jax.ShapeDtypeStruct((CONV_W - 1, ns, D_XBC), F32),
                 jax.ShapeDtypeStruct((ns, D_INNER, D_STATE), F32))
    out_specs = (x_spec, pool_spec, conv_spec, st_spec)
    scratch = _stage_scratch(S, L) + _ssd_scratch(R, (R // CHUNK, LANES, CHUNK))
    kern = functools.partial(_sample_kernel, S=S, L=L, n_inner=n_inner,
                             seq_per_inner=seq_per_inner)
    return pl.pallas_call(
        kern, grid=(ns // S, n_inner), in_specs=in_specs, out_specs=out_specs,
        out_shape=out_shape, scratch_shapes=scratch, name="layer_sample",
        compiler_params=pltpu.CompilerParams(
            dimension_semantics=("arbitrary", "arbitrary"), vmem_limit_bytes=VMEM_LIMIT_BYTES),
    )(x, mod, *states, *weights)


def _block_sum_matrix(ql):
    r = jnp.arange(CHUNK)[:, None]
    c = jnp.arange(CHUNK)[None, :]
    same = (r // ql) == (c // ql)
    return jnp.concatenate([same & (c <= r), same], axis=0).astype(BF16)


def _expand_matrix():
    k = jnp.arange(LANES)[:, None]
    c = jnp.arange(D_INNER)[None, :] // HEAD_DIM
    return ((k % N_HEADS == c) & (k < 3 * N_HEADS)).astype(BF16)


def kernel(x_prompt, x_sample, state_pool, state_conv, state_ssm, c_prompt, c_sample, w_ada, b_ada, norm_g, w_in, conv_w, conv_b, dt_bias, a_log, d_skip, ssm_norm_g, pool_w, pool_scale, w_pool_out, w_ssm_out, w_o, final_g):
    depth = w_ada.shape[0]
    assert depth == 1, "kernel is written for a single layer"
    nb, seq, _ = x_prompt.shape
    ns, dec_seq, _ = x_sample.shape
    l = 0

    def head_lanes(v):
        return jnp.pad(jnp.tile(v, (1, 3)), ((0, 0), (0, LANES - 3 * N_HEADS)))

    c_all = jnp.concatenate([c_prompt, c_sample], axis=0)
    mod = pl.pallas_call(
        _adaln_kernel, out_shape=jax.ShapeDtypeStruct((nb + ns, 3 * D_MODEL), F32), name="adaln",
        compiler_params=pltpu.CompilerParams(vmem_limit_bytes=VMEM_LIMIT_BYTES),
    )(c_all, w_ada[l].astype(BF16), b_ada[l][None, :])
    mod = mod[:, None, :]

    e3 = _expand_matrix()
    w_main, w_tail = _repack_w_in(jnp.transpose(w_in[l]))

    def weights(ql):
        return [
            norm_g[l][None, :],
            w_main, w_tail,
            conv_w[l], conv_b[l][None, :],
            head_lanes(dt_bias[l][None, :]),
            head_lanes(a_log[l][None, :]),
            jnp.repeat(d_skip[l], HEAD_DIM)[None, :],
            ssm_norm_g[l][None, :],
            pool_w[l].astype(BF16), pool_scale[l][None, :],
            w_pool_out[l].astype(BF16), w_ssm_out[l].astype(BF16), w_o[l].astype(BF16),
            final_g[None, :],
            _block_sum_matrix(ql), e3,
        ]

    tile_rows = 2 * CHUNK
    y_p, pool_p, conv_p, ssm_p = _prompt_call(x_prompt, mod[:nb], weights(CHUNK), L=tile_rows)

    swap = lambda v: jnp.transpose(v, (1, 0, 2))
    states = (swap(state_pool[l]), swap(state_conv[l]),
              state_ssm[l].reshape(ns, D_INNER, D_STATE))
    y_s, pool_s, conv_s, ssm_s = _sample_call(
        x_sample, mod[nb:], states, weights(dec_seq), S=CHUNK // dec_seq, seq_per_inner=2)

    shp = (N_HEADS, HEAD_DIM, D_STATE)
    return (y_p, y_s, swap(pool_p)[None], swap(conv_p)[None], ssm_p.reshape((1, nb) + shp),
            swap(pool_s)[None], swap(conv_s)[None], ssm_s.reshape((1, ns) + shp))
```

```python
import functools
import types

import jax
import jax.numpy as jnp
from jax import lax
from jax.experimental import pallas as pl
from jax.experimental.pallas import tpu as pltpu

F32 = jnp.float32
BF16 = jnp.bfloat16

EPS = 1e-6
D_MODEL = 1024
POOL_WINDOWS = (2, 4, 8, 16)
POOL_GROUP = 128
D_POOL = POOL_GROUP * len(POOL_WINDOWS)
POOL_BUF = max(POOL_WINDOWS) - 1
D_INNER = 2048
HEAD_DIM = 64
N_HEADS = D_INNER // HEAD_DIM
N_BC_GROUPS = 4
HEADS_PER_GROUP = N_HEADS // N_BC_GROUPS
GROUP_COLS = HEADS_PER_GROUP * HEAD_DIM
D_STATE = 128
CONV_W = 4
D_XBC = D_INNER + 2 * N_BC_GROUPS * D_STATE
PAST_LEN = 16384

LANES = 128
CHUNK = 128
POOL_HDR = 16
CONV_HDR = 8
DOT_COLS = 512
ROW_BLOCK = 16
CONV_ROW_BLOCK = 32
VMEM_LIMIT_BYTES = 58 * 1024 * 1024

COL_UP = 0
COL_ZP = D_POOL
COL_ZS = 2 * D_POOL
COL_XBC = COL_ZS + D_INNER
COL_DT = COL_XBC + D_XBC
COL_G = COL_DT + LANES
N_WEIGHTS = 17
LOG2E = 1.4426950408889634


def _dot(a, b):
    return jnp.dot(a, b, preferred_element_type=F32)


def _dot_nt(a, b):
    return lax.dot_general(a, b, (((1,), (1,)), ((), ())), preferred_element_type=F32)


def _pack3(v, lane):
    hi = v.astype(BF16).astype(F32)
    r = v - hi
    mid = r.astype(BF16).astype(F32)
    lo = r - mid
    packed = jnp.where(lane < N_HEADS, hi,
                       jnp.where(lane < 2 * N_HEADS, mid,
                                 jnp.where(lane < 3 * N_HEADS, lo, 0.0)))
    return packed.astype(BF16)


def _silu(v):
    return v / (1.0 + jnp.exp2(v * -LOG2E))


def _sigmoid(v):
    return 1.0 / (1.0 + jnp.exp2(v * -LOG2E))


def _softplus(v):
    return jnp.maximum(v, 0.0) + jnp.log1p(jnp.exp(-jnp.abs(v)))


def _repack_main_kernel(x_ref, o_ref):
    o_ref[...] = x_ref[...].T.astype(o_ref.dtype)


def _repack_tail_kernel(dt_ref, g_ref, o_ref, *, g_row0):
    dt = dt_ref[...]
    heads = jnp.concatenate([dt, dt, dt, jnp.zeros((LANES - 3 * N_HEADS, D_MODEL), F32)], axis=0)
    o_ref[:, 0:LANES] = heads.T.astype(o_ref.dtype)
    for j in range(0, 2 * D_MODEL, DOT_COLS):
        o_ref[:, LANES + j:LANES + j + DOT_COLS] = (
            g_ref[g_row0 + j:g_row0 + j + DOT_COLS, :].T.astype(o_ref.dtype))


def _repack_w_in(w_in_t):
    n_in, d = w_in_t.shape
    assert d == D_MODEL and n_in == COL_DT + N_HEADS + 2 * D_MODEL
    w_main = pl.pallas_call(
        _repack_main_kernel, grid=(COL_DT // DOT_COLS,),
        in_specs=[pl.BlockSpec((DOT_COLS, D_MODEL), lambda j: (j, 0))],
        out_specs=pl.BlockSpec((D_MODEL, DOT_COLS), lambda j: (0, j)),
        out_shape=jax.ShapeDtypeStruct((D_MODEL, COL_DT), BF16), name="repack_w_main",
    )(w_in_t)
    g_block = n_in // 4
    g_row = COL_DT + N_HEADS
    g_row0 = g_row % g_block
    assert n_in % g_block == 0 and g_block % 8 == 0 and g_row0 % 8 == 0
    assert g_row0 + 2 * D_MODEL <= g_block and COL_DT % N_HEADS == 0
    w_tail = pl.pallas_call(
        functools.partial(_repack_tail_kernel, g_row0=g_row0), grid=(1,),
        in_specs=[pl.BlockSpec((N_HEADS, D_MODEL), lambda i: (COL_DT // N_HEADS, 0)),
                  pl.BlockSpec((g_block, D_MODEL), lambda i: (g_row // g_block, 0))],
        out_specs=pl.BlockSpec((D_MODEL, LANES + 2 * D_MODEL), lambda i: (0, 0)),
        out_shape=jax.ShapeDtypeStruct((D_MODEL, LANES + 2 * D_MODEL), BF16), name="repack_w_tail",
        compiler_params=pltpu.CompilerParams(vmem_limit_bytes=VMEM_LIMIT_BYTES),
    )(w_in_t, w_in_t)
    return w_main, w_tail


def _adaln_kernel(c_ref, w_ref, b_ref, o_ref):
    c = c_ref[...]
    o_ref[...] = _dot(_silu(c).astype(BF16), w_ref[...]) + b_ref[...]


def _weight_refs(it):
    names = ("normg", "wmain", "wtail", "convw", "convb", "dtb", "alog", "dskip", "ssmg", "poolw",
             "pscale", "wpo", "wso", "wo", "fg", "ta", "e3")
    assert len(names) == N_WEIGHTS
    return types.SimpleNamespace(**{n: next(it) for n in names})


def _row_blocks(S, L, nrows):
    if L >= nrows:
        return [(slice(s, s + 1), slice(l0, l0 + nrows), slice(s * L + l0, s * L + l0 + nrows))
                for s in range(S) for l0 in range(0, L, nrows)]
    sb = nrows // L
    return [(slice(s0, s0 + sb), slice(0, L), slice(s0 * L, (s0 + sb) * L))
            for s0 in range(0, S, sb)]


def _norm_modulate(w, x_ref, mod_of, h_ref, S, L):
    for ss, ls, rs in _row_blocks(S, L, ROW_BLOCK):
        x3 = x_ref[ss, ls, :]
        mod3 = mod_of(ss)
        ms = jnp.mean(x3 * x3, axis=-1, keepdims=True)
        hn = x3 * lax.rsqrt(ms + EPS) * w.normg[...]
        hm = hn * (1.0 + mod3[:, :, D_MODEL:2 * D_MODEL]) + mod3[:, :, 0:D_MODEL]
        h_ref[rs, :] = hm.reshape(rs.stop - rs.start, D_MODEL).astype(BF16)


def _roll_rows(x, d, drop_first=False):
    n, rows, lanes = x.shape
    tiles = [x[:, j:j + 8, :] for j in range(0, rows, 8)]
    first = 1 if drop_first else 0
    if d == 8:
        out = [tiles[j - 1] for j in range(first, len(tiles))]
    else:
        rolled = [pltpu.roll(tl, d, 1) for tl in tiles]
        sub = lax.broadcasted_iota(jnp.int32, (n, 8, lanes), 1)
        out = [jnp.where(sub < d, rolled[j - 1], rolled[j]) for j in range(first, len(tiles))]
    return jnp.concatenate(out, axis=1)


def _proj_chunks(col0, ncols):
    return [(col0 + j, min(DOT_COLS, ncols - j), j) for j in range(0, ncols, DOT_COLS)]


def _pool_windows(up3, t, pos0, S, L):
    for ss, ls, rs in _row_blocks(S, L, ROW_BLOCK):
        nseq, ntok = ss.stop - ss.start, ls.stop - ls.start
        pos = pos0 + ls.start + lax.broadcasted_iota(jnp.int32, (nseq, ntok, POOL_GROUP), 1)
        for gi, win in enumerate(POOL_WINDOWS):
            sl = slice(gi * POOL_GROUP, (gi + 1) * POOL_GROUP)
            s = up3[ss, ls.start:POOL_HDR + ls.stop, sl]
            u = s[:, POOL_HDR:, :]
            shift = 1
            while shift < win:
                s = s + _roll_rows(s, shift)
                shift *= 2
            cnt = jnp.minimum(pos + 1, win).astype(F32)
            t.pm[rs, sl] = (s[:, POOL_HDR:, :] / cnt - u).reshape(
                nseq * ntok, POOL_GROUP).astype(BF16)


def _pool_mix(w, zp, t):
    for gi in range(len(POOL_WINDOWS)):
        sl = slice(gi * POOL_GROUP, (gi + 1) * POOL_GROUP)
        yg = _dot(t.pm[:, sl], w.poolw[gi]) * w.pscale[:, sl]
        t.pm[:, sl] = (yg * _silu(zp[:, sl])).astype(BF16)
    for j in range(0, D_MODEL, DOT_COLS):
        t.p[:, j:j + DOT_COLS] = _dot(t.pm[...], w.wpo[:, j:j + DOT_COLS])


def _conv_chunk(w, xbc3, t, j, S, L):
    sl = slice(j, j + DOT_COLS)
    tile_of = lambda row: jnp.broadcast_to(row[None], (1, 8, DOT_COLS))
    taps_w = [tile_of(w.convw[k:k + 1, sl]) for k in range(CONV_W)]
    bias = tile_of(w.convb[:, sl])
    for ss, ls, rs in _row_blocks(S, L, CONV_ROW_BLOCK):
        ext = xbc3[ss, ls.start:CONV_HDR + ls.stop, sl]
        taps = [ext[:, CONV_HDR:, :] if k == CONV_W - 1
                else _roll_rows(ext, CONV_W - 1 - k, drop_first=True) for k in range(CONV_W)]
        out_tiles = []
        for r in range(0, ls.stop - ls.start, 8):
            acc = bias
            for k in range(CONV_W):
                acc = acc + taps[k][:, r:r + 8, :] * taps_w[k]
            out_tiles.append(_silu(acc))
        v = jnp.concatenate(out_tiles, axis=1).reshape(rs.stop - rs.start, DOT_COLS)
        if j < D_INNER:
            t.xs[rs, sl] = v
        elif j < D_INNER + N_BC_GROUPS * D_STATE:
            t.b[rs, :] = v
        else:
            t.c[rs, :] = v


def _ssd_prep(w, t, dt_ref, c, prompt):
    rows = pl.ds(c * CHUNK, CHUNK)
    lane = lax.broadcasted_iota(jnp.int32, (CHUNK, LANES), 1)
    neg_a = jnp.where(lane[0:1, :] < 3 * N_HEADS, -jnp.exp(w.alog[...]), 0.0)
    dtc = dt_ref[rows, :]
    a = dtc * neg_a
    cs = _dot(w.ta[...], _pack3(a, lane))
    cs = cs + pltpu.roll(cs, 2 * N_HEADS, 1)
    cs = cs + pltpu.roll(cs, N_HEADS, 1)
    acs = cs[0:CHUNK, :]
    last = cs[CHUNK:2 * CHUNK, :]
    exp_acs = jnp.exp(acs)
    wgt = jnp.exp(last - acs) * dtc
    t.ea[rows, :] = _dot(_pack3(exp_acs, lane), w.e3[...])
    t.we[rows, :] = _dot(_pack3(wgt, lane), w.e3[...])
    if not prompt:
        t.aux[c] = jnp.exp(last).T
    acs2 = acs * LOG2E
    t.hd[c, 0] = acs2
    t.hd[c, 1] = (acs2 - jnp.log2(dtc)).T


def _ssd_intra(w, t, c, QL, prompt):
    row_start = c * CHUNK
    rows = pl.ds(row_start, CHUNK)
    lane = lax.broadcasted_iota(jnp.int32, (CHUNK, LANES), 1)
    row = lax.broadcasted_iota(jnp.int32, (CHUNK, LANES), 0)
    acs2, shifted_t = t.hd[c, 0], t.hd[c, 1]
    ql_bits = QL.bit_length() - 1
    mask = ((lax.shift_right_logical(row, ql_bits) == lax.shift_right_logical(lane, ql_bits))
            & (lane <= row))
    for g in range(N_BC_GROUPS):
        gs = slice(g * D_STATE, (g + 1) * D_STATE)
        gc = slice(g * GROUP_COLS, (g + 1) * GROUP_COLS)
        cg = t.c[rows, gs].astype(BF16)
        cb = _dot_nt(cg, t.b[rows, gs].astype(BF16))
        if prompt:
            ht = t.aux[:, gc]
            t.yoff[...] = _dot(cg, ht.astype(BF16))
            xw = (t.xs[rows, gc] * t.we[rows, gc]).astype(BF16)
            upd = _dot(t.b[rows, gs].T.astype(BF16), xw)
            t.aux[:, gc] = ht * t.ea[row_start + CHUNK - 1:row_start + CHUNK, gc] + upd
        for k in range(HEADS_PER_GROUP // 2):
            atts = []
            for h in (g * HEADS_PER_GROUP + 2 * k, g * HEADS_PER_GROUP + 2 * k + 1):
                seg2 = acs2[:, h:h + 1] - shifted_t[h:h + 1, :]
                decay_dt = jnp.exp2(jnp.where(mask, seg2, -jnp.inf))
                atts.append((cb * decay_dt).astype(BF16))
            ps = slice((g * HEADS_PER_GROUP + 2 * k) * HEAD_DIM,
                       (g * HEADS_PER_GROUP + 2 * k + 2) * HEAD_DIM)
            xp = t.xs[rows, ps]
            rhs = jnp.concatenate([jnp.where(lane < HEAD_DIM, xp, 0.0).astype(BF16),
                                   jnp.where(lane >= HEAD_DIM, xp, 0.0).astype(BF16)], axis=0)
            yp = _dot(jnp.concatenate(atts, axis=1), rhs) + w.dskip[:, ps] * xp
            if prompt:
                yp = yp + t.yoff[:, 2 * k * HEAD_DIM:(2 * k + 2) * HEAD_DIM] * t.ea[rows, ps]
            t.yssd[rows, ps] = yp


def _sample_state_pair(t, r, QL, hins, houts):
    n = 2 * QL
    c = r // CHUNK
    r0 = r % CHUNK
    rows = pl.ds(pl.multiple_of(r, n), n)
    lane = lax.broadcasted_iota(jnp.int32, (CHUNK, LANES), 1)
    rowi = lax.broadcasted_iota(jnp.int32, (n, GROUP_COLS), 0)
    total_t = t.aux[c]
    for q in range(2):
        col = jnp.sum(jnp.where(lane == r0 + q * QL, total_t, 0.0), axis=1, keepdims=True)
        rb = jnp.broadcast_to(col, (CHUNK, LANES))
        mine = (rowi >= q * QL) & (rowi < (q + 1) * QL)
        for g in range(N_BC_GROUPS):
            gs = slice(g * D_STATE, (g + 1) * D_STATE)
            cs = slice(g * GROUP_COLS, (g + 1) * GROUP_COLS)
            hg = hins[q][cs, :]
            yoff = _dot_nt(t.c[rows, gs].astype(BF16), hg.astype(BF16)) * t.ea[rows, cs]
            t.yssd[rows, cs] = t.yssd[rows, cs] + jnp.where(mine, yoff, 0.0)
            xw = jnp.where(mine, t.xs[rows, cs] * t.we[rows, cs], 0.0)
            upd = _dot(xw.T.astype(BF16), t.b[rows, gs].astype(BF16))
            for e in range(HEADS_PER_GROUP):
                h = g * HEADS_PER_GROUP + e
                hs = slice(h * HEAD_DIM, (h + 1) * HEAD_DIM)
                es = slice(e * HEAD_DIM, (e + 1) * HEAD_DIM)
                houts[q][hs, :] = hg[es, :] * rb[h:h + 1, :] + upd[es, :]


def _dense_out(w, t, x_ref, gate_of, y_ref, S, L):
    R = S * L
    col_chunks = [slice(j, j + DOT_COLS) for j in range(0, D_INNER, DOT_COLS)]
    for r in range(0, R, ROW_BLOCK):
        rs = slice(r, r + ROW_BLOCK)
        ys = [t.yssd[rs, sl] * _silu(t.zs[rs, sl]) for sl in col_chunks]
        ssq = sum(jnp.sum(y * y, axis=-1, keepdims=True) for y in ys)
        inv = lax.rsqrt(ssq * (1.0 / D_INNER) + EPS)
        for y, sl in zip(ys, col_chunks):
            t.yn[rs, sl] = (y * inv * w.ssmg[:, sl]).astype(BF16)
    for j in range(0, D_MODEL, DOT_COLS):
        t.s[:, j:j + DOT_COLS] = _dot(t.yn[...], w.wso[:, j:j + DOT_COLS])
    for r in range(0, R, ROW_BLOCK):
        rs = slice(r, r + ROW_BLOCK)
        for j in range(0, D_MODEL, DOT_COLS):
            sl = slice(j, j + DOT_COLS)
            sl2 = slice(D_MODEL + j, D_MODEL + j + DOT_COLS)
            t.yn[rs, sl] = (_sigmoid(t.g[rs, sl]) * t.p[rs, sl]
                            + _sigmoid(t.g[rs, sl2]) * t.s[rs, sl]).astype(BF16)
    for j in range(0, D_MODEL, DOT_COLS):
        t.s[:, j:j + DOT_COLS] = _dot(t.yn[:, 0:D_MODEL], w.wo[:, j:j + DOT_COLS])
    for ss, ls, rs in _row_blocks(S, L, ROW_BLOCK):
        o3 = t.s[rs, :].reshape(ss.stop - ss.start, ls.stop - ls.start, D_MODEL)
        xn = x_ref[ss, ls, :] + gate_of(ss) * o3
        ms = jnp.mean(xn * xn, axis=-1, keepdims=True)
        y_ref[ss, ls, :] = xn * lax.rsqrt(ms + EPS) * w.fg[...]


def _ssd_refs(it):
    names = ("xs", "b", "c", "yssd", "pm", "p", "ea", "we", "hd", "aux", "yn", "s", "zs", "g")
    return types.SimpleNamespace(**{n: next(it) for n in names})


def _input_projection(w, h_ref, stores):
    for col0, (ncols, store) in stores.items():
        for col, wd, j in _proj_chunks(col0, ncols):
            if col < COL_DT:
                wc = w.wmain[:, col:col + wd]
            else:
                wc = w.wtail[:, col - COL_DT:col - COL_DT + wd]
            store(j, wd, _dot(h_ref[...], wc))


def _prompt_kernel(*refs, L):
    S = 1
    it = iter(refs)
    x_ref, mod_ref = next(it), next(it)
    w = _weight_refs(it)
    y_ref, pool_out_ref, conv_out_ref, ssm_out_ref = [next(it) for _ in range(4)]
    h_scr, up_scr, zp_scr, xbc_scr, dt_scr = [next(it) for _ in range(5)]
    t = _ssd_refs(it)
    t.yoff = next(it)
    step = pl.program_id(1)

    @pl.when(step == 0)
    def _():
        up_scr[:, 0:POOL_HDR, :] = jnp.zeros((S, POOL_HDR, D_POOL), F32)
        xbc_scr[:, 0:CONV_HDR, :] = jnp.zeros((S, CONV_HDR, D_XBC), F32)
        t.aux[...] = jnp.zeros(t.aux.shape, F32)

    _norm_modulate(w, x_ref, lambda ss: mod_ref[...], h_scr, S, L)

    def store_up(j, wd, v):
        up_scr[:, POOL_HDR:POOL_HDR + L, j:j + wd] = v.reshape(S, L, wd)

    def store_zp(j, wd, v):
        zp_scr[:, j:j + wd] = v

    def store_zs(j, wd, v):
        t.zs[:, j:j + wd] = v

    def store_xbc(j, wd, v):
        xbc_scr[:, CONV_HDR:CONV_HDR + L, j:j + wd] = v.reshape(S, L, wd)

    def store_dt(j, wd, v):
        dt_scr[...] = _softplus(v + w.dtb[...])

    def store_g(j, wd, v):
        t.g[:, j:j + wd] = v

    _input_projection(w, h_scr, {
        COL_UP: (D_POOL, store_up), COL_ZP: (D_POOL, store_zp), COL_DT: (LANES, store_dt),
        COL_XBC: (D_XBC, store_xbc), COL_ZS: (D_INNER, store_zs), COL_G: (2 * D_MODEL, store_g)})

    _pool_windows(up_scr, t, step * L, S, L)
    _pool_mix(w, zp_scr, t)
    up_scr[:, 0:POOL_HDR, :] = up_scr[:, L:L + POOL_HDR, :]

    for j in range(0, D_XBC, DOT_COLS):
        _conv_chunk(w, xbc_scr, t, j, S, L)
    xbc_scr[:, 0:CONV_HDR, :] = xbc_scr[:, L:L + CONV_HDR, :]

    for c in range(L // CHUNK):
        _ssd_prep(w, t, dt_scr, c, True)
    for c in range(L // CHUNK):
        _ssd_intra(w, t, c, CHUNK, True)

    _dense_out(w, t, x_ref, lambda ss: mod_ref[...][:, :, 2 * D_MODEL:3 * D_MODEL], y_ref, S, L)

    @pl.when(step == pl.num_programs(1) - 1)
    def _():
        for n in range(D_INNER // LANES):
            cols = slice(n * LANES, (n + 1) * LANES)
            ssm_out_ref[0, cols, :] = t.aux[:, cols].T
        batch = pl.ds(pl.program_id(0), 1)
        for r in range(POOL_BUF):
            pool_out_ref[r, batch, :] = up_scr[0, 1 + r:2 + r, :]
        for r in range(CONV_W - 1):
            r0 = CONV_HDR - (CONV_W - 1) + r
            conv_out_ref[r, batch, :] = xbc_scr[0, r0:r0 + 1, :]


def _sample_kernel(*refs, S, L, seq_per_block):
    R = S * L
    it = iter(refs)
    x_ref, mod_ref, pool_in_ref, conv_in_ref, ssm_in_hbm = [next(it) for _ in range(5)]
    w = _weight_refs(it)
    y_ref, pool_out_ref, conv_out_ref, ssm_out_hbm = [next(it) for _ in range(4)]
    h_scr, up_scr, zp_scr, xbc_scr, dt_scr = [next(it) for _ in range(5)]
    t = _ssd_refs(it)
    block_ref = next(it)
    first_block = pl.program_id(0) * (S // seq_per_block)

    _norm_modulate(w, x_ref, lambda ss: mod_ref[ss], h_scr, S, L)

    def store_up(j, wd, v):
        up_scr[:, POOL_HDR:POOL_HDR + L, j:j + wd] = v.reshape(S, L, wd)

    def store_zp(j, wd, v):
        zp_scr[:, j:j + wd] = v

    def store_zs(j, wd, v):
        t.zs[:, j:j + wd] = v

    def store_xbc(j, wd, v):
        xbc_scr[:, CONV_HDR:CONV_HDR + L, j:j + wd] = v.reshape(S, L, wd)

    def store_dt(j, wd, v):
        dt_scr[...] = _softplus(v + w.dtb[...])

    def store_g(j, wd, v):
        t.g[:, j:j + wd] = v

    _input_projection(w, h_scr, {
        COL_UP: (D_POOL, store_up), COL_ZP: (D_POOL, store_zp), COL_DT: (LANES, store_dt),
        COL_XBC: (D_XBC, store_xbc), COL_ZS: (D_INNER, store_zs),
        COL_G: (2 * D_MODEL, store_g)})

    for r in range(POOL_BUF):
        up_scr[:, 1 + r, :] = pool_in_ref[r]
    _pool_windows(up_scr, t, PAST_LEN, S, L)
    _pool_mix(w, zp_scr, t)
    for r in range(POOL_BUF):
        pool_out_ref[r] = up_scr[:, L + 1 + r, :]

    for r in range(CONV_W - 1):
        xbc_scr[:, CONV_HDR - (CONV_W - 1) + r, :] = conv_in_ref[r]
    for j in range(0, D_XBC, DOT_COLS):
        _conv_chunk(w, xbc_scr, t, j, S, L)
    for r in range(CONV_W - 1):
        conv_out_ref[r] = xbc_scr[:, L + CONV_HDR - (CONV_W - 1) + r, :]
    for c in range(R // CHUNK):
        _ssd_prep(w, t, dt_scr, c, False)
    for c in range(R // CHUNK):
        _ssd_intra(w, t, c, L, False)

    def state_block(ssm_in_ref, ssm_out_ref):
        i = block_ref[0]
        block_ref[0] = i + 1
        for q in range(0, seq_per_block, 2):
            _sample_state_pair(t, (i * seq_per_block + q) * L, L,
                               (ssm_in_ref.at[q], ssm_in_ref.at[q + 1]),
                               (ssm_out_ref.at[q], ssm_out_ref.at[q + 1]))

    st_spec = pl.BlockSpec((seq_per_block, D_INNER, D_STATE), lambda i: (first_block + i, 0, 0))
    block_ref[0] = 0
    pltpu.emit_pipeline(state_block, grid=(S // seq_per_block,), in_specs=[st_spec],
                        out_specs=[st_spec])(ssm_in_hbm, ssm_out_hbm)

    _dense_out(w, t, x_ref, lambda ss: mod_ref[ss][:, :, 2 * D_MODEL:3 * D_MODEL], y_ref, S, L)


def _resident(shape):
    zeros = (0,) * len(shape)
    return pl.BlockSpec(shape, lambda *_: zeros, pipeline_mode=pl.Buffered(1))


def _ssd_scratch(R, aux_shape):
    return [
        pltpu.VMEM((R, D_INNER), F32),
        pltpu.VMEM((R, N_BC_GROUPS * D_STATE), F32),
        pltpu.VMEM((R, N_BC_GROUPS * D_STATE), F32),
        pltpu.VMEM((R, D_INNER), F32),
        pltpu.VMEM((R, D_POOL), BF16),
        pltpu.VMEM((R, D_MODEL), F32),
        pltpu.VMEM((R, D_INNER), F32),
        pltpu.VMEM((R, D_INNER), F32),
        pltpu.VMEM((R // CHUNK, 2, CHUNK, LANES), F32),
        pltpu.VMEM(aux_shape, F32),
        pltpu.VMEM((R, D_INNER), BF16),
        pltpu.VMEM((R, D_MODEL), F32),
        pltpu.VMEM((R, D_INNER), F32),
        pltpu.VMEM((R, 2 * D_MODEL), F32),
    ]


def _stage_scratch(S, L):
    return [
        pltpu.VMEM((S * L, D_MODEL), BF16),
        pltpu.VMEM((S, POOL_HDR + L, D_POOL), F32),
        pltpu.VMEM((S * L, D_POOL), F32),
        pltpu.VMEM((S, CONV_HDR + L, D_XBC), F32),
        pltpu.VMEM((S * L, LANES), F32),
    ]


def _prompt_call(x, mod, weights, *, L):
    nb, seq, _ = x.shape
    x_spec = pl.BlockSpec((1, L, D_MODEL), lambda b, t: (b, t, 0))
    seq_map = lambda b, t: (b, 0, 0)
    in_specs = ([x_spec, pl.BlockSpec((1, 1, 3 * D_MODEL), seq_map)]
                + [_resident(v.shape) for v in weights])
    whole = lambda b, t: (0, 0, 0)
    out_shape = (jax.ShapeDtypeStruct(x.shape, F32),
                 jax.ShapeDtypeStruct((POOL_BUF, nb, D_POOL), F32),
                 jax.ShapeDtypeStruct((CONV_W - 1, nb, D_XBC), F32),
                 jax.ShapeDtypeStruct((nb, D_INNER, D_STATE), F32))
    out_specs = (x_spec,
                 pl.BlockSpec((POOL_BUF, nb, D_POOL), whole),
                 pl.BlockSpec((CONV_W - 1, nb, D_XBC), whole),
                 pl.BlockSpec((1, D_INNER, D_STATE), seq_map))
    scratch = (_stage_scratch(1, L) + _ssd_scratch(L, (D_STATE, D_INNER))
               + [pltpu.VMEM((CHUNK, GROUP_COLS), F32)])
    kern = functools.partial(_prompt_kernel, L=L)
    return pl.pallas_call(
        kern, grid=(nb, seq // L), in_specs=in_specs, out_specs=out_specs, out_shape=out_shape,
        scratch_shapes=scratch, name="layer_prompt",
        compiler_params=pltpu.CompilerParams(
            dimension_semantics=("arbitrary", "arbitrary"), vmem_limit_bytes=VMEM_LIMIT_BYTES),
    )(x, mod, *weights)


def _sample_call(x, mod, states, weights, *, S, seq_per_block):
    ns, L, _ = x.shape
    R = S * L
    seq_map = lambda o: (o, 0, 0)
    time_major = lambda o: (0, o, 0)
    x_spec = pl.BlockSpec((S, L, D_MODEL), seq_map)
    pool_spec = pl.BlockSpec((POOL_BUF, S, D_POOL), time_major)
    conv_spec = pl.BlockSpec((CONV_W - 1, S, D_XBC), time_major)
    st_spec = pl.BlockSpec(memory_space=pl.ANY)
    in_specs = ([x_spec, pl.BlockSpec((S, 1, 3 * D_MODEL), seq_map), pool_spec, conv_spec, st_spec]
                + [_resident(v.shape) for v in weights])
    out_shape = (jax.ShapeDtypeStruct(x.shape, F32),
                 jax.ShapeDtypeStruct((POOL_BUF, ns, D_POOL), F32),
                 jax.ShapeDtypeStruct((CONV_W - 1, ns, D_XBC), F32),
                 jax.ShapeDtypeStruct((ns, D_INNER, D_STATE), F32))
    out_specs = (x_spec, pool_spec, conv_spec, st_spec)
    scratch = (_stage_scratch(S, L) + _ssd_scratch(R, (R // CHUNK, LANES, CHUNK))
               + [pltpu.SMEM((1,), jnp.int32)])
    kern = functools.partial(_sample_kernel, S=S, L=L, seq_per_block=seq_per_block)
    return pl.pallas_call(
        kern, grid=(ns // S,), in_specs=in_specs, out_specs=out_specs,
        out_shape=out_shape, scratch_shapes=scratch, name="layer_sample",
        compiler_params=pltpu.CompilerParams(
            dimension_semantics=("arbitrary",), vmem_limit_bytes=VMEM_LIMIT_BYTES),
    )(x, mod, *states, *weights)


def _block_sum_matrix(ql):
    r = jnp.arange(CHUNK)[:, None]
    c = jnp.arange(CHUNK)[None, :]
    same = (r // ql) == (c // ql)
    return jnp.concatenate([same & (c <= r), same], axis=0).astype(BF16)


def _expand_matrix():
    k = jnp.arange(LANES)[:, None]
    c = jnp.arange(D_INNER)[None, :] // HEAD_DIM
    return ((k % N_HEADS == c) & (k < 3 * N_HEADS)).astype(BF16)


def kernel(x_prompt, x_sample, state_pool, state_conv, state_ssm, c_prompt, c_sample, w_ada, b_ada, norm_g, w_in, conv_w, conv_b, dt_bias, a_log, d_skip, ssm_norm_g, pool_w, pool_scale, w_pool_out, w_ssm_out, w_o, final_g):
    depth = w_ada.shape[0]
    assert depth == 1, "kernel is written for a single layer"
    nb, seq, _ = x_prompt.shape
    ns, dec_seq, _ = x_sample.shape
    l = 0

    def head_lanes(v):
        return jnp.pad(jnp.tile(v, (1, 3)), ((0, 0), (0, LANES - 3 * N_HEADS)))

    c_all = jnp.concatenate([c_prompt, c_sample], axis=0)
    mod = pl.pallas_call(
        _adaln_kernel, out_shape=jax.ShapeDtypeStruct((nb + ns, 3 * D_MODEL), F32), name="adaln",
        compiler_params=pltpu.CompilerParams(vmem_limit_bytes=VMEM_LIMIT_BYTES),
    )(c_all, w_ada[l].astype(BF16), b_ada[l][None, :])
    mod = mod[:, None, :]

    e3 = _expand_matrix()
    w_main, w_tail = _repack_w_in(jnp.transpose(w_in[l]))

    def weights(ql):
        return [
            norm_g[l][None, :],
            w_main, w_tail,
            conv_w[l], conv_b[l][None, :],
            head_lanes(dt_bias[l][None, :]),
            head_lanes(a_log[l][None, :]),
            jnp.repeat(d_skip[l], HEAD_DIM)[None, :],
            ssm_norm_g[l][None, :],
            pool_w[l].astype(BF16), pool_scale[l][None, :],
            w_pool_out[l].astype(BF16), w_ssm_out[l].astype(BF16), w_o[l].astype(BF16),
            final_g[None, :],
            _block_sum_matrix(ql), e3,
        ]

    tile_rows = 2 * CHUNK
    y_p, pool_p, conv_p, ssm_p = _prompt_call(x_prompt, mod[:nb], weights(CHUNK), L=tile_rows)

    swap = lambda v: jnp.transpose(v, (1, 0, 2))
    states = (swap(state_pool[l]), swap(state_conv[l]),
              state_ssm[l].reshape(ns, D_INNER, D_STATE))
    y_s, pool_s, conv_s, ssm_s = _sample_call(
        x_sample, mod[nb:], states, weights(dec_seq), S=CHUNK // dec_seq, seq_per_block=2)

    shp = (N_HEADS, HEAD_DIM, D_STATE)
    return (y_p, y_s, swap(pool_p)[None], swap(conv_p)[None], ssm_p.reshape((1, nb) + shp),
            swap(pool_s)[None], swap(conv_s)[None], ssm_s.reshape((1, ns) + shp))
```

```python
import functools
import types

import jax
import jax.numpy as jnp
from jax import lax
from jax.experimental import pallas as pl
from jax.experimental.pallas import tpu as pltpu

F32 = jnp.float32
BF16 = jnp.bfloat16

EPS = 1e-6
D_MODEL = 1024
POOL_WINDOWS = (2, 4, 8, 16)
POOL_GROUP = 128
D_POOL = POOL_GROUP * len(POOL_WINDOWS)
POOL_BUF = max(POOL_WINDOWS) - 1
D_INNER = 2048
HEAD_DIM = 64
N_HEADS = D_INNER // HEAD_DIM
N_BC_GROUPS = 4
HEADS_PER_GROUP = N_HEADS // N_BC_GROUPS
GROUP_COLS = HEADS_PER_GROUP * HEAD_DIM
D_STATE = 128
CONV_W = 4
D_XBC = D_INNER + 2 * N_BC_GROUPS * D_STATE
PAST_LEN = 16384

LANES = 128
CHUNK = 128
POOL_HDR = 16
CONV_HDR = 8
DOT_COLS = 512
ROW_BLOCK = 16
CONV_ROW_BLOCK = 32
VMEM_LIMIT_BYTES = 63 * 1024 * 1024

COL_UP = 0
COL_ZP = D_POOL
COL_ZS = 2 * D_POOL
COL_XBC = COL_ZS + D_INNER
COL_DT = COL_XBC + D_XBC
COL_G = COL_DT + LANES
N_WEIGHTS = 17
LOG2E = 1.4426950408889634


def _dot(a, b):
    return jnp.dot(a, b, preferred_element_type=F32)


def _dot_nt(a, b):
    return lax.dot_general(a, b, (((1,), (1,)), ((), ())), preferred_element_type=F32)


def _pack3(v, lane):
    hi = v.astype(BF16).astype(F32)
    r = v - hi
    mid = r.astype(BF16).astype(F32)
    lo = r - mid
    packed = jnp.where(lane < N_HEADS, hi,
                       jnp.where(lane < 2 * N_HEADS, mid,
                                 jnp.where(lane < 3 * N_HEADS, lo, 0.0)))
    return packed.astype(BF16)


def _silu(v):
    return v / (1.0 + jnp.exp2(v * -LOG2E))


def _sigmoid(v):
    return 1.0 / (1.0 + jnp.exp2(v * -LOG2E))


def _softplus(v):
    return jnp.maximum(v, 0.0) + jnp.log1p(jnp.exp(-jnp.abs(v)))


def _repack_main_kernel(x_ref, o_ref):
    o_ref[...] = x_ref[...].T.astype(o_ref.dtype)


def _repack_tail_kernel(dt_ref, g_ref, o_ref, *, g_row0):
    dt = dt_ref[...]
    heads = jnp.concatenate([dt, dt, dt, jnp.zeros((LANES - 3 * N_HEADS, D_MODEL), F32)], axis=0)
    o_ref[:, 0:LANES] = heads.T.astype(o_ref.dtype)
    for j in range(0, 2 * D_MODEL, DOT_COLS):
        o_ref[:, LANES + j:LANES + j + DOT_COLS] = (
            g_ref[g_row0 + j:g_row0 + j + DOT_COLS, :].T.astype(o_ref.dtype))


def _repack_w_in(w_in_t):
    n_in, d = w_in_t.shape
    assert d == D_MODEL and n_in == COL_DT + N_HEADS + 2 * D_MODEL
    w_main = pl.pallas_call(
        _repack_main_kernel, grid=(COL_DT // DOT_COLS,),
        in_specs=[pl.BlockSpec((DOT_COLS, D_MODEL), lambda j: (j, 0))],
        out_specs=pl.BlockSpec((D_MODEL, DOT_COLS), lambda j: (0, j)),
        out_shape=jax.ShapeDtypeStruct((D_MODEL, COL_DT), BF16), name="repack_w_main",
    )(w_in_t)
    g_block = n_in // 4
    g_row = COL_DT + N_HEADS
    g_row0 = g_row % g_block
    assert n_in % g_block == 0 and g_block % 8 == 0 and g_row0 % 8 == 0
    assert g_row0 + 2 * D_MODEL <= g_block and COL_DT % N_HEADS == 0
    w_tail = pl.pallas_call(
        functools.partial(_repack_tail_kernel, g_row0=g_row0), grid=(1,),
        in_specs=[pl.BlockSpec((N_HEADS, D_MODEL), lambda i: (COL_DT // N_HEADS, 0)),
                  pl.BlockSpec((g_block, D_MODEL), lambda i: (g_row // g_block, 0))],
        out_specs=pl.BlockSpec((D_MODEL, LANES + 2 * D_MODEL), lambda i: (0, 0)),
        out_shape=jax.ShapeDtypeStruct((D_MODEL, LANES + 2 * D_MODEL), BF16), name="repack_w_tail",
        compiler_params=pltpu.CompilerParams(vmem_limit_bytes=VMEM_LIMIT_BYTES),
    )(w_in_t, w_in_t)
    return w_main, w_tail


def _adaln_kernel(c_ref, w_ref, b_ref, o_ref):
    c = c_ref[...]
    o_ref[...] = _dot(_silu(c).astype(BF16), w_ref[...]) + b_ref[...]


def _weight_refs(it):
    names = ("normg", "wmain", "wtail", "convw", "convb", "dtb", "alog", "dskip", "ssmg", "poolw",
             "pscale", "wpo", "wso", "wo", "fg", "ta", "e3")
    assert len(names) == N_WEIGHTS
    return types.SimpleNamespace(**{n: next(it) for n in names})


def _row_blocks(S, L, nrows):
    if L >= nrows:
        return [(slice(s, s + 1), slice(l0, l0 + nrows), slice(s * L + l0, s * L + l0 + nrows))
                for s in range(S) for l0 in range(0, L, nrows)]
    sb = nrows // L
    return [(slice(s0, s0 + sb), slice(0, L), slice(s0 * L, (s0 + sb) * L))
            for s0 in range(0, S, sb)]


def _norm_modulate(w, x_ref, mod_of, h_ref, S, L):
    for ss, ls, rs in _row_blocks(S, L, ROW_BLOCK):
        x3 = x_ref[ss, ls, :]
        mod3 = mod_of(ss)
        ms = jnp.mean(x3 * x3, axis=-1, keepdims=True)
        hn = x3 * lax.rsqrt(ms + EPS) * w.normg[...]
        hm = hn * (1.0 + mod3[:, :, D_MODEL:2 * D_MODEL]) + mod3[:, :, 0:D_MODEL]
        h_ref[rs, :] = hm.reshape(rs.stop - rs.start, D_MODEL).astype(BF16)


def _roll_rows(x, d, drop_first=False):
    n, rows, lanes = x.shape
    tiles = [x[:, j:j + 8, :] for j in range(0, rows, 8)]
    first = 1 if drop_first else 0
    if d == 8:
        out = [tiles[j - 1] for j in range(first, len(tiles))]
    else:
        rolled = [pltpu.roll(tl, d, 1) for tl in tiles]
        sub = lax.broadcasted_iota(jnp.int32, (n, 8, lanes), 1)
        out = [jnp.where(sub < d, rolled[j - 1], rolled[j]) for j in range(first, len(tiles))]
    return jnp.concatenate(out, axis=1)


def _proj_chunks(col0, ncols):
    return [(col0 + j, min(DOT_COLS, ncols - j), j) for j in range(0, ncols, DOT_COLS)]


def _pool_windows(up3, t, pos0, S, L):
    for ss, ls, rs in _row_blocks(S, L, ROW_BLOCK):
        nseq, ntok = ss.stop - ss.start, ls.stop - ls.start
        pos = pos0 + ls.start + lax.broadcasted_iota(jnp.int32, (nseq, ntok, POOL_GROUP), 1)
        for gi, win in enumerate(POOL_WINDOWS):
            sl = slice(gi * POOL_GROUP, (gi + 1) * POOL_GROUP)
            s = up3[ss, ls.start:POOL_HDR + ls.stop, sl]
            u = s[:, POOL_HDR:, :]
            shift = 1
            while shift < win:
                s = s + _roll_rows(s, shift)
                shift *= 2
            cnt = jnp.minimum(pos + 1, win).astype(F32)
            t.pm[rs, sl] = (s[:, POOL_HDR:, :] / cnt - u).reshape(
                nseq * ntok, POOL_GROUP).astype(BF16)


def _pool_mix(w, zp, t):
    for gi in range(len(POOL_WINDOWS)):
        sl = slice(gi * POOL_GROUP, (gi + 1) * POOL_GROUP)
        yg = _dot(t.pm[:, sl], w.poolw[gi]) * w.pscale[:, sl]
        t.pm[:, sl] = (yg * _silu(zp[:, sl])).astype(BF16)
    for j in range(0, D_MODEL, DOT_COLS):
        t.p[:, j:j + DOT_COLS] = _dot(t.pm[...], w.wpo[:, j:j + DOT_COLS])


def _conv_chunk(w, xbc3, t, j, S, L):
    sl = slice(j, j + DOT_COLS)
    tile_of = lambda row: jnp.broadcast_to(row[None], (1, 8, DOT_COLS))
    taps_w = [tile_of(w.convw[k:k + 1, sl]) for k in range(CONV_W)]
    bias = tile_of(w.convb[:, sl])
    for ss, ls, rs in _row_blocks(S, L, CONV_ROW_BLOCK):
        ext = xbc3[ss, ls.start:CONV_HDR + ls.stop, sl]
        taps = [ext[:, CONV_HDR:, :] if k == CONV_W - 1
                else _roll_rows(ext, CONV_W - 1 - k, drop_first=True) for k in range(CONV_W)]
        out_tiles = []
        for r in range(0, ls.stop - ls.start, 8):
            acc = bias
            for k in range(CONV_W):
                acc = acc + taps[k][:, r:r + 8, :] * taps_w[k]
            out_tiles.append(_silu(acc))
        v = jnp.concatenate(out_tiles, axis=1).reshape(rs.stop - rs.start, DOT_COLS)
        if j < D_INNER:
            t.xs[rs, sl] = v
        elif j < D_INNER + N_BC_GROUPS * D_STATE:
            t.b[rs, :] = v
        else:
            t.c[rs, :] = v


def _ssd_prep(w, t, dt_ref, c, prompt):
    rows = pl.ds(c * CHUNK, CHUNK)
    lane = lax.broadcasted_iota(jnp.int32, (CHUNK, LANES), 1)
    neg_a = jnp.where(lane[0:1, :] < 3 * N_HEADS, -jnp.exp(w.alog[...]), 0.0)
    dtc = dt_ref[rows, :]
    a = dtc * neg_a
    cs = _dot(w.ta[...], _pack3(a, lane))
    cs = cs + pltpu.roll(cs, 2 * N_HEADS, 1)
    cs = cs + pltpu.roll(cs, N_HEADS, 1)
    acs = cs[0:CHUNK, :]
    last = cs[CHUNK:2 * CHUNK, :]
    exp_acs = jnp.exp(acs)
    wgt = jnp.exp(last - acs) * dtc
    t.ea[rows, :] = _dot(_pack3(exp_acs, lane), w.e3[...])
    t.we[rows, :] = _dot(_pack3(wgt, lane), w.e3[...])
    if not prompt:
        t.aux[c] = jnp.exp(last).T
    acs2 = acs * LOG2E
    t.hd[c, 0] = acs2
    t.hd[c, 1] = (acs2 - jnp.log2(dtc)).T


def _ssd_intra(w, t, c, QL, prompt):
    row_start = c * CHUNK
    rows = pl.ds(row_start, CHUNK)
    lane = lax.broadcasted_iota(jnp.int32, (CHUNK, LANES), 1)
    row = lax.broadcasted_iota(jnp.int32, (CHUNK, LANES), 0)
    acs2, shifted_t = t.hd[c, 0], t.hd[c, 1]
    ql_bits = QL.bit_length() - 1
    mask = ((lax.shift_right_logical(row, ql_bits) == lax.shift_right_logical(lane, ql_bits))
            & (lane <= row))
    for g in range(N_BC_GROUPS):
        gs = slice(g * D_STATE, (g + 1) * D_STATE)
        gc = slice(g * GROUP_COLS, (g + 1) * GROUP_COLS)
        cg = t.c[rows, gs].astype(BF16)
        cb = _dot_nt(cg, t.b[rows, gs].astype(BF16))
        if prompt:
            ht = t.aux[:, gc]
            t.yoff[...] = _dot(cg, ht.astype(BF16))
            xw = (t.xs[rows, gc] * t.we[rows, gc]).astype(BF16)
            upd = _dot(t.b[rows, gs].T.astype(BF16), xw)
            t.aux[:, gc] = ht * t.ea[row_start + CHUNK - 1:row_start + CHUNK, gc] + upd
        for k in range(HEADS_PER_GROUP // 2):
            atts = []
            for h in (g * HEADS_PER_GROUP + 2 * k, g * HEADS_PER_GROUP + 2 * k + 1):
                seg2 = acs2[:, h:h + 1] - shifted_t[h:h + 1, :]
                decay_dt = jnp.exp2(jnp.where(mask, seg2, -jnp.inf))
                atts.append((cb * decay_dt).astype(BF16))
            ps = slice((g * HEADS_PER_GROUP + 2 * k) * HEAD_DIM,
                       (g * HEADS_PER_GROUP + 2 * k + 2) * HEAD_DIM)
            xp = t.xs[rows, ps]
            rhs = jnp.concatenate([jnp.where(lane < HEAD_DIM, xp, 0.0).astype(BF16),
                                   jnp.where(lane >= HEAD_DIM, xp, 0.0).astype(BF16)], axis=0)
            yp = _dot(jnp.concatenate(atts, axis=1), rhs) + w.dskip[:, ps] * xp
            if prompt:
                yp = yp + t.yoff[:, 2 * k * HEAD_DIM:(2 * k + 2) * HEAD_DIM] * t.ea[rows, ps]
            t.yssd[rows, ps] = yp


def _sample_state_pair(t, r, QL, hins, houts):
    n = 2 * QL
    c = r // CHUNK
    r0 = r % CHUNK
    rows = pl.ds(pl.multiple_of(r, n), n)
    lane = lax.broadcasted_iota(jnp.int32, (CHUNK, LANES), 1)
    rowi = lax.broadcasted_iota(jnp.int32, (n, GROUP_COLS), 0)
    total_t = t.aux[c]
    for q in range(2):
        col = jnp.sum(jnp.where(lane == r0 + q * QL, total_t, 0.0), axis=1, keepdims=True)
        rb = jnp.broadcast_to(col, (CHUNK, LANES))
        mine = (rowi >= q * QL) & (rowi < (q + 1) * QL)
        for g in range(N_BC_GROUPS):
            gs = slice(g * D_STATE, (g + 1) * D_STATE)
            cs = slice(g * GROUP_COLS, (g + 1) * GROUP_COLS)
            hg = hins[q][cs, :]
            yoff = _dot_nt(t.c[rows, gs].astype(BF16), hg.astype(BF16)) * t.ea[rows, cs]
            t.yssd[rows, cs] = t.yssd[rows, cs] + jnp.where(mine, yoff, 0.0)
            xw = jnp.where(mine, t.xs[rows, cs] * t.we[rows, cs], 0.0)
            upd = _dot(xw.T.astype(BF16), t.b[rows, gs].astype(BF16))
            for e in range(HEADS_PER_GROUP):
                h = g * HEADS_PER_GROUP + e
                hs = slice(h * HEAD_DIM, (h + 1) * HEAD_DIM)
                es = slice(e * HEAD_DIM, (e + 1) * HEAD_DIM)
                houts[q][hs, :] = hg[es, :] * rb[h:h + 1, :] + upd[es, :]


def _dense_out(w, t, x_ref, gate_of, y_ref, S, L):
    R = S * L
    col_chunks = [slice(j, j + DOT_COLS) for j in range(0, D_INNER, DOT_COLS)]
    for r in range(0, R, ROW_BLOCK):
        rs = slice(r, r + ROW_BLOCK)
        ys = [t.yssd[rs, sl] * _silu(t.zs[rs, sl]) for sl in col_chunks]
        ssq = sum(jnp.sum(y * y, axis=-1, keepdims=True) for y in ys)
        inv = lax.rsqrt(ssq * (1.0 / D_INNER) + EPS)
        for y, sl in zip(ys, col_chunks):
            t.yn[rs, sl] = (y * inv * w.ssmg[:, sl]).astype(BF16)
    for j in range(0, D_MODEL, DOT_COLS):
        t.s[:, j:j + DOT_COLS] = _dot(t.yn[...], w.wso[:, j:j + DOT_COLS])
    for r in range(0, R, ROW_BLOCK):
        rs = slice(r, r + ROW_BLOCK)
        for j in range(0, D_MODEL, DOT_COLS):
            sl = slice(j, j + DOT_COLS)
            sl2 = slice(D_MODEL + j, D_MODEL + j + DOT_COLS)
            t.yn[rs, sl] = (_sigmoid(t.g[rs, sl]) * t.p[rs, sl]
                            + _sigmoid(t.g[rs, sl2]) * t.s[rs, sl]).astype(BF16)
    for j in range(0, D_MODEL, DOT_COLS):
        t.s[:, j:j + DOT_COLS] = _dot(t.yn[:, 0:D_MODEL], w.wo[:, j:j + DOT_COLS])
    for ss, ls, rs in _row_blocks(S, L, ROW_BLOCK):
        o3 = t.s[rs, :].reshape(ss.stop - ss.start, ls.stop - ls.start, D_MODEL)
        xn = x_ref[ss, ls, :] + gate_of(ss) * o3
        ms = jnp.mean(xn * xn, axis=-1, keepdims=True)
        y_ref[ss, ls, :] = xn * lax.rsqrt(ms + EPS) * w.fg[...]


def _ssd_refs(it):
    names = ("xs", "b", "c", "yssd", "pm", "p", "ea", "we", "hd", "aux", "yn", "s", "zs", "g")
    return types.SimpleNamespace(**{n: next(it) for n in names})


def _input_projection(w, h_ref, stores):
    for col0, (ncols, store) in stores.items():
        for col, wd, j in _proj_chunks(col0, ncols):
            if col < COL_DT:
                wc = w.wmain[:, col:col + wd]
            else:
                wc = w.wtail[:, col - COL_DT:col - COL_DT + wd]
            store(j, wd, _dot(h_ref[...], wc))


def _prompt_kernel(*refs, L):
    S = 1
    it = iter(refs)
    x_ref, mod_ref = next(it), next(it)
    w = _weight_refs(it)
    y_ref, pool_out_ref, conv_out_ref, ssm_out_ref = [next(it) for _ in range(4)]
    h_scr, up_scr, zp_scr, xbc_scr, dt_scr = [next(it) for _ in range(5)]
    t = _ssd_refs(it)
    t.yoff = next(it)
    step = pl.program_id(1)

    @pl.when(step == 0)
    def _():
        up_scr[:, 0:POOL_HDR, :] = jnp.zeros((S, POOL_HDR, D_POOL), F32)
        xbc_scr[:, 0:CONV_HDR, :] = jnp.zeros((S, CONV_HDR, D_XBC), F32)
        t.aux[...] = jnp.zeros(t.aux.shape, F32)

    _norm_modulate(w, x_ref, lambda ss: mod_ref[...], h_scr, S, L)

    def store_up(j, wd, v):
        up_scr[:, POOL_HDR:POOL_HDR + L, j:j + wd] = v.reshape(S, L, wd)

    def store_zp(j, wd, v):
        zp_scr[:, j:j + wd] = v

    def store_zs(j, wd, v):
        t.zs[:, j:j + wd] = v

    def store_xbc(j, wd, v):
        xbc_scr[:, CONV_HDR:CONV_HDR + L, j:j + wd] = v.reshape(S, L, wd)

    def store_dt(j, wd, v):
        dt_scr[...] = _softplus(v + w.dtb[...])

    def store_g(j, wd, v):
        t.g[:, j:j + wd] = v

    _input_projection(w, h_scr, {
        COL_UP: (D_POOL, store_up), COL_ZP: (D_POOL, store_zp), COL_DT: (LANES, store_dt),
        COL_XBC: (D_XBC, store_xbc), COL_ZS: (D_INNER, store_zs), COL_G: (2 * D_MODEL, store_g)})

    _pool_windows(up_scr, t, step * L, S, L)
    _pool_mix(w, zp_scr, t)
    up_scr[:, 0:POOL_HDR, :] = up_scr[:, L:L + POOL_HDR, :]

    for j in range(0, D_XBC, DOT_COLS):
        _conv_chunk(w, xbc_scr, t, j, S, L)
    xbc_scr[:, 0:CONV_HDR, :] = xbc_scr[:, L:L + CONV_HDR, :]

    for c in range(L // CHUNK):
        _ssd_prep(w, t, dt_scr, c, True)
    for c in range(L // CHUNK):
        _ssd_intra(w, t, c, CHUNK, True)

    _dense_out(w, t, x_ref, lambda ss: mod_ref[...][:, :, 2 * D_MODEL:3 * D_MODEL], y_ref, S, L)

    @pl.when(step == pl.num_programs(1) - 1)
    def _():
        for n in range(D_INNER // LANES):
            cols = slice(n * LANES, (n + 1) * LANES)
            ssm_out_ref[0, cols, :] = t.aux[:, cols].T
        batch = pl.ds(pl.program_id(0), 1)
        for r in range(POOL_BUF):
            pool_out_ref[r, batch, :] = up_scr[0, 1 + r:2 + r, :]
        for r in range(CONV_W - 1):
            r0 = CONV_HDR - (CONV_W - 1) + r
            conv_out_ref[r, batch, :] = xbc_scr[0, r0:r0 + 1, :]


def _sample_kernel(*refs, S, L, n_inner, seq_per_inner):
    R = S * L
    it = iter(refs)
    x_ref, mod_ref, pool_in_ref, conv_in_ref, ssm_in_ref = [next(it) for _ in range(5)]
    w = _weight_refs(it)
    y_ref, pool_out_ref, conv_out_ref, ssm_out_ref = [next(it) for _ in range(4)]
    h_scr, up_scr, zp_scr, xbc_scr, dt_scr = [next(it) for _ in range(5)]
    t = _ssd_refs(it)
    step = pl.program_id(1)

    @pl.when(step == 0)
    def _():
        _norm_modulate(w, x_ref, lambda ss: mod_ref[ss], h_scr, S, L)

        def store_up(j, wd, v):
            up_scr[:, POOL_HDR:POOL_HDR + L, j:j + wd] = v.reshape(S, L, wd)

        def store_zp(j, wd, v):
            zp_scr[:, j:j + wd] = v

        def store_zs(j, wd, v):
            t.zs[:, j:j + wd] = v

        def store_xbc(j, wd, v):
            xbc_scr[:, CONV_HDR:CONV_HDR + L, j:j + wd] = v.reshape(S, L, wd)

        def store_dt(j, wd, v):
            dt_scr[...] = _softplus(v + w.dtb[...])

        def store_g(j, wd, v):
            t.g[:, j:j + wd] = v

        _input_projection(w, h_scr, {
            COL_UP: (D_POOL, store_up), COL_ZP: (D_POOL, store_zp), COL_DT: (LANES, store_dt),
            COL_XBC: (D_XBC, store_xbc), COL_ZS: (D_INNER, store_zs),
            COL_G: (2 * D_MODEL, store_g)})

        for r in range(POOL_BUF):
            up_scr[:, 1 + r, :] = pool_in_ref[r]
        _pool_windows(up_scr, t, PAST_LEN, S, L)
        _pool_mix(w, zp_scr, t)
        for r in range(POOL_BUF):
            pool_out_ref[r] = up_scr[:, L + 1 + r, :]

        for r in range(CONV_W - 1):
            xbc_scr[:, CONV_HDR - (CONV_W - 1) + r, :] = conv_in_ref[r]
        for j in range(0, D_XBC, DOT_COLS):
            _conv_chunk(w, xbc_scr, t, j, S, L)
        for r in range(CONV_W - 1):
            conv_out_ref[r] = xbc_scr[:, L + CONV_HDR - (CONV_W - 1) + r, :]
        for c in range(R // CHUNK):
            _ssd_prep(w, t, dt_scr, c, False)
        for c in range(R // CHUNK):
            _ssd_intra(w, t, c, L, False)

    for q in range(0, seq_per_inner, 2):
        _sample_state_pair(t, (step * seq_per_inner + q) * L, L,
                           (ssm_in_ref.at[q], ssm_in_ref.at[q + 1]),
                           (ssm_out_ref.at[q], ssm_out_ref.at[q + 1]))

    @pl.when(step == n_inner - 1)
    def _():
        _dense_out(w, t, x_ref, lambda ss: mod_ref[ss][:, :, 2 * D_MODEL:3 * D_MODEL],
                   y_ref, S, L)


def _resident(shape):
    zeros = (0,) * len(shape)
    return pl.BlockSpec(shape, lambda *_: zeros, pipeline_mode=pl.Buffered(1))


def _ssd_scratch(R, aux_shape):
    return [
        pltpu.VMEM((R, D_INNER), F32),
        pltpu.VMEM((R, N_BC_GROUPS * D_STATE), F32),
        pltpu.VMEM((R, N_BC_GROUPS * D_STATE), F32),
        pltpu.VMEM((R, D_INNER), F32),
        pltpu.VMEM((R, D_POOL), BF16),
        pltpu.VMEM((R, D_MODEL), F32),
        pltpu.VMEM((R, D_INNER), F32),
        pltpu.VMEM((R, D_INNER), F32),
        pltpu.VMEM((R // CHUNK, 2, CHUNK, LANES), F32),
        pltpu.VMEM(aux_shape, F32),
        pltpu.VMEM((R, D_INNER), BF16),
        pltpu.VMEM((R, D_MODEL), F32),
        pltpu.VMEM((R, D_INNER), F32),
        pltpu.VMEM((R, 2 * D_MODEL), F32),
    ]


def _stage_scratch(S, L):
    return [
        pltpu.VMEM((S * L, D_MODEL), BF16),
        pltpu.VMEM((S, POOL_HDR + L, D_POOL), F32),
        pltpu.VMEM((S * L, D_POOL), F32),
        pltpu.VMEM((S, CONV_HDR + L, D_XBC), F32),
        pltpu.VMEM((S * L, LANES), F32),
    ]


def _prompt_call(x, mod, weights, *, L):
    nb, seq, _ = x.shape
    x_spec = pl.BlockSpec((1, L, D_MODEL), lambda b, t: (b, t, 0))
    seq_map = lambda b, t: (b, 0, 0)
    in_specs = ([x_spec, pl.BlockSpec((1, 1, 3 * D_MODEL), seq_map)]
                + [_resident(v.shape) for v in weights])
    whole = lambda b, t: (0, 0, 0)
    out_shape = (jax.ShapeDtypeStruct(x.shape, F32),
                 jax.ShapeDtypeStruct((POOL_BUF, nb, D_POOL), F32),
                 jax.ShapeDtypeStruct((CONV_W - 1, nb, D_XBC), F32),
                 jax.ShapeDtypeStruct((nb, D_INNER, D_STATE), F32))
    out_specs = (x_spec,
                 pl.BlockSpec((POOL_BUF, nb, D_POOL), whole),
                 pl.BlockSpec((CONV_W - 1, nb, D_XBC), whole),
                 pl.BlockSpec((1, D_INNER, D_STATE), seq_map))
    scratch = (_stage_scratch(1, L) + _ssd_scratch(L, (D_STATE, D_INNER))
               + [pltpu.VMEM((CHUNK, GROUP_COLS), F32)])
    kern = functools.partial(_prompt_kernel, L=L)
    return pl.pallas_call(
        kern, grid=(nb, seq // L), in_specs=in_specs, out_specs=out_specs, out_shape=out_shape,
        scratch_shapes=scratch, name="layer_prompt",
        compiler_params=pltpu.CompilerParams(
            dimension_semantics=("arbitrary", "arbitrary"), vmem_limit_bytes=VMEM_LIMIT_BYTES),
    )(x, mod, *weights)


def _sample_call(x, mod, states, weights, *, S, seq_per_inner):
    ns, L, _ = x.shape
    R = S * L
    n_inner = S // seq_per_inner
    seq_map = lambda o, i: (o, 0, 0)
    st_map = lambda o, i: (o * n_inner + i, 0, 0)
    time_major = lambda o, i: (0, o, 0)
    x_spec = pl.BlockSpec((S, L, D_MODEL), seq_map)
    pool_spec = pl.BlockSpec((POOL_BUF, S, D_POOL), time_major)
    conv_spec = pl.BlockSpec((CONV_W - 1, S, D_XBC), time_major)
    st_spec = pl.BlockSpec((seq_per_inner, D_INNER, D_STATE), st_map)
    in_specs = ([x_spec, pl.BlockSpec((S, 1, 3 * D_MODEL), seq_map), pool_spec, conv_spec, st_spec]
                + [_resident(v.shape) for v in weights])
    out_shape = (jax.ShapeDtypeStruct(x.shape, F32),
                 jax.ShapeDtypeStruct((POOL_BUF, ns, D_POOL), F32),
                 jax.ShapeDtypeStruct((CONV_W - 1, ns, D_XBC), F32),
                 jax.ShapeDtypeStruct((ns, D_INNER, D_STATE), F32))
    out_specs = (x_spec, pool_spec, conv_spec, st_spec)
    scratch = _stage_scratch(S, L) + _ssd_scratch(R, (R // CHUNK, LANES, CHUNK))
    kern = functools.partial(_sample_kernel, S=S, L=L, n_inner=n_inner,
                             seq_per_inner=seq_per_inner)
    return pl.pallas_call(
        kern, grid=(ns // S, n_inner), in_specs=in_specs, out_specs=out_specs,
        out_shape=out_shape, scratch_shapes=scratch, name="layer_sample",
        compiler_params=pltpu.CompilerParams(
            dimension_semantics=("arbitrary", "arbitrary"), vmem_limit_bytes=VMEM_LIMIT_BYTES),
    )(x, mod, *states, *weights)


def _block_sum_matrix(ql):
    r = jnp.arange(CHUNK)[:, None]
    c = jnp.arange(CHUNK)[None, :]
    same = (r // ql) == (c // ql)
    return jnp.concatenate([same & (c <= r), same], axis=0).astype(BF16)


def _expand_matrix():
    k = jnp.arange(LANES)[:, None]
    c = jnp.arange(D_INNER)[None, :] // HEAD_DIM
    return ((k % N_HEADS == c) & (k < 3 * N_HEADS)).astype(BF16)


def kernel(x_prompt, x_sample, state_pool, state_conv, state_ssm, c_prompt, c_sample, w_ada, b_ada, norm_g, w_in, conv_w, conv_b, dt_bias, a_log, d_skip, ssm_norm_g, pool_w, pool_scale, w_pool_out, w_ssm_out, w_o, final_g):
    depth = w_ada.shape[0]
    assert depth == 1, "kernel is written for a single layer"
    nb, seq, _ = x_prompt.shape
    ns, dec_seq, _ = x_sample.shape
    l = 0

    def head_lanes(v):
        return jnp.pad(jnp.tile(v, (1, 3)), ((0, 0), (0, LANES - 3 * N_HEADS)))

    c_all = jnp.concatenate([c_prompt, c_sample], axis=0)
    mod = pl.pallas_call(
        _adaln_kernel, out_shape=jax.ShapeDtypeStruct((nb + ns, 3 * D_MODEL), F32), name="adaln",
        compiler_params=pltpu.CompilerParams(vmem_limit_bytes=VMEM_LIMIT_BYTES),
    )(c_all, w_ada[l].astype(BF16), b_ada[l][None, :])
    mod = mod[:, None, :]

    e3 = _expand_matrix()
    w_main, w_tail = _repack_w_in(jnp.transpose(w_in[l]))

    def weights(ql):
        return [
            norm_g[l][None, :],
            w_main, w_tail,
            conv_w[l], conv_b[l][None, :],
            head_lanes(dt_bias[l][None, :]),
            head_lanes(a_log[l][None, :]),
            jnp.repeat(d_skip[l], HEAD_DIM)[None, :],
            ssm_norm_g[l][None, :],
            pool_w[l].astype(BF16), pool_scale[l][None, :],
            w_pool_out[l].astype(BF16), w_ssm_out[l].astype(BF16), w_o[l].astype(BF16),
            final_g[None, :],
            _block_sum_matrix(ql), e3,
        ]

    tile_rows = 2 * CHUNK
    y_p, pool_p, conv_p, ssm_p = _prompt_call(x_prompt, mod[:nb], weights(CHUNK), L=tile_rows)

    swap = lambda v: jnp.transpose(v, (1, 0, 2))
    states = (swap(state_pool[l]), swap(state_conv[l]),
              state_ssm[l].reshape(ns, D_INNER, D_STATE))
    y_s, pool_s, conv_s, ssm_s = _sample_call(
        x_sample, mod[nb:], states, weights(dec_seq), S=CHUNK // dec_seq, seq_per_inner=4)

    shp = (N_HEADS, HEAD_DIM, D_STATE)
    return (y_p, y_s, swap(pool_p)[None], swap(conv_p)[None], ssm_p.reshape((1, nb) + shp),
            swap(pool_s)[None], swap(conv_s)[None], ssm_s.reshape((1, ns) + shp))
```

```python
import functools
import types

import jax
import jax.numpy as jnp
from jax import lax
from jax.experimental import pallas as pl
from jax.experimental.pallas import tpu as pltpu

F32 = jnp.float32
BF16 = jnp.bfloat16

EPS = 1e-6
D_MODEL = 1024
POOL_WINDOWS = (2, 4, 8, 16)
POOL_GROUP = 128
D_POOL = POOL_GROUP * len(POOL_WINDOWS)
POOL_BUF = max(POOL_WINDOWS) - 1
D_INNER = 2048
HEAD_DIM = 64
N_HEADS = D_INNER // HEAD_DIM
N_BC_GROUPS = 4
HEADS_PER_GROUP = N_HEADS // N_BC_GROUPS
GROUP_COLS = HEADS_PER_GROUP * HEAD_DIM
D_STATE = 128
CONV_W = 4
D_XBC = D_INNER + 2 * N_BC_GROUPS * D_STATE
PAST_LEN = 16384

LANES = 128
CHUNK = 128
POOL_HDR = 16
CONV_HDR = 8
DOT_COLS = 512
ROW_BLOCK = 16
CONV_ROW_BLOCK = 32
VMEM_LIMIT_BYTES = 63 * 1024 * 1024

COL_UP = 0
COL_ZP = D_POOL
COL_ZS = 2 * D_POOL
COL_XBC = COL_ZS + D_INNER
COL_DT = COL_XBC + D_XBC
COL_G = COL_DT + LANES
N_WEIGHTS = 17
LOG2E = 1.4426950408889634


def _dot(a, b):
    return jnp.dot(a, b, preferred_element_type=F32)


def _dot_nt(a, b):
    return lax.dot_general(a, b, (((1,), (1,)), ((), ())), preferred_element_type=F32)


def _pack3(v, lane):
    hi = v.astype(BF16).astype(F32)
    r = v - hi
    mid = r.astype(BF16).astype(F32)
    lo = r - mid
    packed = jnp.where(lane < N_HEADS, hi,
                       jnp.where(lane < 2 * N_HEADS, mid,
                                 jnp.where(lane < 3 * N_HEADS, lo, 0.0)))
    return packed.astype(BF16)


def _silu(v):
    return v / (1.0 + jnp.exp2(v * -LOG2E))


def _sigmoid(v):
    return 1.0 / (1.0 + jnp.exp2(v * -LOG2E))


def _softplus(v):
    return jnp.maximum(v, 0.0) + jnp.log1p(jnp.exp(-jnp.abs(v)))


def _repack_main_kernel(x_ref, o_ref):
    o_ref[...] = x_ref[...].T.astype(o_ref.dtype)


def _repack_tail_kernel(dt_ref, g_ref, o_ref, *, g_row0):
    dt = dt_ref[...]
    heads = jnp.concatenate([dt, dt, dt, jnp.zeros((LANES - 3 * N_HEADS, D_MODEL), F32)], axis=0)
    o_ref[:, 0:LANES] = heads.T.astype(o_ref.dtype)
    for j in range(0, 2 * D_MODEL, DOT_COLS):
        o_ref[:, LANES + j:LANES + j + DOT_COLS] = (
            g_ref[g_row0 + j:g_row0 + j + DOT_COLS, :].T.astype(o_ref.dtype))


def _repack_w_in(w_in_t):
    n_in, d = w_in_t.shape
    assert d == D_MODEL and n_in == COL_DT + N_HEADS + 2 * D_MODEL
    w_main = pl.pallas_call(
        _repack_main_kernel, grid=(COL_DT // D_MODEL,),
        in_specs=[pl.BlockSpec((D_MODEL, D_MODEL), lambda j: (j, 0))],
        out_specs=pl.BlockSpec((D_MODEL, D_MODEL), lambda j: (0, j)),
        out_shape=jax.ShapeDtypeStruct((D_MODEL, COL_DT), BF16), name="repack_w_main",
    )(w_in_t)
    g_block = n_in // 4
    g_row = COL_DT + N_HEADS
    g_row0 = g_row % g_block
    assert n_in % g_block == 0 and g_block % 8 == 0 and g_row0 % 8 == 0
    assert g_row0 + 2 * D_MODEL <= g_block and COL_DT % N_HEADS == 0
    w_tail = pl.pallas_call(
        functools.partial(_repack_tail_kernel, g_row0=g_row0), grid=(1,),
        in_specs=[pl.BlockSpec((N_HEADS, D_MODEL), lambda i: (COL_DT // N_HEADS, 0)),
                  pl.BlockSpec((g_block, D_MODEL), lambda i: (g_row // g_block, 0))],
        out_specs=pl.BlockSpec((D_MODEL, LANES + 2 * D_MODEL), lambda i: (0, 0)),
        out_shape=jax.ShapeDtypeStruct((D_MODEL, LANES + 2 * D_MODEL), BF16), name="repack_w_tail",
        compiler_params=pltpu.CompilerParams(vmem_limit_bytes=VMEM_LIMIT_BYTES),
    )(w_in_t, w_in_t)
    return w_main, w_tail


def _adaln_kernel(c_ref, w_ref, b_ref, o_ref):
    c = c_ref[...]
    o_ref[...] = _dot(_silu(c).astype(BF16), w_ref[...].astype(BF16)) + b_ref[...]


def _weight_refs(it):
    names = ("normg", "wmain", "wtail", "convw", "convb", "dtb", "alog", "dskip", "ssmg", "poolw",
             "pscale", "wpo", "wso", "wo", "fg", "ta", "e3")
    assert len(names) == N_WEIGHTS
    return types.SimpleNamespace(**{n: next(it) for n in names})


def _row_blocks(S, L, nrows):
    if L >= nrows:
        return [(slice(s, s + 1), slice(l0, l0 + nrows), slice(s * L + l0, s * L + l0 + nrows))
                for s in range(S) for l0 in range(0, L, nrows)]
    sb = nrows // L
    return [(slice(s0, s0 + sb), slice(0, L), slice(s0 * L, (s0 + sb) * L))
            for s0 in range(0, S, sb)]


def _norm_modulate(w, x_ref, mod_of, h_ref, S, L):
    for ss, ls, rs in _row_blocks(S, L, ROW_BLOCK):
        x3 = x_ref[ss, ls, :]
        mod3 = mod_of(ss)
        ms = jnp.mean(x3 * x3, axis=-1, keepdims=True)
        hn = x3 * lax.rsqrt(ms + EPS) * w.normg[...]
        hm = hn * (1.0 + mod3[:, :, D_MODEL:2 * D_MODEL]) + mod3[:, :, 0:D_MODEL]
        h_ref[rs, :] = hm.reshape(rs.stop - rs.start, D_MODEL).astype(BF16)


def _roll_rows(x, d, drop_first=False):
    n, rows, lanes = x.shape
    tiles = [x[:, j:j + 8, :] for j in range(0, rows, 8)]
    first = 1 if drop_first else 0
    if d == 8:
        out = [tiles[j - 1] for j in range(first, len(tiles))]
    else:
        rolled = [pltpu.roll(tl, d, 1) for tl in tiles]
        sub = lax.broadcasted_iota(jnp.int32, (n, 8, lanes), 1)
        out = [jnp.where(sub < d, rolled[j - 1], rolled[j]) for j in range(first, len(tiles))]
    return jnp.concatenate(out, axis=1)


def _proj_chunks(col0, ncols):
    return [(col0 + j, min(DOT_COLS, ncols - j), j) for j in range(0, ncols, DOT_COLS)]


def _pool_windows(up3, t, pos0, S, L):
    for ss, ls, rs in _row_blocks(S, L, ROW_BLOCK):
        nseq, ntok = ss.stop - ss.start, ls.stop - ls.start
        pos = pos0 + ls.start + lax.broadcasted_iota(jnp.int32, (nseq, ntok, POOL_GROUP), 1)
        for gi, win in enumerate(POOL_WINDOWS):
            sl = slice(gi * POOL_GROUP, (gi + 1) * POOL_GROUP)
            s = up3[ss, ls.start:POOL_HDR + ls.stop, sl]
            u = s[:, POOL_HDR:, :]
            shift = 1
            while shift < win:
                s = s + _roll_rows(s, shift)
                shift *= 2
            cnt = jnp.minimum(pos + 1, win).astype(F32)
            t.pm[rs, sl] = (s[:, POOL_HDR:, :] / cnt - u).reshape(
                nseq * ntok, POOL_GROUP).astype(BF16)


def _pool_mix(w, zp, t):
    for gi in range(len(POOL_WINDOWS)):
        sl = slice(gi * POOL_GROUP, (gi + 1) * POOL_GROUP)
        yg = _dot(t.pm[:, sl], w.poolw[gi]) * w.pscale[:, sl]
        t.pm[:, sl] = (yg * _silu(zp[:, sl])).astype(BF16)
    for j in range(0, D_MODEL, DOT_COLS):
        t.p[:, j:j + DOT_COLS] = _dot(t.pm[...], w.wpo[:, j:j + DOT_COLS])


def _conv_chunk(w, xbc3, t, j, S, L):
    sl = slice(j, j + DOT_COLS)
    tile_of = lambda row: jnp.broadcast_to(row[None], (1, 8, DOT_COLS))
    taps_w = [tile_of(w.convw[k:k + 1, sl]) for k in range(CONV_W)]
    bias = tile_of(w.convb[:, sl])
    for ss, ls, rs in _row_blocks(S, L, CONV_ROW_BLOCK):
        ext = xbc3[ss, ls.start:CONV_HDR + ls.stop, sl]
        taps = [ext[:, CONV_HDR:, :] if k == CONV_W - 1
                else _roll_rows(ext, CONV_W - 1 - k, drop_first=True) for k in range(CONV_W)]
        out_tiles = []
        for r in range(0, ls.stop - ls.start, 8):
            acc = bias
            for k in range(CONV_W):
                acc = acc + taps[k][:, r:r + 8, :] * taps_w[k]
            out_tiles.append(_silu(acc))
        v = jnp.concatenate(out_tiles, axis=1).reshape(rs.stop - rs.start, DOT_COLS)
        if j < D_INNER:
            t.xs[rs, sl] = v
        elif j < D_INNER + N_BC_GROUPS * D_STATE:
            t.b[rs, :] = v
        else:
            t.c[rs, :] = v


def _ssd_prep(w, t, dt_ref, c, prompt):
    rows = pl.ds(c * CHUNK, CHUNK)
    lane = lax.broadcasted_iota(jnp.int32, (CHUNK, LANES), 1)
    neg_a = jnp.where(lane[0:1, :] < 3 * N_HEADS, -jnp.exp(w.alog[...]), 0.0)
    dtc = dt_ref[rows, :]
    a = dtc * neg_a
    cs = _dot(w.ta[...], _pack3(a, lane))
    cs = cs + pltpu.roll(cs, 2 * N_HEADS, 1)
    cs = cs + pltpu.roll(cs, N_HEADS, 1)
    acs = cs[0:CHUNK, :]
    last = cs[CHUNK:2 * CHUNK, :]
    exp_acs = jnp.exp(acs)
    wgt = jnp.exp(last - acs) * dtc
    t.ea[rows, :] = _dot(_pack3(exp_acs, lane), w.e3[...])
    t.we[rows, :] = _dot(_pack3(wgt, lane), w.e3[...])
    if not prompt:
        t.aux[c] = jnp.exp(last).T
    acs2 = acs * LOG2E
    t.hd[c, 0] = acs2
    t.hd[c, 1] = (acs2 - jnp.log2(dtc)).T


def _ssd_intra(w, t, c, QL, prompt):
    row_start = c * CHUNK
    rows = pl.ds(row_start, CHUNK)
    lane = lax.broadcasted_iota(jnp.int32, (CHUNK, LANES), 1)
    row = lax.broadcasted_iota(jnp.int32, (CHUNK, LANES), 0)
    acs2, shifted_t = t.hd[c, 0], t.hd[c, 1]
    ql_bits = QL.bit_length() - 1
    mask = ((lax.shift_right_logical(row, ql_bits) == lax.shift_right_logical(lane, ql_bits))
            & (lane <= row))
    for g in range(N_BC_GROUPS):
        gs = slice(g * D_STATE, (g + 1) * D_STATE)
        gc = slice(g * GROUP_COLS, (g + 1) * GROUP_COLS)
        cg = t.c[rows, gs].astype(BF16)
        cb = _dot_nt(cg, t.b[rows, gs].astype(BF16))
        if prompt:
            ht = t.aux[:, gc]
            t.yoff[...] = _dot(cg, ht.astype(BF16))
            xw = (t.xs[rows, gc] * t.we[rows, gc]).astype(BF16)
            upd = _dot(t.b[rows, gs].T.astype(BF16), xw)
            t.aux[:, gc] = ht * t.ea[row_start + CHUNK - 1:row_start + CHUNK, gc] + upd
        for k in range(HEADS_PER_GROUP // 2):
            atts = []
            for h in (g * HEADS_PER_GROUP + 2 * k, g * HEADS_PER_GROUP + 2 * k + 1):
                seg2 = acs2[:, h:h + 1] - shifted_t[h:h + 1, :]
                decay_dt = jnp.exp2(jnp.where(mask, seg2, -jnp.inf))
                atts.append((cb * decay_dt).astype(BF16))
            ps = slice((g * HEADS_PER_GROUP + 2 * k) * HEAD_DIM,
                       (g * HEADS_PER_GROUP + 2 * k + 2) * HEAD_DIM)
            xp = t.xs[rows, ps]
            rhs = jnp.concatenate([jnp.where(lane < HEAD_DIM, xp, 0.0).astype(BF16),
                                   jnp.where(lane >= HEAD_DIM, xp, 0.0).astype(BF16)], axis=0)
            yp = _dot(jnp.concatenate(atts, axis=1), rhs) + w.dskip[:, ps] * xp
            if prompt:
                yp = yp + t.yoff[:, 2 * k * HEAD_DIM:(2 * k + 2) * HEAD_DIM] * t.ea[rows, ps]
            t.yssd[rows, ps] = yp


def _sample_state_pair(t, r, QL, hins, houts):
    n = 2 * QL
    c = r // CHUNK
    r0 = r % CHUNK
    rows = pl.ds(pl.multiple_of(r, n), n)
    lane = lax.broadcasted_iota(jnp.int32, (CHUNK, LANES), 1)
    rowi = lax.broadcasted_iota(jnp.int32, (n, GROUP_COLS), 0)
    total_t = t.aux[c]
    for q in range(2):
        col = jnp.sum(jnp.where(lane == r0 + q * QL, total_t, 0.0), axis=1, keepdims=True)
        rb = jnp.broadcast_to(col, (CHUNK, LANES))
        mine = (rowi >= q * QL) & (rowi < (q + 1) * QL)
        for g in range(N_BC_GROUPS):
            gs = slice(g * D_STATE, (g + 1) * D_STATE)
            cs = slice(g * GROUP_COLS, (g + 1) * GROUP_COLS)
            hg = hins[q][cs, :]
            yoff = _dot_nt(t.c[rows, gs].astype(BF16), hg.astype(BF16)) * t.ea[rows, cs]
            t.yssd[rows, cs] = t.yssd[rows, cs] + jnp.where(mine, yoff, 0.0)
            xw = jnp.where(mine, t.xs[rows, cs] * t.we[rows, cs], 0.0)
            upd = _dot(xw.T.astype(BF16), t.b[rows, gs].astype(BF16))
            for e in range(HEADS_PER_GROUP):
                h = g * HEADS_PER_GROUP + e
                hs = slice(h * HEAD_DIM, (h + 1) * HEAD_DIM)
                es = slice(e * HEAD_DIM, (e + 1) * HEAD_DIM)
                houts[q][hs, :] = hg[es, :] * rb[h:h + 1, :] + upd[es, :]


def _dense_out(w, t, x_ref, gate_of, y_ref, S, L):
    R = S * L
    col_chunks = [slice(j, j + DOT_COLS) for j in range(0, D_INNER, DOT_COLS)]
    for r in range(0, R, ROW_BLOCK):
        rs = slice(r, r + ROW_BLOCK)
        ys = [t.yssd[rs, sl] * _silu(t.zs[rs, sl]) for sl in col_chunks]
        ssq = sum(jnp.sum(y * y, axis=-1, keepdims=True) for y in ys)
        inv = lax.rsqrt(ssq * (1.0 / D_INNER) + EPS)
        for y, sl in zip(ys, col_chunks):
            t.yn[rs, sl] = (y * inv * w.ssmg[:, sl]).astype(BF16)
    for j in range(0, D_MODEL, DOT_COLS):
        t.s[:, j:j + DOT_COLS] = _dot(t.yn[...], w.wso[:, j:j + DOT_COLS])
    for r in range(0, R, ROW_BLOCK):
        rs = slice(r, r + ROW_BLOCK)
        for j in range(0, D_MODEL, DOT_COLS):
            sl = slice(j, j + DOT_COLS)
            sl2 = slice(D_MODEL + j, D_MODEL + j + DOT_COLS)
            t.yn[rs, sl] = (_sigmoid(t.g[rs, sl]) * t.p[rs, sl]
                            + _sigmoid(t.g[rs, sl2]) * t.s[rs, sl]).astype(BF16)
    for j in range(0, D_MODEL, DOT_COLS):
        t.s[:, j:j + DOT_COLS] = _dot(t.yn[:, 0:D_MODEL], w.wo[:, j:j + DOT_COLS])
    for ss, ls, rs in _row_blocks(S, L, ROW_BLOCK):
        o3 = t.s[rs, :].reshape(ss.stop - ss.start, ls.stop - ls.start, D_MODEL)
        xn = x_ref[ss, ls, :] + gate_of(ss) * o3
        ms = jnp.mean(xn * xn, axis=-1, keepdims=True)
        y_ref[ss, ls, :] = xn * lax.rsqrt(ms + EPS) * w.fg[...]


def _ssd_refs(it):
    names = ("xs", "b", "c", "yssd", "pm", "p", "ea", "we", "hd", "aux", "yn", "s", "zs", "g")
    return types.SimpleNamespace(**{n: next(it) for n in names})


def _input_projection(w, h_ref, stores):
    for col0, (ncols, store) in stores.items():
        for col, wd, j in _proj_chunks(col0, ncols):
            if col < COL_DT:
                wc = w.wmain[:, col:col + wd]
            else:
                wc = w.wtail[:, col - COL_DT:col - COL_DT + wd]
            store(j, wd, _dot(h_ref[...], wc))


def _prompt_kernel(*refs, L):
    S = 1
    it = iter(refs)
    x_ref, mod_ref = next(it), next(it)
    w = _weight_refs(it)
    y_ref, pool_out_ref, conv_out_ref, ssm_out_ref = [next(it) for _ in range(4)]
    h_scr, up_scr, zp_scr, xbc_scr, dt_scr = [next(it) for _ in range(5)]
    t = _ssd_refs(it)
    t.yoff = next(it)
    step = pl.program_id(1)

    @pl.when(step == 0)
    def _():
        up_scr[:, 0:POOL_HDR, :] = jnp.zeros((S, POOL_HDR, D_POOL), F32)
        xbc_scr[:, 0:CONV_HDR, :] = jnp.zeros((S, CONV_HDR, D_XBC), F32)
        t.aux[...] = jnp.zeros(t.aux.shape, F32)

    _norm_modulate(w, x_ref, lambda ss: mod_ref[...], h_scr, S, L)

    def store_up(j, wd, v):
        up_scr[:, POOL_HDR:POOL_HDR + L, j:j + wd] = v.reshape(S, L, wd)

    def store_zp(j, wd, v):
        zp_scr[:, j:j + wd] = v

    def store_zs(j, wd, v):
        t.zs[:, j:j + wd] = v

    def store_xbc(j, wd, v):
        xbc_scr[:, CONV_HDR:CONV_HDR + L, j:j + wd] = v.reshape(S, L, wd)

    def store_dt(j, wd, v):
        dt_scr[...] = _softplus(v + w.dtb[...])

    def store_g(j, wd, v):
        t.g[:, j:j + wd] = v

    _input_projection(w, h_scr, {
        COL_UP: (D_POOL, store_up), COL_ZP: (D_POOL, store_zp), COL_DT: (LANES, store_dt),
        COL_XBC: (D_XBC, store_xbc), COL_ZS: (D_INNER, store_zs), COL_G: (2 * D_MODEL, store_g)})

    _pool_windows(up_scr, t, step * L, S, L)
    _pool_mix(w, zp_scr, t)
    up_scr[:, 0:POOL_HDR, :] = up_scr[:, L:L + POOL_HDR, :]

    for j in range(0, D_XBC, DOT_COLS):
        _conv_chunk(w, xbc_scr, t, j, S, L)
    xbc_scr[:, 0:CONV_HDR, :] = xbc_scr[:, L:L + CONV_HDR, :]

    for c in range(L // CHUNK):
        _ssd_prep(w, t, dt_scr, c, True)
    for c in range(L // CHUNK):
        _ssd_intra(w, t, c, CHUNK, True)

    _dense_out(w, t, x_ref, lambda ss: mod_ref[...][:, :, 2 * D_MODEL:3 * D_MODEL], y_ref, S, L)

    @pl.when(step == pl.num_programs(1) - 1)
    def _():
        for n in range(D_INNER // LANES):
            cols = slice(n * LANES, (n + 1) * LANES)
            ssm_out_ref[0, cols, :] = t.aux[:, cols].T
        batch = pl.ds(pl.program_id(0), 1)
        for r in range(POOL_BUF):
            pool_out_ref[r, batch, :] = up_scr[0, 1 + r:2 + r, :]
        for r in range(CONV_W - 1):
            r0 = CONV_HDR - (CONV_W - 1) + r
            conv_out_ref[r, batch, :] = xbc_scr[0, r0:r0 + 1, :]


def _sample_kernel(*refs, S, L, n_inner, seq_per_inner):
    R = S * L
    it = iter(refs)
    x_ref, mod_ref, pool_in_ref, conv_in_ref, ssm_in_ref = [next(it) for _ in range(5)]
    w = _weight_refs(it)
    y_ref, pool_out_ref, conv_out_ref, ssm_out_ref = [next(it) for _ in range(4)]
    h_scr, up_scr, zp_scr, xbc_scr, dt_scr = [next(it) for _ in range(5)]
    t = _ssd_refs(it)
    step = pl.program_id(1)

    @pl.when(step == 0)
    def _():
        _norm_modulate(w, x_ref, lambda ss: mod_ref[ss], h_scr, S, L)

        def store_up(j, wd, v):
            up_scr[:, POOL_HDR:POOL_HDR + L, j:j + wd] = v.reshape(S, L, wd)

        def store_zp(j, wd, v):
            zp_scr[:, j:j + wd] = v

        def store_zs(j, wd, v):
            t.zs[:, j:j + wd] = v

        def store_xbc(j, wd, v):
            xbc_scr[:, CONV_HDR:CONV_HDR + L, j:j + wd] = v.reshape(S, L, wd)

        def store_dt(j, wd, v):
            dt_scr[...] = _softplus(v + w.dtb[...])

        def store_g(j, wd, v):
            t.g[:, j:j + wd] = v

        _input_projection(w, h_scr, {
            COL_UP: (D_POOL, store_up), COL_ZP: (D_POOL, store_zp), COL_DT: (LANES, store_dt),
            COL_XBC: (D_XBC, store_xbc), COL_ZS: (D_INNER, store_zs),
            COL_G: (2 * D_MODEL, store_g)})

        for r in range(POOL_BUF):
            up_scr[:, 1 + r, :] = pool_in_ref[r]
        _pool_windows(up_scr, t, PAST_LEN, S, L)
        _pool_mix(w, zp_scr, t)
        for r in range(POOL_BUF):
            pool_out_ref[r] = up_scr[:, L + 1 + r, :]

        for r in range(CONV_W - 1):
            xbc_scr[:, CONV_HDR - (CONV_W - 1) + r, :] = conv_in_ref[r]
        for j in range(0, D_XBC, DOT_COLS):
            _conv_chunk(w, xbc_scr, t, j, S, L)
        for r in range(CONV_W - 1):
            conv_out_ref[r] = xbc_scr[:, L + CONV_HDR - (CONV_W - 1) + r, :]
        for c in range(R // CHUNK):
            _ssd_prep(w, t, dt_scr, c, False)
        for c in range(R // CHUNK):
            _ssd_intra(w, t, c, L, False)

    for q in range(0, seq_per_inner, 2):
        _sample_state_pair(t, (step * seq_per_inner + q) * L, L,
                           (ssm_in_ref.at[q], ssm_in_ref.at[q + 1]),
                           (ssm_out_ref.at[q], ssm_out_ref.at[q + 1]))

    @pl.when(step == n_inner - 1)
    def _():
        _dense_out(w, t, x_ref, lambda ss: mod_ref[ss][:, :, 2 * D_MODEL:3 * D_MODEL],
                   y_ref, S, L)


def _resident(shape):
    zeros = (0,) * len(shape)
    return pl.BlockSpec(shape, lambda *_: zeros, pipeline_mode=pl.Buffered(1))


def _ssd_scratch(R, aux_shape):
    return [
        pltpu.VMEM((R, D_INNER), F32),
        pltpu.VMEM((R, N_BC_GROUPS * D_STATE), F32),
        pltpu.VMEM((R, N_BC_GROUPS * D_STATE), F32),
        pltpu.VMEM((R, D_INNER), F32),
        pltpu.VMEM((R, D_POOL), BF16),
        pltpu.VMEM((R, D_MODEL), F32),
        pltpu.VMEM((R, D_INNER), F32),
        pltpu.VMEM((R, D_INNER), F32),
        pltpu.VMEM((R // CHUNK, 2, CHUNK, LANES), F32),
        pltpu.VMEM(aux_shape, F32),
        pltpu.VMEM((R, D_INNER), BF16),
        pltpu.VMEM((R, D_MODEL), F32),
        pltpu.VMEM((R, D_INNER), F32),
        pltpu.VMEM((R, 2 * D_MODEL), F32),
    ]


def _stage_scratch(S, L):
    return [
        pltpu.VMEM((S * L, D_MODEL), BF16),
        pltpu.VMEM((S, POOL_HDR + L, D_POOL), F32),
        pltpu.VMEM((S * L, D_POOL), F32),
        pltpu.VMEM((S, CONV_HDR + L, D_XBC), F32),
        pltpu.VMEM((S * L, LANES), F32),
    ]


def _prompt_call(x, mod, weights, *, L):
    nb, seq, _ = x.shape
    x_spec = pl.BlockSpec((1, L, D_MODEL), lambda b, t: (b, t, 0))
    seq_map = lambda b, t: (b, 0, 0)
    in_specs = ([x_spec, pl.BlockSpec((1, 1, 3 * D_MODEL), seq_map)]
                + [_resident(v.shape) for v in weights])
    whole = lambda b, t: (0, 0, 0)
    out_shape = (jax.ShapeDtypeStruct(x.shape, F32),
                 jax.ShapeDtypeStruct((POOL_BUF, nb, D_POOL), F32),
                 jax.ShapeDtypeStruct((CONV_W - 1, nb, D_XBC), F32),
                 jax.ShapeDtypeStruct((nb, D_INNER, D_STATE), F32))
    out_specs = (x_spec,
                 pl.BlockSpec((POOL_BUF, nb, D_POOL), whole),
                 pl.BlockSpec((CONV_W - 1, nb, D_XBC), whole),
                 pl.BlockSpec((1, D_INNER, D_STATE), seq_map))
    scratch = (_stage_scratch(1, L) + _ssd_scratch(L, (D_STATE, D_INNER))
               + [pltpu.VMEM((CHUNK, GROUP_COLS), F32)])
    kern = functools.partial(_prompt_kernel, L=L)
    return pl.pallas_call(
        kern, grid=(nb, seq // L), in_specs=in_specs, out_specs=out_specs, out_shape=out_shape,
        scratch_shapes=scratch, name="layer_prompt",
        compiler_params=pltpu.CompilerParams(
            dimension_semantics=("arbitrary", "arbitrary"), vmem_limit_bytes=VMEM_LIMIT_BYTES),
    )(x, mod, *weights)


def _sample_call(x, mod, states, weights, *, S, seq_per_inner):
    ns, L, _ = x.shape
    R = S * L
    n_inner = S // seq_per_inner
    seq_map = lambda o, i: (o, 0, 0)
    st_map = lambda o, i: (o * n_inner + i, 0, 0)
    time_major = lambda o, i: (0, o, 0)
    x_spec = pl.BlockSpec((S, L, D_MODEL), seq_map)
    pool_spec = pl.BlockSpec((POOL_BUF, S, D_POOL), time_major)
    conv_spec = pl.BlockSpec((CONV_W - 1, S, D_XBC), time_major)
    st_spec = pl.BlockSpec((seq_per_inner, D_INNER, D_STATE), st_map)
    in_specs = ([x_spec, pl.BlockSpec((S, 1, 3 * D_MODEL), seq_map), pool_spec, conv_spec, st_spec]
                + [_resident(v.shape) for v in weights])
    out_shape = (jax.ShapeDtypeStruct(x.shape, F32),
                 jax.ShapeDtypeStruct((POOL_BUF, ns, D_POOL), F32),
                 jax.ShapeDtypeStruct((CONV_W - 1, ns, D_XBC), F32),
                 jax.ShapeDtypeStruct((ns, D_INNER, D_STATE), F32))
    out_specs = (x_spec, pool_spec, conv_spec, st_spec)
    scratch = _stage_scratch(S, L) + _ssd_scratch(R, (R // CHUNK, LANES, CHUNK))
    kern = functools.partial(_sample_kernel, S=S, L=L, n_inner=n_inner,
                             seq_per_inner=seq_per_inner)
    return pl.pallas_call(
        kern, grid=(ns // S, n_inner), in_specs=in_specs, out_specs=out_specs,
        out_shape=out_shape, scratch_shapes=scratch, name="layer_sample",
        compiler_params=pltpu.CompilerParams(
            dimension_semantics=("arbitrary", "arbitrary"), vmem_limit_bytes=VMEM_LIMIT_BYTES),
    )(x, mod, *states, *weights)


def _block_sum_matrix(ql):
    r = jnp.arange(CHUNK)[:, None]
    c = jnp.arange(CHUNK)[None, :]
    same = (r // ql) == (c // ql)
    return jnp.concatenate([same & (c <= r), same], axis=0).astype(BF16)


def _expand_matrix():
    k = jnp.arange(LANES)[:, None]
    c = jnp.arange(D_INNER)[None, :] // HEAD_DIM
    return ((k % N_HEADS == c) & (k < 3 * N_HEADS)).astype(BF16)


def kernel(x_prompt, x_sample, state_pool, state_conv, state_ssm, c_prompt, c_sample, w_ada, b_ada, norm_g, w_in, conv_w, conv_b, dt_bias, a_log, d_skip, ssm_norm_g, pool_w, pool_scale, w_pool_out, w_ssm_out, w_o, final_g):
    depth = w_ada.shape[0]
    assert depth == 1, "kernel is written for a single layer"
    nb, seq, _ = x_prompt.shape
    ns, dec_seq, _ = x_sample.shape
    l = 0

    def head_lanes(v):
        return jnp.pad(jnp.tile(v, (1, 3)), ((0, 0), (0, LANES - 3 * N_HEADS)))

    c_all = jnp.concatenate([c_prompt, c_sample], axis=0)
    mod = pl.pallas_call(
        _adaln_kernel, grid=(3,),
        in_specs=[pl.BlockSpec((nb + ns, D_MODEL), lambda j: (0, 0)),
                  pl.BlockSpec((D_MODEL, D_MODEL), lambda j: (0, j)),
                  pl.BlockSpec((1, D_MODEL), lambda j: (0, j))],
        out_specs=pl.BlockSpec((nb + ns, D_MODEL), lambda j: (0, j)),
        out_shape=jax.ShapeDtypeStruct((nb + ns, 3 * D_MODEL), F32), name="adaln",
    )(c_all, w_ada[l], b_ada[l][None, :])
    mod = mod[:, None, :]

    e3 = _expand_matrix()
    w_main, w_tail = _repack_w_in(jnp.transpose(w_in[l]))

    def weights(ql):
        return [
            norm_g[l][None, :],
            w_main, w_tail,
            conv_w[l], conv_b[l][None, :],
            head_lanes(dt_bias[l][None, :]),
            head_lanes(a_log[l][None, :]),
            jnp.repeat(d_skip[l], HEAD_DIM)[None, :],
            ssm_norm_g[l][None, :],
            pool_w[l].astype(BF16), pool_scale[l][None, :],
            w_pool_out[l].astype(BF16), w_ssm_out[l].astype(BF16), w_o[l].astype(BF16),
            final_g[None, :],
            _block_sum_matrix(ql), e3,
        ]

    tile_rows = 2 * CHUNK
    y_p, pool_p, conv_p, ssm_p = _prompt_call(x_prompt, mod[:nb], weights(CHUNK), L=tile_rows)

    swap = lambda v: jnp.transpose(v, (1, 0, 2))
    states = (swap(state_pool[l]), swap(state_conv[l]),
              state_ssm[l].reshape(ns, D_INNER, D_STATE))
    y_s, pool_s, conv_s, ssm_s = _sample_call(
        x_sample, mod[nb:], states, weights(dec_seq), S=CHUNK // dec_seq, seq_per_inner=4)

    shp = (N_HEADS, HEAD_DIM, D_STATE)
    return (y_p, y_s, swap(pool_p)[None], swap(conv_p)[None], ssm_p.reshape((1, nb) + shp),
            swap(pool_s)[None], swap(conv_s)[None], ssm_s.reshape((1, ns) + shp))
```

```python
import functools
import types

import jax
import jax.numpy as jnp
from jax import lax
from jax.experimental import pallas as pl
from jax.experimental.pallas import tpu as pltpu

F32 = jnp.float32
BF16 = jnp.bfloat16

EPS = 1e-6
D_MODEL = 1024
POOL_WINDOWS = (2, 4, 8, 16)
POOL_GROUP = 128
D_POOL = POOL_GROUP * len(POOL_WINDOWS)
POOL_BUF = max(POOL_WINDOWS) - 1
D_INNER = 2048
HEAD_DIM = 64
N_HEADS = D_INNER // HEAD_DIM
N_BC_GROUPS = 4
HEADS_PER_GROUP = N_HEADS // N_BC_GROUPS
GROUP_COLS = HEADS_PER_GROUP * HEAD_DIM
D_STATE = 128
CONV_W = 4
D_XBC = D_INNER + 2 * N_BC_GROUPS * D_STATE
PAST_LEN = 16384

LANES = 128
CHUNK = 128
POOL_HDR = 16
CONV_HDR = 8
DOT_COLS = 512
ROW_BLOCK = 16
CONV_ROW_BLOCK = 32
VMEM_LIMIT_BYTES = 63 * 1024 * 1024

COL_UP = 0
COL_ZP = D_POOL
COL_ZS = 2 * D_POOL
COL_XBC = COL_ZS + D_INNER
COL_DT = COL_XBC + D_XBC
COL_G = COL_DT + LANES
N_WEIGHTS = 17
LOG2E = 1.4426950408889634


def _dot(a, b):
    return jnp.dot(a, b, preferred_element_type=F32)


def _dot_nt(a, b):
    return lax.dot_general(a, b, (((1,), (1,)), ((), ())), preferred_element_type=F32)


def _pack3(v, lane):
    hi = v.astype(BF16).astype(F32)
    r = v - hi
    mid = r.astype(BF16).astype(F32)
    lo = r - mid
    packed = jnp.where(lane < N_HEADS, hi,
                       jnp.where(lane < 2 * N_HEADS, mid,
                                 jnp.where(lane < 3 * N_HEADS, lo, 0.0)))
    return packed.astype(BF16)


def _silu(v):
    return v / (1.0 + jnp.exp2(v * -LOG2E))


def _sigmoid(v):
    return 1.0 / (1.0 + jnp.exp2(v * -LOG2E))


def _softplus(v):
    return jnp.maximum(v, 0.0) + jnp.log1p(jnp.exp(-jnp.abs(v)))


def _repack_main_kernel(x_ref, o_ref):
    o_ref[...] = x_ref[...].T.astype(o_ref.dtype)


def _repack_tail_kernel(dt_ref, g_ref, o_ref, *, g_row0):
    dt = dt_ref[...]
    heads = jnp.concatenate([dt, dt, dt, jnp.zeros((LANES - 3 * N_HEADS, D_MODEL), F32)], axis=0)
    o_ref[:, 0:LANES] = heads.T.astype(o_ref.dtype)
    for j in range(0, 2 * D_MODEL, DOT_COLS):
        o_ref[:, LANES + j:LANES + j + DOT_COLS] = (
            g_ref[g_row0 + j:g_row0 + j + DOT_COLS, :].T.astype(o_ref.dtype))


def _repack_w_in(w_in_t):
    n_in, d = w_in_t.shape
    assert d == D_MODEL and n_in == COL_DT + N_HEADS + 2 * D_MODEL
    w_main = pl.pallas_call(
        _repack_main_kernel, grid=(COL_DT // D_MODEL,),
        in_specs=[pl.BlockSpec((D_MODEL, D_MODEL), lambda j: (j, 0))],
        out_specs=pl.BlockSpec((D_MODEL, D_MODEL), lambda j: (0, j)),
        out_shape=jax.ShapeDtypeStruct((D_MODEL, COL_DT), BF16), name="repack_w_main",
    )(w_in_t)
    g_block = n_in // 4
    g_row = COL_DT + N_HEADS
    g_row0 = g_row % g_block
    assert n_in % g_block == 0 and g_block % 8 == 0 and g_row0 % 8 == 0
    assert g_row0 + 2 * D_MODEL <= g_block and COL_DT % N_HEADS == 0
    w_tail = pl.pallas_call(
        functools.partial(_repack_tail_kernel, g_row0=g_row0), grid=(1,),
        in_specs=[pl.BlockSpec((N_HEADS, D_MODEL), lambda i: (COL_DT // N_HEADS, 0)),
                  pl.BlockSpec((g_block, D_MODEL), lambda i: (g_row // g_block, 0))],
        out_specs=pl.BlockSpec((D_MODEL, LANES + 2 * D_MODEL), lambda i: (0, 0)),
        out_shape=jax.ShapeDtypeStruct((D_MODEL, LANES + 2 * D_MODEL), BF16), name="repack_w_tail",
        compiler_params=pltpu.CompilerParams(vmem_limit_bytes=VMEM_LIMIT_BYTES),
    )(w_in_t, w_in_t)
    return w_main, w_tail


def _adaln_kernel(c_ref, w_ref, b_ref, op_ref, os_ref):
    c = c_ref[...]
    mod = _dot(_silu(c).astype(BF16), w_ref[...].astype(BF16)) + b_ref[...]
    n_prompt = op_ref.shape[0]
    op_ref[:, 0, :] = mod[0:n_prompt, :]
    os_ref[:, 0, :] = mod[n_prompt:, :]


def _weight_refs(it):
    names = ("normg", "wmain", "wtail", "convw", "convb", "dtb", "alog", "dskip", "ssmg", "poolw",
             "pscale", "wpo", "wso", "wo", "fg", "ta", "e3")
    assert len(names) == N_WEIGHTS
    return types.SimpleNamespace(**{n: next(it) for n in names})


def _row_blocks(S, L, nrows):
    if L >= nrows:
        return [(slice(s, s + 1), slice(l0, l0 + nrows), slice(s * L + l0, s * L + l0 + nrows))
                for s in range(S) for l0 in range(0, L, nrows)]
    sb = nrows // L
    return [(slice(s0, s0 + sb), slice(0, L), slice(s0 * L, (s0 + sb) * L))
            for s0 in range(0, S, sb)]


def _norm_modulate(w, x_ref, mod_of, h_ref, S, L):
    for ss, ls, rs in _row_blocks(S, L, ROW_BLOCK):
        x3 = x_ref[ss, ls, :]
        mod3 = mod_of(ss)
        ms = jnp.mean(x3 * x3, axis=-1, keepdims=True)
        hn = x3 * lax.rsqrt(ms + EPS) * w.normg[...]
        hm = hn * (1.0 + mod3[:, :, D_MODEL:2 * D_MODEL]) + mod3[:, :, 0:D_MODEL]
        h_ref[rs, :] = hm.reshape(rs.stop - rs.start, D_MODEL).astype(BF16)


def _roll_rows(x, d, drop_first=False):
    n, rows, lanes = x.shape
    tiles = [x[:, j:j + 8, :] for j in range(0, rows, 8)]
    first = 1 if drop_first else 0
    if d == 8:
        out = [tiles[j - 1] for j in range(first, len(tiles))]
    else:
        rolled = [pltpu.roll(tl, d, 1) for tl in tiles]
        sub = lax.broadcasted_iota(jnp.int32, (n, 8, lanes), 1)
        out = [jnp.where(sub < d, rolled[j - 1], rolled[j]) for j in range(first, len(tiles))]
    return jnp.concatenate(out, axis=1)


def _proj_chunks(col0, ncols):
    return [(col0 + j, min(DOT_COLS, ncols - j), j) for j in range(0, ncols, DOT_COLS)]


def _pool_windows(up3, t, pos0, S, L):
    for ss, ls, rs in _row_blocks(S, L, ROW_BLOCK):
        nseq, ntok = ss.stop - ss.start, ls.stop - ls.start
        pos = pos0 + ls.start + lax.broadcasted_iota(jnp.int32, (nseq, ntok, POOL_GROUP), 1)
        for gi, win in enumerate(POOL_WINDOWS):
            sl = slice(gi * POOL_GROUP, (gi + 1) * POOL_GROUP)
            s = up3[ss, ls.start:POOL_HDR + ls.stop, sl]
            u = s[:, POOL_HDR:, :]
            shift = 1
            while shift < win:
                s = s + _roll_rows(s, shift)
                shift *= 2
            cnt = jnp.minimum(pos + 1, win).astype(F32)
            t.pm[rs, sl] = (s[:, POOL_HDR:, :] / cnt - u).reshape(
                nseq * ntok, POOL_GROUP).astype(BF16)


def _pool_mix(w, zp, t):
    for gi in range(len(POOL_WINDOWS)):
        sl = slice(gi * POOL_GROUP, (gi + 1) * POOL_GROUP)
        yg = _dot(t.pm[:, sl], w.poolw[gi]) * w.pscale[:, sl]
        t.pm[:, sl] = (yg * _silu(zp[:, sl])).astype(BF16)
    for j in range(0, D_MODEL, DOT_COLS):
        t.p[:, j:j + DOT_COLS] = _dot(t.pm[...], w.wpo[:, j:j + DOT_COLS])


def _conv_chunk(w, xbc3, t, j, S, L):
    sl = slice(j, j + DOT_COLS)
    tile_of = lambda row: jnp.broadcast_to(row[None], (1, 8, DOT_COLS))
    taps_w = [tile_of(w.convw[k:k + 1, sl]) for k in range(CONV_W)]
    bias = tile_of(w.convb[:, sl])
    for ss, ls, rs in _row_blocks(S, L, CONV_ROW_BLOCK):
        ext = xbc3[ss, ls.start:CONV_HDR + ls.stop, sl]
        taps = [ext[:, CONV_HDR:, :] if k == CONV_W - 1
                else _roll_rows(ext, CONV_W - 1 - k, drop_first=True) for k in range(CONV_W)]
        out_tiles = []
        for r in range(0, ls.stop - ls.start, 8):
            acc = bias
            for k in range(CONV_W):
                acc = acc + taps[k][:, r:r + 8, :] * taps_w[k]
            out_tiles.append(_silu(acc))
        v = jnp.concatenate(out_tiles, axis=1).reshape(rs.stop - rs.start, DOT_COLS)
        if j < D_INNER:
            t.xs[rs, sl] = v
        elif j < D_INNER + N_BC_GROUPS * D_STATE:
            t.b[rs, :] = v
        else:
            t.c[rs, :] = v


def _ssd_prep(w, t, dt_ref, c, prompt):
    rows = pl.ds(c * CHUNK, CHUNK)
    lane = lax.broadcasted_iota(jnp.int32, (CHUNK, LANES), 1)
    neg_a = jnp.where(lane[0:1, :] < 3 * N_HEADS, -jnp.exp(w.alog[...]), 0.0)
    dtc = dt_ref[rows, :]
    a = dtc * neg_a
    cs = _dot(w.ta[...], _pack3(a, lane))
    cs = cs + pltpu.roll(cs, 2 * N_HEADS, 1)
    cs = cs + pltpu.roll(cs, N_HEADS, 1)
    acs = cs[0:CHUNK, :]
    last = cs[CHUNK:2 * CHUNK, :]
    exp_acs = jnp.exp(acs)
    wgt = jnp.exp(last - acs) * dtc
    t.ea[rows, :] = _dot(_pack3(exp_acs, lane), w.e3[...])
    t.we[rows, :] = _dot(_pack3(wgt, lane), w.e3[...])
    if not prompt:
        t.aux[c] = jnp.exp(last).T
    acs2 = acs * LOG2E
    t.hd[c, 0] = acs2
    t.hd[c, 1] = (acs2 - jnp.log2(dtc)).T


def _ssd_intra(w, t, c, QL, prompt):
    row_start = c * CHUNK
    rows = pl.ds(row_start, CHUNK)
    lane = lax.broadcasted_iota(jnp.int32, (CHUNK, LANES), 1)
    row = lax.broadcasted_iota(jnp.int32, (CHUNK, LANES), 0)
    acs2, shifted_t = t.hd[c, 0], t.hd[c, 1]
    ql_bits = QL.bit_length() - 1
    mask = ((lax.shift_right_logical(row, ql_bits) == lax.shift_right_logical(lane, ql_bits))
            & (lane <= row))
    for g in range(N_BC_GROUPS):
        gs = slice(g * D_STATE, (g + 1) * D_STATE)
        gc = slice(g * GROUP_COLS, (g + 1) * GROUP_COLS)
        cg = t.c[rows, gs].astype(BF16)
        cb = _dot_nt(cg, t.b[rows, gs].astype(BF16))
        if prompt:
            ht = t.aux[:, gc]
            t.yoff[...] = _dot(cg, ht.astype(BF16))
            xw = (t.xs[rows, gc] * t.we[rows, gc]).astype(BF16)
            upd = _dot(t.b[rows, gs].T.astype(BF16), xw)
            t.aux[:, gc] = ht * t.ea[row_start + CHUNK - 1:row_start + CHUNK, gc] + upd
        for k in range(HEADS_PER_GROUP // 2):
            atts = []
            for h in (g * HEADS_PER_GROUP + 2 * k, g * HEADS_PER_GROUP + 2 * k + 1):
                seg2 = acs2[:, h:h + 1] - shifted_t[h:h + 1, :]
                decay_dt = jnp.exp2(jnp.where(mask, seg2, -jnp.inf))
                atts.append((cb * decay_dt).astype(BF16))
            ps = slice((g * HEADS_PER_GROUP + 2 * k) * HEAD_DIM,
                       (g * HEADS_PER_GROUP + 2 * k + 2) * HEAD_DIM)
            xp = t.xs[rows, ps]
            rhs = jnp.concatenate([jnp.where(lane < HEAD_DIM, xp, 0.0).astype(BF16),
                                   jnp.where(lane >= HEAD_DIM, xp, 0.0).astype(BF16)], axis=0)
            yp = _dot(jnp.concatenate(atts, axis=1), rhs) + w.dskip[:, ps] * xp
            if prompt:
                yp = yp + t.yoff[:, 2 * k * HEAD_DIM:(2 * k + 2) * HEAD_DIM] * t.ea[rows, ps]
            t.yssd[rows, ps] = yp


def _sample_state_pair(t, r, QL, hins, houts):
    n = 2 * QL
    c = r // CHUNK
    r0 = r % CHUNK
    rows = pl.ds(pl.multiple_of(r, n), n)
    lane = lax.broadcasted_iota(jnp.int32, (CHUNK, LANES), 1)
    rowi = lax.broadcasted_iota(jnp.int32, (n, GROUP_COLS), 0)
    total_t = t.aux[c]
    for q in range(2):
        col = jnp.sum(jnp.where(lane == r0 + q * QL, total_t, 0.0), axis=1, keepdims=True)
        rb = jnp.broadcast_to(col, (CHUNK, LANES))
        mine = (rowi >= q * QL) & (rowi < (q + 1) * QL)
        for g in range(N_BC_GROUPS):
            gs = slice(g * D_STATE, (g + 1) * D_STATE)
            cs = slice(g * GROUP_COLS, (g + 1) * GROUP_COLS)
            hg = hins[q][cs, :]
            yoff = _dot_nt(t.c[rows, gs].astype(BF16), hg.astype(BF16)) * t.ea[rows, cs]
            t.yssd[rows, cs] = t.yssd[rows, cs] + jnp.where(mine, yoff, 0.0)
            xw = jnp.where(mine, t.xs[rows, cs] * t.we[rows, cs], 0.0)
            upd = _dot(xw.T.astype(BF16), t.b[rows, gs].astype(BF16))
            for e in range(HEADS_PER_GROUP):
                h = g * HEADS_PER_GROUP + e
                hs = slice(h * HEAD_DIM, (h + 1) * HEAD_DIM)
                es = slice(e * HEAD_DIM, (e + 1) * HEAD_DIM)
                houts[q][hs, :] = hg[es, :] * rb[h:h + 1, :] + upd[es, :]


def _dense_out(w, t, x_ref, gate_of, y_ref, S, L):
    R = S * L
    col_chunks = [slice(j, j + DOT_COLS) for j in range(0, D_INNER, DOT_COLS)]
    for r in range(0, R, ROW_BLOCK):
        rs = slice(r, r + ROW_BLOCK)
        ys = [t.yssd[rs, sl] * _silu(t.zs[rs, sl]) for sl in col_chunks]
        ssq = sum(jnp.sum(y * y, axis=-1, keepdims=True) for y in ys)
        inv = lax.rsqrt(ssq * (1.0 / D_INNER) + EPS)
        for y, sl in zip(ys, col_chunks):
            t.yn[rs, sl] = (y * inv * w.ssmg[:, sl]).astype(BF16)
    for j in range(0, D_MODEL, DOT_COLS):
        t.s[:, j:j + DOT_COLS] = _dot(t.yn[...], w.wso[:, j:j + DOT_COLS])
    for r in range(0, R, ROW_BLOCK):
        rs = slice(r, r + ROW_BLOCK)
        for j in range(0, D_MODEL, DOT_COLS):
            sl = slice(j, j + DOT_COLS)
            sl2 = slice(D_MODEL + j, D_MODEL + j + DOT_COLS)
            t.yn[rs, sl] = (_sigmoid(t.g[rs, sl]) * t.p[rs, sl]
                            + _sigmoid(t.g[rs, sl2]) * t.s[rs, sl]).astype(BF16)
    for j in range(0, D_MODEL, DOT_COLS):
        t.s[:, j:j + DOT_COLS] = _dot(t.yn[:, 0:D_MODEL], w.wo[:, j:j + DOT_COLS])
    for ss, ls, rs in _row_blocks(S, L, ROW_BLOCK):
        o3 = t.s[rs, :].reshape(ss.stop - ss.start, ls.stop - ls.start, D_MODEL)
        xn = x_ref[ss, ls, :] + gate_of(ss) * o3
        ms = jnp.mean(xn * xn, axis=-1, keepdims=True)
        y_ref[ss, ls, :] = xn * lax.rsqrt(ms + EPS) * w.fg[...]


def _ssd_refs(it):
    names = ("xs", "b", "c", "yssd", "pm", "p", "ea", "we", "hd", "aux", "yn", "s", "zs", "g")
    return types.SimpleNamespace(**{n: next(it) for n in names})


def _input_projection(w, h_ref, stores):
    for col0, (ncols, store) in stores.items():
        for col, wd, j in _proj_chunks(col0, ncols):
            if col < COL_DT:
                wc = w.wmain[:, col:col + wd]
            else:
                wc = w.wtail[:, col - COL_DT:col - COL_DT + wd]
            store(j, wd, _dot(h_ref[...], wc))


def _prompt_kernel(*refs, L):
    S = 1
    it = iter(refs)
    x_ref, mod_ref = next(it), next(it)
    w = _weight_refs(it)
    y_ref, pool_out_ref, conv_out_ref, ssm_out_ref = [next(it) for _ in range(4)]
    h_scr, up_scr, zp_scr, xbc_scr, dt_scr = [next(it) for _ in range(5)]
    t = _ssd_refs(it)
    t.yoff = next(it)
    step = pl.program_id(1)

    @pl.when(step == 0)
    def _():
        up_scr[:, 0:POOL_HDR, :] = jnp.zeros((S, POOL_HDR, D_POOL), F32)
        xbc_scr[:, 0:CONV_HDR, :] = jnp.zeros((S, CONV_HDR, D_XBC), F32)
        t.aux[...] = jnp.zeros(t.aux.shape, F32)

    _norm_modulate(w, x_ref, lambda ss: mod_ref[...], h_scr, S, L)

    def store_up(j, wd, v):
        up_scr[:, POOL_HDR:POOL_HDR + L, j:j + wd] = v.reshape(S, L, wd)

    def store_zp(j, wd, v):
        zp_scr[:, j:j + wd] = v

    def store_zs(j, wd, v):
        t.zs[:, j:j + wd] = v

    def store_xbc(j, wd, v):
        xbc_scr[:, CONV_HDR:CONV_HDR + L, j:j + wd] = v.reshape(S, L, wd)

    def store_dt(j, wd, v):
        dt_scr[...] = _softplus(v + w.dtb[...])

    def store_g(j, wd, v):
        t.g[:, j:j + wd] = v

    _input_projection(w, h_scr, {
        COL_UP: (D_POOL, store_up), COL_ZP: (D_POOL, store_zp), COL_DT: (LANES, store_dt),
        COL_XBC: (D_XBC, store_xbc), COL_ZS: (D_INNER, store_zs), COL_G: (2 * D_MODEL, store_g)})

    _pool_windows(up_scr, t, step * L, S, L)
    _pool_mix(w, zp_scr, t)
    up_scr[:, 0:POOL_HDR, :] = up_scr[:, L:L + POOL_HDR, :]

    for j in range(0, D_XBC, DOT_COLS):
        _conv_chunk(w, xbc_scr, t, j, S, L)
    xbc_scr[:, 0:CONV_HDR, :] = xbc_scr[:, L:L + CONV_HDR, :]

    for c in range(L // CHUNK):
        _ssd_prep(w, t, dt_scr, c, True)
    for c in range(L // CHUNK):
        _ssd_intra(w, t, c, CHUNK, True)

    _dense_out(w, t, x_ref, lambda ss: mod_ref[...][:, :, 2 * D_MODEL:3 * D_MODEL], y_ref, S, L)

    @pl.when(step == pl.num_programs(1) - 1)
    def _():
        for n in range(D_INNER // LANES):
            cols = slice(n * LANES, (n + 1) * LANES)
            ssm_out_ref[0, cols, :] = t.aux[:, cols].T
        batch = pl.ds(pl.program_id(0), 1)
        for r in range(POOL_BUF):
            pool_out_ref[r, batch, :] = up_scr[0, 1 + r:2 + r, :]
        for r in range(CONV_W - 1):
            r0 = CONV_HDR - (CONV_W - 1) + r
            conv_out_ref[r, batch, :] = xbc_scr[0, r0:r0 + 1, :]


def _sample_kernel(*refs, S, L, n_inner, seq_per_inner):
    R = S * L
    it = iter(refs)
    x_ref, mod_ref, pool_in_ref, conv_in_ref, ssm_in_ref = [next(it) for _ in range(5)]
    w = _weight_refs(it)
    y_ref, pool_out_ref, conv_out_ref, ssm_out_ref = [next(it) for _ in range(4)]
    h_scr, up_scr, zp_scr, xbc_scr, dt_scr = [next(it) for _ in range(5)]
    t = _ssd_refs(it)
    step = pl.program_id(1)

    @pl.when(step == 0)
    def _():
        _norm_modulate(w, x_ref, lambda ss: mod_ref[ss], h_scr, S, L)

        def store_up(j, wd, v):
            up_scr[:, POOL_HDR:POOL_HDR + L, j:j + wd] = v.reshape(S, L, wd)

        def store_zp(j, wd, v):
            zp_scr[:, j:j + wd] = v

        def store_zs(j, wd, v):
            t.zs[:, j:j + wd] = v

        def store_xbc(j, wd, v):
            xbc_scr[:, CONV_HDR:CONV_HDR + L, j:j + wd] = v.reshape(S, L, wd)

        def store_dt(j, wd, v):
            dt_scr[...] = _softplus(v + w.dtb[...])

        def store_g(j, wd, v):
            t.g[:, j:j + wd] = v

        _input_projection(w, h_scr, {
            COL_UP: (D_POOL, store_up), COL_ZP: (D_POOL, store_zp), COL_DT: (LANES, store_dt),
            COL_XBC: (D_XBC, store_xbc), COL_ZS: (D_INNER, store_zs),
            COL_G: (2 * D_MODEL, store_g)})

        for r in range(POOL_BUF):
            up_scr[:, 1 + r, :] = pool_in_ref[r]
        _pool_windows(up_scr, t, PAST_LEN, S, L)
        _pool_mix(w, zp_scr, t)
        for r in range(POOL_BUF):
            pool_out_ref[r] = up_scr[:, L + 1 + r, :]

        for r in range(CONV_W - 1):
            xbc_scr[:, CONV_HDR - (CONV_W - 1) + r, :] = conv_in_ref[r]
        for j in range(0, D_XBC, DOT_COLS):
            _conv_chunk(w, xbc_scr, t, j, S, L)
        for r in range(CONV_W - 1):
            conv_out_ref[r] = xbc_scr[:, L + CONV_HDR - (CONV_W - 1) + r, :]
        for c in range(R // CHUNK):
            _ssd_prep(w, t, dt_scr, c, False)
        for c in range(R // CHUNK):
            _ssd_intra(w, t, c, L, False)

    for q in range(0, seq_per_inner, 2):
        _sample_state_pair(t, (step * seq_per_inner + q) * L, L,
                           (ssm_in_ref.at[q], ssm_in_ref.at[q + 1]),
                           (ssm_out_ref.at[q], ssm_out_ref.at[q + 1]))

    @pl.when(step == n_inner - 1)
    def _():
        _dense_out(w, t, x_ref, lambda ss: mod_ref[ss][:, :, 2 * D_MODEL:3 * D_MODEL],
                   y_ref, S, L)


def _resident(shape):
    zeros = (0,) * len(shape)
    return pl.BlockSpec(shape, lambda *_: zeros, pipeline_mode=pl.Buffered(1))


def _ssd_scratch(R, aux_shape):
    return [
        pltpu.VMEM((R, D_INNER), F32),
        pltpu.VMEM((R, N_BC_GROUPS * D_STATE), F32),
        pltpu.VMEM((R, N_BC_GROUPS * D_STATE), F32),
        pltpu.VMEM((R, D_INNER), F32),
        pltpu.VMEM((R, D_POOL), BF16),
        pltpu.VMEM((R, D_MODEL), F32),
        pltpu.VMEM((R, D_INNER), F32),
        pltpu.VMEM((R, D_INNER), F32),
        pltpu.VMEM((R // CHUNK, 2, CHUNK, LANES), F32),
        pltpu.VMEM(aux_shape, F32),
        pltpu.VMEM((R, D_INNER), BF16),
        pltpu.VMEM((R, D_MODEL), F32),
        pltpu.VMEM((R, D_INNER), F32),
        pltpu.VMEM((R, 2 * D_MODEL), F32),
    ]


def _stage_scratch(S, L):
    return [
        pltpu.VMEM((S * L, D_MODEL), BF16),
        pltpu.VMEM((S, POOL_HDR + L, D_POOL), F32),
        pltpu.VMEM((S * L, D_POOL), F32),
        pltpu.VMEM((S, CONV_HDR + L, D_XBC), F32),
        pltpu.VMEM((S * L, LANES), F32),
    ]


def _prompt_call(x, mod, weights, *, L):
    nb, seq, _ = x.shape
    x_spec = pl.BlockSpec((1, L, D_MODEL), lambda b, t: (b, t, 0))
    seq_map = lambda b, t: (b, 0, 0)
    in_specs = ([x_spec, pl.BlockSpec((1, 1, 3 * D_MODEL), seq_map)]
                + [_resident(v.shape) for v in weights])
    whole = lambda b, t: (0, 0, 0)
    out_shape = (jax.ShapeDtypeStruct(x.shape, F32),
                 jax.ShapeDtypeStruct((POOL_BUF, nb, D_POOL), F32),
                 jax.ShapeDtypeStruct((CONV_W - 1, nb, D_XBC), F32),
                 jax.ShapeDtypeStruct((nb, D_INNER, D_STATE), F32))
    out_specs = (x_spec,
                 pl.BlockSpec((POOL_BUF, nb, D_POOL), whole),
                 pl.BlockSpec((CONV_W - 1, nb, D_XBC), whole),
                 pl.BlockSpec((1, D_INNER, D_STATE), seq_map))
    scratch = (_stage_scratch(1, L) + _ssd_scratch(L, (D_STATE, D_INNER))
               + [pltpu.VMEM((CHUNK, GROUP_COLS), F32)])
    kern = functools.partial(_prompt_kernel, L=L)
    return pl.pallas_call(
        kern, grid=(nb, seq // L), in_specs=in_specs, out_specs=out_specs, out_shape=out_shape,
        scratch_shapes=scratch, name="layer_prompt",
        compiler_params=pltpu.CompilerParams(
            dimension_semantics=("arbitrary", "arbitrary"), vmem_limit_bytes=VMEM_LIMIT_BYTES),
    )(x, mod, *weights)


def _sample_call(x, mod, states, weights, *, S, seq_per_inner):
    ns, L, _ = x.shape
    R = S * L
    n_inner = S // seq_per_inner
    seq_map = lambda o, i: (o, 0, 0)
    st_map = lambda o, i: (o * n_inner + i, 0, 0)
    time_major = lambda o, i: (0, o, 0)
    x_spec = pl.BlockSpec((S, L, D_MODEL), seq_map)
    pool_spec = pl.BlockSpec((POOL_BUF, S, D_POOL), time_major)
    conv_spec = pl.BlockSpec((CONV_W - 1, S, D_XBC), time_major)
    st_spec = pl.BlockSpec((seq_per_inner, D_INNER, D_STATE), st_map)
    in_specs = ([x_spec, pl.BlockSpec((S, 1, 3 * D_MODEL), seq_map), pool_spec, conv_spec, st_spec]
                + [_resident(v.shape) for v in weights])
    out_shape = (jax.ShapeDtypeStruct(x.shape, F32),
                 jax.ShapeDtypeStruct((POOL_BUF, ns, D_POOL), F32),
                 jax.ShapeDtypeStruct((CONV_W - 1, ns, D_XBC), F32),
                 jax.ShapeDtypeStruct((ns, D_INNER, D_STATE), F32))
    out_specs = (x_spec, pool_spec, conv_spec, st_spec)
    scratch = _stage_scratch(S, L) + _ssd_scratch(R, (R // CHUNK, LANES, CHUNK))
    kern = functools.partial(_sample_kernel, S=S, L=L, n_inner=n_inner,
                             seq_per_inner=seq_per_inner)
    return pl.pallas_call(
        kern, grid=(ns // S, n_inner), in_specs=in_specs, out_specs=out_specs,
        out_shape=out_shape, scratch_shapes=scratch, name="layer_sample",
        compiler_params=pltpu.CompilerParams(
            dimension_semantics=("arbitrary", "arbitrary"), vmem_limit_bytes=VMEM_LIMIT_BYTES),
    )(x, mod, *states, *weights)


def _block_sum_matrix(ql):
    r = jnp.arange(CHUNK)[:, None]
    c = jnp.arange(CHUNK)[None, :]
    same = (r // ql) == (c // ql)
    return jnp.concatenate([same & (c <= r), same], axis=0).astype(BF16)


def _expand_matrix():
    k = jnp.arange(LANES)[:, None]
    c = jnp.arange(D_INNER)[None, :] // HEAD_DIM
    return ((k % N_HEADS == c) & (k < 3 * N_HEADS)).astype(BF16)


def kernel(x_prompt, x_sample, state_pool, state_conv, state_ssm, c_prompt, c_sample, w_ada, b_ada, norm_g, w_in, conv_w, conv_b, dt_bias, a_log, d_skip, ssm_norm_g, pool_w, pool_scale, w_pool_out, w_ssm_out, w_o, final_g):
    depth = w_ada.shape[0]
    assert depth == 1, "kernel is written for a single layer"
    nb, seq, _ = x_prompt.shape
    ns, dec_seq, _ = x_sample.shape
    l = 0

    def head_lanes(v):
        return jnp.pad(jnp.tile(v, (1, 3)), ((0, 0), (0, LANES - 3 * N_HEADS)))

    c_all = jnp.concatenate([c_prompt, c_sample], axis=0)
    mod_p, mod_s = pl.pallas_call(
        _adaln_kernel, grid=(3,),
        in_specs=[pl.BlockSpec((nb + ns, D_MODEL), lambda j: (0, 0)),
                  pl.BlockSpec((D_MODEL, D_MODEL), lambda j: (0, j)),
                  pl.BlockSpec((1, D_MODEL), lambda j: (0, j))],
        out_specs=(pl.BlockSpec((nb, 1, D_MODEL), lambda j: (0, 0, j)),
                   pl.BlockSpec((ns, 1, D_MODEL), lambda j: (0, 0, j))),
        out_shape=(jax.ShapeDtypeStruct((nb, 1, 3 * D_MODEL), F32),
                   jax.ShapeDtypeStruct((ns, 1, 3 * D_MODEL), F32)), name="adaln",
    )(c_all, w_ada[l], b_ada[l][None, :])

    e3 = _expand_matrix()
    w_main, w_tail = _repack_w_in(jnp.transpose(w_in[l]))

    def weights(ql):
        return [
            norm_g[l][None, :],
            w_main, w_tail,
            conv_w[l], conv_b[l][None, :],
            head_lanes(dt_bias[l][None, :]),
            head_lanes(a_log[l][None, :]),
            jnp.repeat(d_skip[l], HEAD_DIM)[None, :],
            ssm_norm_g[l][None, :],
            pool_w[l].astype(BF16), pool_scale[l][None, :],
            w_pool_out[l].astype(BF16), w_ssm_out[l].astype(BF16), w_o[l].astype(BF16),
            final_g[None, :],
            _block_sum_matrix(ql), e3,
        ]

    tile_rows = 2 * CHUNK
    y_p, pool_p, conv_p, ssm_p = _prompt_call(x_prompt, mod_p, weights(CHUNK), L=tile_rows)

    swap = lambda v: jnp.transpose(v, (1, 0, 2))
    states = (swap(state_pool[l]), swap(state_conv[l]),
              state_ssm[l].reshape(ns, D_INNER, D_STATE))
    y_s, pool_s, conv_s, ssm_s = _sample_call(
        x_sample, mod_s, states, weights(dec_seq), S=CHUNK // dec_seq, seq_per_inner=4)

    shp = (N_HEADS, HEAD_DIM, D_STATE)
    return (y_p, y_s, swap(pool_p)[None], swap(conv_p)[None], ssm_p.reshape((1, nb) + shp),
            swap(pool_s)[None], swap(conv_s)[None], ssm_s.reshape((1, ns) + shp))
```

```python
import functools
import types

import jax
import jax.numpy as jnp
from jax import lax
from jax.experimental import pallas as pl
from jax.experimental.pallas import tpu as pltpu

F32 = jnp.float32
BF16 = jnp.bfloat16

EPS = 1e-6
D_MODEL = 1024
POOL_WINDOWS = (2, 4, 8, 16)
POOL_GROUP = 128
D_POOL = POOL_GROUP * len(POOL_WINDOWS)
POOL_BUF = max(POOL_WINDOWS) - 1
D_INNER = 2048
HEAD_DIM = 64
N_HEADS = D_INNER // HEAD_DIM
N_BC_GROUPS = 4
HEADS_PER_GROUP = N_HEADS // N_BC_GROUPS
GROUP_COLS = HEADS_PER_GROUP * HEAD_DIM
D_STATE = 128
CONV_W = 4
D_XBC = D_INNER + 2 * N_BC_GROUPS * D_STATE
PAST_LEN = 16384

LANES = 128
CHUNK = 128
POOL_HDR = 16
CONV_HDR = 8
DOT_COLS = 512
ROW_BLOCK = 16
CONV_ROW_BLOCK = 32
VMEM_LIMIT_BYTES = 63 * 1024 * 1024

COL_UP = 0
COL_ZP = D_POOL
COL_ZS = 2 * D_POOL
COL_XBC = COL_ZS + D_INNER
COL_DT = COL_XBC + D_XBC
COL_G = COL_DT + LANES
N_WEIGHTS = 17
LOG2E = 1.4426950408889634


def _dot(a, b):
    return jnp.dot(a, b, preferred_element_type=F32)


def _dot_nt(a, b):
    return lax.dot_general(a, b, (((1,), (1,)), ((), ())), preferred_element_type=F32)


def _pack3(v, lane):
    hi = v.astype(BF16).astype(F32)
    r = v - hi
    mid = r.astype(BF16).astype(F32)
    lo = r - mid
    packed = jnp.where(lane < N_HEADS, hi,
                       jnp.where(lane < 2 * N_HEADS, mid,
                                 jnp.where(lane < 3 * N_HEADS, lo, 0.0)))
    return packed.astype(BF16)


def _silu(v):
    return v / (1.0 + jnp.exp2(v * -LOG2E))


def _sigmoid(v):
    return 1.0 / (1.0 + jnp.exp2(v * -LOG2E))


def _softplus(v):
    return jnp.maximum(v, 0.0) + jnp.log1p(jnp.exp(-jnp.abs(v)))


def _repack_main_kernel(x_ref, o_ref):
    o_ref[...] = x_ref[...].T.astype(o_ref.dtype)


def _repack_tail_kernel(dt_ref, g_ref, o_ref, *, g_row0):
    dt = dt_ref[...]
    heads = jnp.concatenate([dt, dt, dt, jnp.zeros((LANES - 3 * N_HEADS, D_MODEL), F32)], axis=0)
    o_ref[:, 0:LANES] = heads.T.astype(o_ref.dtype)
    for j in range(0, 2 * D_MODEL, DOT_COLS):
        o_ref[:, LANES + j:LANES + j + DOT_COLS] = (
            g_ref[g_row0 + j:g_row0 + j + DOT_COLS, :].T.astype(o_ref.dtype))


def _repack_w_in(w_in_t):
    n_in, d = w_in_t.shape
    assert d == D_MODEL and n_in == COL_DT + N_HEADS + 2 * D_MODEL
    w_main = pl.pallas_call(
        _repack_main_kernel, grid=(COL_DT // D_MODEL,),
        in_specs=[pl.BlockSpec((D_MODEL, D_MODEL), lambda j: (j, 0))],
        out_specs=pl.BlockSpec((D_MODEL, D_MODEL), lambda j: (0, j)),
        out_shape=jax.ShapeDtypeStruct((D_MODEL, COL_DT), BF16), name="repack_w_main",
    )(w_in_t)
    g_block = n_in // 4
    g_row = COL_DT + N_HEADS
    g_row0 = g_row % g_block
    assert n_in % g_block == 0 and g_block % 8 == 0 and g_row0 % 8 == 0
    assert g_row0 + 2 * D_MODEL <= g_block and COL_DT % N_HEADS == 0
    w_tail = pl.pallas_call(
        functools.partial(_repack_tail_kernel, g_row0=g_row0), grid=(1,),
        in_specs=[pl.BlockSpec((N_HEADS, D_MODEL), lambda i: (COL_DT // N_HEADS, 0)),
                  pl.BlockSpec((g_block, D_MODEL), lambda i: (g_row // g_block, 0))],
        out_specs=pl.BlockSpec((D_MODEL, LANES + 2 * D_MODEL), lambda i: (0, 0)),
        out_shape=jax.ShapeDtypeStruct((D_MODEL, LANES + 2 * D_MODEL), BF16), name="repack_w_tail",
        compiler_params=pltpu.CompilerParams(vmem_limit_bytes=VMEM_LIMIT_BYTES),
    )(w_in_t, w_in_t)
    return w_main, w_tail


def _adaln_kernel(c_ref, w_ref, b_ref, op_ref, os_ref):
    c = c_ref[...]
    mod = _dot(_silu(c).astype(BF16), w_ref[...].astype(BF16)) + b_ref[...]
    n_prompt = op_ref.shape[0]
    op_ref[:, 0, :] = mod[0:n_prompt, :]
    os_ref[:, 0, :] = mod[n_prompt:, :]


def _weight_refs(it):
    names = ("normg", "wmain", "wtail", "convw", "convb", "dtb", "alog", "dskip", "ssmg", "poolw",
             "pscale", "wpo", "wso", "wo", "fg", "ta", "e3")
    assert len(names) == N_WEIGHTS
    return types.SimpleNamespace(**{n: next(it) for n in names})


def _row_blocks(S, L, nrows):
    if L >= nrows:
        return [(slice(s, s + 1), slice(l0, l0 + nrows), slice(s * L + l0, s * L + l0 + nrows))
                for s in range(S) for l0 in range(0, L, nrows)]
    sb = nrows // L
    return [(slice(s0, s0 + sb), slice(0, L), slice(s0 * L, (s0 + sb) * L))
            for s0 in range(0, S, sb)]


def _norm_modulate(w, x_ref, mod_of, h_ref, S, L):
    for ss, ls, rs in _row_blocks(S, L, ROW_BLOCK):
        x3 = x_ref[ss, ls, :]
        mod3 = mod_of(ss)
        ms = jnp.mean(x3 * x3, axis=-1, keepdims=True)
        hn = x3 * lax.rsqrt(ms + EPS) * w.normg[...]
        hm = hn * (1.0 + mod3[:, :, D_MODEL:2 * D_MODEL]) + mod3[:, :, 0:D_MODEL]
        h_ref[rs, :] = hm.reshape(rs.stop - rs.start, D_MODEL).astype(BF16)


def _roll_rows(x, d, drop_first=False):
    n, rows, lanes = x.shape
    tiles = [x[:, j:j + 8, :] for j in range(0, rows, 8)]
    first = 1 if drop_first else 0
    if d == 8:
        out = [tiles[j - 1] for j in range(first, len(tiles))]
    else:
        rolled = [pltpu.roll(tl, d, 1) for tl in tiles]
        sub = lax.broadcasted_iota(jnp.int32, (n, 8, lanes), 1)
        out = [jnp.where(sub < d, rolled[j - 1], rolled[j]) for j in range(first, len(tiles))]
    return jnp.concatenate(out, axis=1)


def _proj_chunks(col0, ncols):
    return [(col0 + j, min(DOT_COLS, ncols - j), j) for j in range(0, ncols, DOT_COLS)]


def _pool_windows(up3, t, pos0, S, L):
    for ss, ls, rs in _row_blocks(S, L, ROW_BLOCK):
        nseq, ntok = ss.stop - ss.start, ls.stop - ls.start
        pos = pos0 + ls.start + lax.broadcasted_iota(jnp.int32, (nseq, ntok, POOL_GROUP), 1)
        for gi, win in enumerate(POOL_WINDOWS):
            sl = slice(gi * POOL_GROUP, (gi + 1) * POOL_GROUP)
            s = up3[ss, ls.start:POOL_HDR + ls.stop, sl]
            u = s[:, POOL_HDR:, :]
            shift = 1
            while shift < win:
                s = s + _roll_rows(s, shift)
                shift *= 2
            cnt = jnp.minimum(pos + 1, win).astype(F32)
            t.pm[rs, sl] = (s[:, POOL_HDR:, :] / cnt - u).reshape(
                nseq * ntok, POOL_GROUP).astype(BF16)


def _pool_mix(w, zp, t):
    for gi in range(len(POOL_WINDOWS)):
        sl = slice(gi * POOL_GROUP, (gi + 1) * POOL_GROUP)
        yg = _dot(t.pm[:, sl], w.poolw[gi]) * w.pscale[:, sl]
        t.pm[:, sl] = (yg * _silu(zp[:, sl])).astype(BF16)
    for j in range(0, D_MODEL, DOT_COLS):
        t.p[:, j:j + DOT_COLS] = _dot(t.pm[...], w.wpo[:, j:j + DOT_COLS])


def _conv_chunk(w, xbc3, t, j, S, L):
    sl = slice(j, j + DOT_COLS)
    tile_of = lambda row: jnp.broadcast_to(row[None], (1, 8, DOT_COLS))
    taps_w = [tile_of(w.convw[k:k + 1, sl]) for k in range(CONV_W)]
    bias = tile_of(w.convb[:, sl])
    for ss, ls, rs in _row_blocks(S, L, CONV_ROW_BLOCK):
        ext = xbc3[ss, ls.start:CONV_HDR + ls.stop, sl]
        taps = [ext[:, CONV_HDR:, :] if k == CONV_W - 1
                else _roll_rows(ext, CONV_W - 1 - k, drop_first=True) for k in range(CONV_W)]
        out_tiles = []
        for r in range(0, ls.stop - ls.start, 8):
            acc = bias
            for k in range(CONV_W):
                acc = acc + taps[k][:, r:r + 8, :] * taps_w[k]
            out_tiles.append(_silu(acc))
        v = jnp.concatenate(out_tiles, axis=1).reshape(rs.stop - rs.start, DOT_COLS)
        if j < D_INNER:
            t.xs[rs, sl] = v
        elif j < D_INNER + N_BC_GROUPS * D_STATE:
            t.b[rs, :] = v
        else:
            t.c[rs, :] = v


def _ssd_prep(w, t, dt_ref, c, prompt):
    rows = pl.ds(c * CHUNK, CHUNK)
    lane = lax.broadcasted_iota(jnp.int32, (CHUNK, LANES), 1)
    neg_a = jnp.where(lane[0:1, :] < 3 * N_HEADS, -jnp.exp(w.alog[...]), 0.0)
    dtc = dt_ref[rows, :]
    a = dtc * neg_a
    cs = _dot(w.ta[...], _pack3(a, lane))
    cs = cs + pltpu.roll(cs, 2 * N_HEADS, 1)
    cs = cs + pltpu.roll(cs, N_HEADS, 1)
    acs = cs[0:CHUNK, :]
    last = cs[CHUNK:2 * CHUNK, :]
    exp_acs = jnp.exp(acs)
    wgt = jnp.exp(last - acs) * dtc
    t.ea[rows, :] = _dot(_pack3(exp_acs, lane), w.e3[...])
    t.we[rows, :] = _dot(_pack3(wgt, lane), w.e3[...])
    if not prompt:
        t.aux[c] = jnp.exp(last).T
    acs2 = acs * LOG2E
    t.hd[c, 0] = acs2
    t.hd[c, 1] = (acs2 - jnp.log2(dtc)).T


def _ssd_intra(w, t, c, QL, prompt):
    row_start = c * CHUNK
    rows = pl.ds(row_start, CHUNK)
    lane = lax.broadcasted_iota(jnp.int32, (CHUNK, LANES), 1)
    row = lax.broadcasted_iota(jnp.int32, (CHUNK, LANES), 0)
    acs2, shifted_t = t.hd[c, 0], t.hd[c, 1]
    ql_bits = QL.bit_length() - 1
    mask = ((lax.shift_right_logical(row, ql_bits) == lax.shift_right_logical(lane, ql_bits))
            & (lane <= row))
    for g in range(N_BC_GROUPS):
        gs = slice(g * D_STATE, (g + 1) * D_STATE)
        gc = slice(g * GROUP_COLS, (g + 1) * GROUP_COLS)
        cg = t.c[rows, gs].astype(BF16)
        cb = _dot_nt(cg, t.b[rows, gs].astype(BF16))
        if prompt:
            ht = t.aux[:, gc]
            t.yoff[...] = _dot(cg, ht.astype(BF16))
            xw = (t.xs[rows, gc] * t.we[rows, gc]).astype(BF16)
            upd = _dot(t.b[rows, gs].T.astype(BF16), xw)
            t.aux[:, gc] = ht * t.ea[row_start + CHUNK - 1:row_start + CHUNK, gc] + upd
        for k in range(HEADS_PER_GROUP // 2):
            atts = []
            for h in (g * HEADS_PER_GROUP + 2 * k, g * HEADS_PER_GROUP + 2 * k + 1):
                seg2 = acs2[:, h:h + 1] - shifted_t[h:h + 1, :]
                decay_dt = jnp.exp2(jnp.where(mask, seg2, -jnp.inf))
                atts.append((cb * decay_dt).astype(BF16))
            ps = slice((g * HEADS_PER_GROUP + 2 * k) * HEAD_DIM,
                       (g * HEADS_PER_GROUP + 2 * k + 2) * HEAD_DIM)
            xp = t.xs[rows, ps]
            rhs = jnp.concatenate([jnp.where(lane < HEAD_DIM, xp, 0.0).astype(BF16),
                                   jnp.where(lane >= HEAD_DIM, xp, 0.0).astype(BF16)], axis=0)
            yp = _dot(jnp.concatenate(atts, axis=1), rhs) + w.dskip[:, ps] * xp
            if prompt:
                yp = yp + t.yoff[:, 2 * k * HEAD_DIM:(2 * k + 2) * HEAD_DIM] * t.ea[rows, ps]
            t.yssd[rows, ps] = yp


def _sample_state_pair(t, r, QL, hins, houts):
    n = 2 * QL
    c = r // CHUNK
    r0 = r % CHUNK
    rows = pl.ds(pl.multiple_of(r, n), n)
    lane = lax.broadcasted_iota(jnp.int32, (CHUNK, LANES), 1)
    rowi = lax.broadcasted_iota(jnp.int32, (n, GROUP_COLS), 0)
    total_t = t.aux[c]
    for q in range(2):
        col = jnp.sum(jnp.where(lane == r0 + q * QL, total_t, 0.0), axis=1, keepdims=True)
        rb = jnp.broadcast_to(col, (CHUNK, LANES))
        mine = (rowi >= q * QL) & (rowi < (q + 1) * QL)
        for g in range(N_BC_GROUPS):
            gs = slice(g * D_STATE, (g + 1) * D_STATE)
            cs = slice(g * GROUP_COLS, (g + 1) * GROUP_COLS)
            hg = hins[q][cs, :]
            yoff = _dot_nt(t.c[rows, gs].astype(BF16), hg.astype(BF16)) * t.ea[rows, cs]
            t.yssd[rows, cs] = t.yssd[rows, cs] + jnp.where(mine, yoff, 0.0)
            xw = jnp.where(mine, t.xs[rows, cs] * t.we[rows, cs], 0.0)
            upd = _dot(xw.T.astype(BF16), t.b[rows, gs].astype(BF16))
            for e in range(HEADS_PER_GROUP):
                h = g * HEADS_PER_GROUP + e
                hs = slice(h * HEAD_DIM, (h + 1) * HEAD_DIM)
                es = slice(e * HEAD_DIM, (e + 1) * HEAD_DIM)
                houts[q][hs, :] = hg[es, :] * rb[h:h + 1, :] + upd[es, :]


def _dense_out(w, t, x_ref, gate_of, y_ref, S, L):
    R = S * L
    col_chunks = [slice(j, j + DOT_COLS) for j in range(0, D_INNER, DOT_COLS)]
    for r in range(0, R, ROW_BLOCK):
        rs = slice(r, r + ROW_BLOCK)
        ys = [t.yssd[rs, sl] * _silu(t.zs[rs, sl]) for sl in col_chunks]
        ssq = sum(jnp.sum(y * y, axis=-1, keepdims=True) for y in ys)
        inv = lax.rsqrt(ssq * (1.0 / D_INNER) + EPS)
        for y, sl in zip(ys, col_chunks):
            t.yn[rs, sl] = (y * inv * w.ssmg[:, sl]).astype(BF16)
    for j in range(0, D_MODEL, DOT_COLS):
        t.s[:, j:j + DOT_COLS] = _dot(t.yn[...], w.wso[:, j:j + DOT_COLS])
    for r in range(0, R, ROW_BLOCK):
        rs = slice(r, r + ROW_BLOCK)
        for j in range(0, D_MODEL, DOT_COLS):
            sl = slice(j, j + DOT_COLS)
            sl2 = slice(D_MODEL + j, D_MODEL + j + DOT_COLS)
            t.yn[rs, sl] = (_sigmoid(t.g[rs, sl]) * t.p[rs, sl]
                            + _sigmoid(t.g[rs, sl2]) * t.s[rs, sl]).astype(BF16)
    for j in range(0, D_MODEL, DOT_COLS):
        t.s[:, j:j + DOT_COLS] = _dot(t.yn[:, 0:D_MODEL], w.wo[:, j:j + DOT_COLS])
    for ss, ls, rs in _row_blocks(S, L, ROW_BLOCK):
        o3 = t.s[rs, :].reshape(ss.stop - ss.start, ls.stop - ls.start, D_MODEL)
        xn = x_ref[ss, ls, :] + gate_of(ss) * o3
        ms = jnp.mean(xn * xn, axis=-1, keepdims=True)
        y_ref[ss, ls, :] = xn * lax.rsqrt(ms + EPS) * w.fg[...]


def _ssd_refs(it):
    names = ("xs", "b", "c", "yssd", "pm", "p", "ea", "we", "hd", "aux", "yn", "s", "zs", "g")
    return types.SimpleNamespace(**{n: next(it) for n in names})


def _input_projection(w, h_ref, stores):
    for col0, (ncols, store) in stores.items():
        for col, wd, j in _proj_chunks(col0, ncols):
            if col < COL_DT:
                wc = w.wmain[:, col:col + wd]
            else:
                wc = w.wtail[:, col - COL_DT:col - COL_DT + wd]
            store(j, wd, _dot(h_ref[...], wc))


def _prompt_kernel(*refs, L):
    S = 1
    it = iter(refs)
    x_ref, mod_ref = next(it), next(it)
    w = _weight_refs(it)
    y_ref, pool_out_ref, conv_out_ref, ssm_out_ref = [next(it) for _ in range(4)]
    h_scr, up_scr, zp_scr, xbc_scr, dt_scr = [next(it) for _ in range(5)]
    t = _ssd_refs(it)
    t.yoff = next(it)
    step = pl.program_id(1)

    @pl.when(step == 0)
    def _():
        up_scr[:, 0:POOL_HDR, :] = jnp.zeros((S, POOL_HDR, D_POOL), F32)
        xbc_scr[:, 0:CONV_HDR, :] = jnp.zeros((S, CONV_HDR, D_XBC), F32)
        t.aux[...] = jnp.zeros(t.aux.shape, F32)

    _norm_modulate(w, x_ref, lambda ss: mod_ref[...], h_scr, S, L)

    def store_up(j, wd, v):
        up_scr[:, POOL_HDR:POOL_HDR + L, j:j + wd] = v.reshape(S, L, wd)

    def store_zp(j, wd, v):
        zp_scr[:, j:j + wd] = v

    def store_zs(j, wd, v):
        t.zs[:, j:j + wd] = v

    def store_xbc(j, wd, v):
        xbc_scr[:, CONV_HDR:CONV_HDR + L, j:j + wd] = v.reshape(S, L, wd)

    def store_dt(j, wd, v):
        dt_scr[...] = _softplus(v + w.dtb[...])

    def store_g(j, wd, v):
        t.g[:, j:j + wd] = v

    _input_projection(w, h_scr, {COL_XBC: (D_XBC, store_xbc)})

    for j in range(0, D_XBC, DOT_COLS):
        _conv_chunk(w, xbc_scr, t, j, S, L)
    xbc_scr[:, 0:CONV_HDR, :] = xbc_scr[:, L:L + CONV_HDR, :]

    _input_projection(w, h_scr, {
        COL_UP: (D_POOL, store_up), COL_ZP: (D_POOL, store_zp), COL_DT: (LANES, store_dt),
        COL_ZS: (D_INNER, store_zs), COL_G: (2 * D_MODEL, store_g)})

    _pool_windows(up_scr, t, step * L, S, L)
    _pool_mix(w, zp_scr, t)
    up_scr[:, 0:POOL_HDR, :] = up_scr[:, L:L + POOL_HDR, :]

    for c in range(L // CHUNK):
        _ssd_prep(w, t, dt_scr, c, True)
    for c in range(L // CHUNK):
        _ssd_intra(w, t, c, CHUNK, True)

    _dense_out(w, t, x_ref, lambda ss: mod_ref[...][:, :, 2 * D_MODEL:3 * D_MODEL], y_ref, S, L)

    @pl.when(step == pl.num_programs(1) - 1)
    def _():
        for n in range(D_INNER // LANES):
            cols = slice(n * LANES, (n + 1) * LANES)
            ssm_out_ref[0, cols, :] = t.aux[:, cols].T
        batch = pl.ds(pl.program_id(0), 1)
        for r in range(POOL_BUF):
            pool_out_ref[r, batch, :] = up_scr[0, 1 + r:2 + r, :]
        for r in range(CONV_W - 1):
            r0 = CONV_HDR - (CONV_W - 1) + r
            conv_out_ref[r, batch, :] = xbc_scr[0, r0:r0 + 1, :]


def _sample_kernel(*refs, S, L, n_inner, seq_per_inner):
    R = S * L
    it = iter(refs)
    x_ref, mod_ref, pool_in_ref, conv_in_ref, ssm_in_ref = [next(it) for _ in range(5)]
    w = _weight_refs(it)
    y_ref, pool_out_ref, conv_out_ref, ssm_out_ref = [next(it) for _ in range(4)]
    h_scr, up_scr, zp_scr, xbc_scr, dt_scr = [next(it) for _ in range(5)]
    t = _ssd_refs(it)
    step = pl.program_id(1)

    @pl.when(step == 0)
    def _():
        _norm_modulate(w, x_ref, lambda ss: mod_ref[ss], h_scr, S, L)

        def store_up(j, wd, v):
            up_scr[:, POOL_HDR:POOL_HDR + L, j:j + wd] = v.reshape(S, L, wd)

        def store_zp(j, wd, v):
            zp_scr[:, j:j + wd] = v

        def store_zs(j, wd, v):
            t.zs[:, j:j + wd] = v

        def store_xbc(j, wd, v):
            xbc_scr[:, CONV_HDR:CONV_HDR + L, j:j + wd] = v.reshape(S, L, wd)

        def store_dt(j, wd, v):
            dt_scr[...] = _softplus(v + w.dtb[...])

        def store_g(j, wd, v):
            t.g[:, j:j + wd] = v

        _input_projection(w, h_scr, {
            COL_UP: (D_POOL, store_up), COL_ZP: (D_POOL, store_zp), COL_DT: (LANES, store_dt),
            COL_XBC: (D_XBC, store_xbc), COL_ZS: (D_INNER, store_zs),
            COL_G: (2 * D_MODEL, store_g)})

        for r in range(POOL_BUF):
            up_scr[:, 1 + r, :] = pool_in_ref[r]
        _pool_windows(up_scr, t, PAST_LEN, S, L)
        _pool_mix(w, zp_scr, t)
        for r in range(POOL_BUF):
            pool_out_ref[r] = up_scr[:, L + 1 + r, :]

        for r in range(CONV_W - 1):
            xbc_scr[:, CONV_HDR - (CONV_W - 1) + r, :] = conv_in_ref[r]
        for j in range(0, D_XBC, DOT_COLS):
            _conv_chunk(w, xbc_scr, t, j, S, L)
        for r in range(CONV_W - 1):
            conv_out_ref[r] = xbc_scr[:, L + CONV_HDR - (CONV_W - 1) + r, :]
        for c in range(R // CHUNK):
            _ssd_prep(w, t, dt_scr, c, False)
        for c in range(R // CHUNK):
            _ssd_intra(w, t, c, L, False)

    for q in range(0, seq_per_inner, 2):
        _sample_state_pair(t, (step * seq_per_inner + q) * L, L,
                           (ssm_in_ref.at[q], ssm_in_ref.at[q + 1]),
                           (ssm_out_ref.at[q], ssm_out_ref.at[q + 1]))

    @pl.when(step == n_inner - 1)
    def _():
        _dense_out(w, t, x_ref, lambda ss: mod_ref[ss][:, :, 2 * D_MODEL:3 * D_MODEL],
                   y_ref, S, L)


def _resident(shape):
    zeros = (0,) * len(shape)
    return pl.BlockSpec(shape, lambda *_: zeros, pipeline_mode=pl.Buffered(1))


def _ssd_scratch(R, aux_shape):
    return [
        pltpu.VMEM((R, D_INNER), F32),
        pltpu.VMEM((R, N_BC_GROUPS * D_STATE), F32),
        pltpu.VMEM((R, N_BC_GROUPS * D_STATE), F32),
        pltpu.VMEM((R, D_INNER), F32),
        pltpu.VMEM((R, D_POOL), BF16),
        pltpu.VMEM((R, D_MODEL), F32),
        pltpu.VMEM((R, D_INNER), F32),
        pltpu.VMEM((R, D_INNER), F32),
        pltpu.VMEM((R // CHUNK, 2, CHUNK, LANES), F32),
        pltpu.VMEM(aux_shape, F32),
        pltpu.VMEM((R, D_INNER), BF16),
        pltpu.VMEM((R, D_MODEL), F32),
        pltpu.VMEM((R, D_INNER), F32),
        pltpu.VMEM((R, 2 * D_MODEL), F32),
    ]


def _stage_scratch(S, L):
    return [
        pltpu.VMEM((S * L, D_MODEL), BF16),
        pltpu.VMEM((S, POOL_HDR + L, D_POOL), F32),
        pltpu.VMEM((S * L, D_POOL), F32),
        pltpu.VMEM((S, CONV_HDR + L, D_XBC), F32),
        pltpu.VMEM((S * L, LANES), F32),
    ]


def _prompt_call(x, mod, weights, *, L):
    nb, seq, _ = x.shape
    x_spec = pl.BlockSpec((1, L, D_MODEL), lambda b, t: (b, t, 0))
    seq_map = lambda b, t: (b, 0, 0)
    in_specs = ([x_spec, pl.BlockSpec((1, 1, 3 * D_MODEL), seq_map)]
                + [_resident(v.shape) for v in weights])
    whole = lambda b, t: (0, 0, 0)
    out_shape = (jax.ShapeDtypeStruct(x.shape, F32),
                 jax.ShapeDtypeStruct((POOL_BUF, nb, D_POOL), F32),
                 jax.ShapeDtypeStruct((CONV_W - 1, nb, D_XBC), F32),
                 jax.ShapeDtypeStruct((nb, D_INNER, D_STATE), F32))
    out_specs = (x_spec,
                 pl.BlockSpec((POOL_BUF, nb, D_POOL), whole),
                 pl.BlockSpec((CONV_W - 1, nb, D_XBC), whole),
                 pl.BlockSpec((1, D_INNER, D_STATE), seq_map))
    scratch = (_stage_scratch(1, L) + _ssd_scratch(L, (D_STATE, D_INNER))
               + [pltpu.VMEM((CHUNK, GROUP_COLS), F32)])
    kern = functools.partial(_prompt_kernel, L=L)
    return pl.pallas_call(
        kern, grid=(nb, seq // L), in_specs=in_specs, out_specs=out_specs, out_shape=out_shape,
        scratch_shapes=scratch, name="layer_prompt",
        compiler_params=pltpu.CompilerParams(
            dimension_semantics=("arbitrary", "arbitrary"), vmem_limit_bytes=VMEM_LIMIT_BYTES),
    )(x, mod, *weights)


def _sample_call(x, mod, states, weights, *, S, seq_per_inner):
    ns, L, _ = x.shape
    R = S * L
    n_inner = S // seq_per_inner
    seq_map = lambda o, i: (o, 0, 0)
    st_map = lambda o, i: (o * n_inner + i, 0, 0)
    time_major = lambda o, i: (0, o, 0)
    x_spec = pl.BlockSpec((S, L, D_MODEL), seq_map)
    pool_spec = pl.BlockSpec((POOL_BUF, S, D_POOL), time_major)
    conv_spec = pl.BlockSpec((CONV_W - 1, S, D_XBC), time_major)
    st_spec = pl.BlockSpec((seq_per_inner, D_INNER, D_STATE), st_map)
    in_specs = ([x_spec, pl.BlockSpec((S, 1, 3 * D_MODEL), seq_map), pool_spec, conv_spec, st_spec]
                + [_resident(v.shape) for v in weights])
    out_shape = (jax.ShapeDtypeStruct(x.shape, F32),
                 jax.ShapeDtypeStruct((POOL_BUF, ns, D_POOL), F32),
                 jax.ShapeDtypeStruct((CONV_W - 1, ns, D_XBC), F32),
                 jax.ShapeDtypeStruct((ns, D_INNER, D_STATE), F32))
    out_specs = (x_spec, pool_spec, conv_spec, st_spec)
    scratch = _stage_scratch(S, L) + _ssd_scratch(R, (R // CHUNK, LANES, CHUNK))
    kern = functools.partial(_sample_kernel, S=S, L=L, n_inner=n_inner,
                             seq_per_inner=seq_per_inner)
    return pl.pallas_call(
        kern, grid=(ns // S, n_inner), in_specs=in_specs, out_specs=out_specs,
        out_shape=out_shape, scratch_shapes=scratch, name="layer_sample",
        compiler_params=pltpu.CompilerParams(
            dimension_semantics=("arbitrary", "arbitrary"), vmem_limit_bytes=VMEM_LIMIT_BYTES),
    )(x, mod, *states, *weights)


def _block_sum_matrix(ql):
    r = jnp.arange(CHUNK)[:, None]
    c = jnp.arange(CHUNK)[None, :]
    same = (r // ql) == (c // ql)
    return jnp.concatenate([same & (c <= r), same], axis=0).astype(BF16)


def _expand_matrix():
    k = jnp.arange(LANES)[:, None]
    c = jnp.arange(D_INNER)[None, :] // HEAD_DIM
    return ((k % N_HEADS == c) & (k < 3 * N_HEADS)).astype(BF16)


def kernel(x_prompt, x_sample, state_pool, state_conv, state_ssm, c_prompt, c_sample, w_ada, b_ada, norm_g, w_in, conv_w, conv_b, dt_bias, a_log, d_skip, ssm_norm_g, pool_w, pool_scale, w_pool_out, w_ssm_out, w_o, final_g):
    depth = w_ada.shape[0]
    assert depth == 1, "kernel is written for a single layer"
    nb, seq, _ = x_prompt.shape
    ns, dec_seq, _ = x_sample.shape
    l = 0

    def head_lanes(v):
        return jnp.pad(jnp.tile(v, (1, 3)), ((0, 0), (0, LANES - 3 * N_HEADS)))

    c_all = jnp.concatenate([c_prompt, c_sample], axis=0)
    mod_p, mod_s = pl.pallas_call(
        _adaln_kernel, grid=(3,),
        in_specs=[pl.BlockSpec((nb + ns, D_MODEL), lambda j: (0, 0)),
                  pl.BlockSpec((D_MODEL, D_MODEL), lambda j: (0, j)),
                  pl.BlockSpec((1, D_MODEL), lambda j: (0, j))],
        out_specs=(pl.BlockSpec((nb, 1, D_MODEL), lambda j: (0, 0, j)),
                   pl.BlockSpec((ns, 1, D_MODEL), lambda j: (0, 0, j))),
        out_shape=(jax.ShapeDtypeStruct((nb, 1, 3 * D_MODEL), F32),
                   jax.ShapeDtypeStruct((ns, 1, 3 * D_MODEL), F32)), name="adaln",
    )(c_all, w_ada[l], b_ada[l][None, :])

    e3 = _expand_matrix()
    w_main, w_tail = _repack_w_in(jnp.transpose(w_in[l]))

    def weights(ql):
        return [
            norm_g[l][None, :],
            w_main, w_tail,
            conv_w[l], conv_b[l][None, :],
            head_lanes(dt_bias[l][None, :]),
            head_lanes(a_log[l][None, :]),
            jnp.repeat(d_skip[l], HEAD_DIM)[None, :],
            ssm_norm_g[l][None, :],
            pool_w[l].astype(BF16), pool_scale[l][None, :],
            w_pool_out[l].astype(BF16), w_ssm_out[l].astype(BF16), w_o[l].astype(BF16),
            final_g[None, :],
            _block_sum_matrix(ql), e3,
        ]

    tile_rows = 2 * CHUNK
    y_p, pool_p, conv_p, ssm_p = _prompt_call(x_prompt, mod_p, weights(CHUNK), L=tile_rows)

    swap = lambda v: jnp.transpose(v, (1, 0, 2))
    states = (swap(state_pool[l]), swap(state_conv[l]),
              state_ssm[l].reshape(ns, D_INNER, D_STATE))
    y_s, pool_s, conv_s, ssm_s = _sample_call(
        x_sample, mod_s, states, weights(dec_seq), S=CHUNK // dec_seq, seq_per_inner=4)

    shp = (N_HEADS, HEAD_DIM, D_STATE)
    return (y_p, y_s, swap(pool_p)[None], swap(conv_p)[None], ssm_p.reshape((1, nb) + shp),
            swap(pool_s)[None], swap(conv_s)[None], ssm_s.reshape((1, ns) + shp))
```

```python
import functools
import types

import jax
import jax.numpy as jnp
from jax import lax
from jax.experimental import pallas as pl
from jax.experimental.pallas import tpu as pltpu

F32 = jnp.float32
BF16 = jnp.bfloat16

EPS = 1e-6
D_MODEL = 1024
POOL_WINDOWS = (2, 4, 8, 16)
POOL_GROUP = 128
D_POOL = POOL_GROUP * len(POOL_WINDOWS)
POOL_BUF = max(POOL_WINDOWS) - 1
D_INNER = 2048
HEAD_DIM = 64
N_HEADS = D_INNER // HEAD_DIM
N_BC_GROUPS = 4
HEADS_PER_GROUP = N_HEADS // N_BC_GROUPS
GROUP_COLS = HEADS_PER_GROUP * HEAD_DIM
D_STATE = 128
CONV_W = 4
D_XBC = D_INNER + 2 * N_BC_GROUPS * D_STATE
PAST_LEN = 16384

LANES = 128
CHUNK = 128
POOL_HDR = 16
CONV_HDR = 8
DOT_COLS = 512
ROW_BLOCK = 16
CONV_ROW_BLOCK = 32
VMEM_LIMIT_BYTES = 63 * 1024 * 1024

COL_UP = 0
COL_ZP = D_POOL
COL_ZS = 2 * D_POOL
COL_XBC = COL_ZS + D_INNER
COL_DT = COL_XBC + D_XBC
COL_G = COL_DT + LANES
N_WEIGHTS = 17
LOG2E = 1.4426950408889634


def _dot(a, b):
    return jnp.dot(a, b, preferred_element_type=F32)


def _dot_nt(a, b):
    return lax.dot_general(a, b, (((1,), (1,)), ((), ())), preferred_element_type=F32)


def _pack3(v, lane):
    hi = v.astype(BF16).astype(F32)
    r = v - hi
    mid = r.astype(BF16).astype(F32)
    lo = r - mid
    packed = jnp.where(lane < N_HEADS, hi,
                       jnp.where(lane < 2 * N_HEADS, mid,
                                 jnp.where(lane < 3 * N_HEADS, lo, 0.0)))
    return packed.astype(BF16)


def _silu(v):
    return v / (1.0 + jnp.exp2(v * -LOG2E))


def _sigmoid(v):
    return 1.0 / (1.0 + jnp.exp2(v * -LOG2E))


def _softplus(v):
    return jnp.maximum(v, 0.0) + jnp.log1p(jnp.exp(-jnp.abs(v)))


def _repack_main_kernel(x_ref, o_ref):
    o_ref[...] = x_ref[...].T.astype(o_ref.dtype)


def _repack_tail_kernel(dt_ref, g_ref, o_ref, *, g_row0):
    dt = dt_ref[...]
    heads = jnp.concatenate([dt, dt, dt, jnp.zeros((LANES - 3 * N_HEADS, D_MODEL), F32)], axis=0)
    o_ref[:, 0:LANES] = heads.T.astype(o_ref.dtype)
    for j in range(0, 2 * D_MODEL, DOT_COLS):
        o_ref[:, LANES + j:LANES + j + DOT_COLS] = (
            g_ref[g_row0 + j:g_row0 + j + DOT_COLS, :].T.astype(o_ref.dtype))


def _repack_w_in(w_in_t):
    n_in, d = w_in_t.shape
    assert d == D_MODEL and n_in == COL_DT + N_HEADS + 2 * D_MODEL
    w_main = pl.pallas_call(
        _repack_main_kernel, grid=(COL_DT // D_MODEL,),
        in_specs=[pl.BlockSpec((D_MODEL, D_MODEL), lambda j: (j, 0))],
        out_specs=pl.BlockSpec((D_MODEL, D_MODEL), lambda j: (0, j)),
        out_shape=jax.ShapeDtypeStruct((D_MODEL, COL_DT), BF16), name="repack_w_main",
    )(w_in_t)
    g_block = n_in // 4
    g_row = COL_DT + N_HEADS
    g_row0 = g_row % g_block
    assert n_in % g_block == 0 and g_block % 8 == 0 and g_row0 % 8 == 0
    assert g_row0 + 2 * D_MODEL <= g_block and COL_DT % N_HEADS == 0
    w_tail = pl.pallas_call(
        functools.partial(_repack_tail_kernel, g_row0=g_row0), grid=(1,),
        in_specs=[pl.BlockSpec((N_HEADS, D_MODEL), lambda i: (COL_DT // N_HEADS, 0)),
                  pl.BlockSpec((g_block, D_MODEL), lambda i: (g_row // g_block, 0))],
        out_specs=pl.BlockSpec((D_MODEL, LANES + 2 * D_MODEL), lambda i: (0, 0)),
        out_shape=jax.ShapeDtypeStruct((D_MODEL, LANES + 2 * D_MODEL), BF16), name="repack_w_tail",
        compiler_params=pltpu.CompilerParams(vmem_limit_bytes=VMEM_LIMIT_BYTES),
    )(w_in_t, w_in_t)
    return w_main, w_tail


def _adaln_kernel(c_ref, w_ref, b_ref, op_ref, os_ref):
    c = c_ref[...]
    mod = _dot(_silu(c).astype(BF16), w_ref[...].astype(BF16)) + b_ref[...]
    n_prompt = op_ref.shape[0]
    op_ref[:, 0, :] = mod[0:n_prompt, :]
    os_ref[:, 0, :] = mod[n_prompt:, :]


def _weight_refs(it):
    names = ("normg", "wmain", "wtail", "convw", "convb", "dtb", "alog", "dskip", "ssmg", "poolw",
             "pscale", "wpo", "wso", "wo", "fg", "ta", "e3")
    assert len(names) == N_WEIGHTS
    return types.SimpleNamespace(**{n: next(it) for n in names})


def _row_blocks(S, L, nrows):
    if L >= nrows:
        return [(slice(s, s + 1), slice(l0, l0 + nrows), slice(s * L + l0, s * L + l0 + nrows))
                for s in range(S) for l0 in range(0, L, nrows)]
    sb = nrows // L
    return [(slice(s0, s0 + sb), slice(0, L), slice(s0 * L, (s0 + sb) * L))
            for s0 in range(0, S, sb)]


def _norm_modulate(w, x_ref, mod_of, h_ref, S, L):
    for ss, ls, rs in _row_blocks(S, L, ROW_BLOCK):
        x3 = x_ref[ss, ls, :]
        mod3 = mod_of(ss)
        ms = jnp.mean(x3 * x3, axis=-1, keepdims=True)
        hn = x3 * lax.rsqrt(ms + EPS) * w.normg[...]
        hm = hn * (1.0 + mod3[:, :, D_MODEL:2 * D_MODEL]) + mod3[:, :, 0:D_MODEL]
        h_ref[rs, :] = hm.reshape(rs.stop - rs.start, D_MODEL).astype(BF16)


def _roll_rows(x, d, drop_first=False):
    n, rows, lanes = x.shape
    tiles = [x[:, j:j + 8, :] for j in range(0, rows, 8)]
    first = 1 if drop_first else 0
    if d == 8:
        out = [tiles[j - 1] for j in range(first, len(tiles))]
    else:
        rolled = [pltpu.roll(tl, d, 1) for tl in tiles]
        sub = lax.broadcasted_iota(jnp.int32, (n, 8, lanes), 1)
        out = [jnp.where(sub < d, rolled[j - 1], rolled[j]) for j in range(first, len(tiles))]
    return jnp.concatenate(out, axis=1)


def _proj_chunks(col0, ncols):
    return [(col0 + j, min(DOT_COLS, ncols - j), j) for j in range(0, ncols, DOT_COLS)]


def _pool_windows(up3, t, pos0, S, L):
    for ss, ls, rs in _row_blocks(S, L, ROW_BLOCK):
        nseq, ntok = ss.stop - ss.start, ls.stop - ls.start
        pos = pos0 + ls.start + lax.broadcasted_iota(jnp.int32, (nseq, ntok, POOL_GROUP), 1)
        for gi, win in enumerate(POOL_WINDOWS):
            sl = slice(gi * POOL_GROUP, (gi + 1) * POOL_GROUP)
            s = up3[ss, ls.start:POOL_HDR + ls.stop, sl]
            u = s[:, POOL_HDR:, :]
            shift = 1
            while shift < win:
                s = s + _roll_rows(s, shift)
                shift *= 2
            cnt = jnp.minimum(pos + 1, win).astype(F32)
            t.pm[rs, sl] = (s[:, POOL_HDR:, :] / cnt - u).reshape(
                nseq * ntok, POOL_GROUP).astype(BF16)


def _pool_mix(w, zp, t):
    for gi in range(len(POOL_WINDOWS)):
        sl = slice(gi * POOL_GROUP, (gi + 1) * POOL_GROUP)
        yg = _dot(t.pm[:, sl], w.poolw[gi]) * w.pscale[:, sl]
        t.pm[:, sl] = (yg * _silu(zp[:, sl])).astype(BF16)
    for j in range(0, D_MODEL, DOT_COLS):
        t.p[:, j:j + DOT_COLS] = _dot(t.pm[...], w.wpo[:, j:j + DOT_COLS])


def _conv_chunk(w, xbc3, t, j, S, L):
    sl = slice(j, j + DOT_COLS)
    tile_of = lambda row: jnp.broadcast_to(row[None], (1, 8, DOT_COLS))
    taps_w = [tile_of(w.convw[k:k + 1, sl]) for k in range(CONV_W)]
    bias = tile_of(w.convb[:, sl])
    for ss, ls, rs in _row_blocks(S, L, CONV_ROW_BLOCK):
        ext = xbc3[ss, ls.start:CONV_HDR + ls.stop, sl]
        taps = [ext[:, CONV_HDR:, :] if k == CONV_W - 1
                else _roll_rows(ext, CONV_W - 1 - k, drop_first=True) for k in range(CONV_W)]
        out_tiles = []
        for r in range(0, ls.stop - ls.start, 8):
            acc = bias
            for k in range(CONV_W):
                acc = acc + taps[k][:, r:r + 8, :] * taps_w[k]
            out_tiles.append(_silu(acc))
        v = jnp.concatenate(out_tiles, axis=1).reshape(rs.stop - rs.start, DOT_COLS)
        if j < D_INNER:
            t.xs[rs, sl] = v
        elif j < D_INNER + N_BC_GROUPS * D_STATE:
            t.b[rs, :] = v
        else:
            t.c[rs, :] = v


def _ssd_prep(w, t, dt_ref, c, prompt):
    rows = pl.ds(c * CHUNK, CHUNK)
    lane = lax.broadcasted_iota(jnp.int32, (CHUNK, LANES), 1)
    neg_a = jnp.where(lane[0:1, :] < 3 * N_HEADS, -jnp.exp(w.alog[...]), 0.0)
    dtc = dt_ref[rows, :]
    a = dtc * neg_a
    cs = _dot(w.ta[...], _pack3(a, lane))
    cs = cs + pltpu.roll(cs, 2 * N_HEADS, 1)
    cs = cs + pltpu.roll(cs, N_HEADS, 1)
    acs = cs[0:CHUNK, :]
    last = cs[CHUNK:2 * CHUNK, :]
    exp_acs = jnp.exp(acs)
    wgt = jnp.exp(last - acs) * dtc
    t.ea[rows, :] = _dot(_pack3(exp_acs, lane), w.e3[...])
    t.we[rows, :] = _dot(_pack3(wgt, lane), w.e3[...])
    if not prompt:
        t.aux[c] = jnp.exp(last).T
    acs2 = acs * LOG2E
    t.hd[c, 0] = acs2
    t.hd[c, 1] = (acs2 - jnp.log2(dtc)).T


def _ssd_intra(w, t, c, QL, prompt):
    row_start = c * CHUNK
    rows = pl.ds(row_start, CHUNK)
    lane = lax.broadcasted_iota(jnp.int32, (CHUNK, LANES), 1)
    row = lax.broadcasted_iota(jnp.int32, (CHUNK, LANES), 0)
    acs2, shifted_t = t.hd[c, 0], t.hd[c, 1]
    ql_bits = QL.bit_length() - 1
    mask = ((lax.shift_right_logical(row, ql_bits) == lax.shift_right_logical(lane, ql_bits))
            & (lane <= row))
    for g in range(N_BC_GROUPS):
        gs = slice(g * D_STATE, (g + 1) * D_STATE)
        gc = slice(g * GROUP_COLS, (g + 1) * GROUP_COLS)
        cg = t.c[rows, gs].astype(BF16)
        cb = _dot_nt(cg, t.b[rows, gs].astype(BF16)).astype(BF16)
        if prompt:
            ht = t.aux[:, gc]
            t.yoff[...] = _dot(cg, ht.astype(BF16))
            xw = (t.xs[rows, gc] * t.we[rows, gc]).astype(BF16)
            upd = _dot(t.b[rows, gs].T.astype(BF16), xw)
            t.aux[:, gc] = ht * t.ea[row_start + CHUNK - 1:row_start + CHUNK, gc] + upd
        for k in range(HEADS_PER_GROUP // 2):
            atts = []
            for h in (g * HEADS_PER_GROUP + 2 * k, g * HEADS_PER_GROUP + 2 * k + 1):
                seg2 = acs2[:, h:h + 1] - shifted_t[h:h + 1, :]
                decay_dt = jnp.exp2(jnp.where(mask, seg2, -jnp.inf))
                atts.append(cb * decay_dt.astype(BF16))
            ps = slice((g * HEADS_PER_GROUP + 2 * k) * HEAD_DIM,
                       (g * HEADS_PER_GROUP + 2 * k + 2) * HEAD_DIM)
            xp = t.xs[rows, ps]
            rhs = jnp.concatenate([jnp.where(lane < HEAD_DIM, xp, 0.0).astype(BF16),
                                   jnp.where(lane >= HEAD_DIM, xp, 0.0).astype(BF16)], axis=0)
            yp = _dot(jnp.concatenate(atts, axis=1), rhs) + w.dskip[:, ps] * xp
            if prompt:
                yp = yp + t.yoff[:, 2 * k * HEAD_DIM:(2 * k + 2) * HEAD_DIM] * t.ea[rows, ps]
            t.yssd[rows, ps] = yp


def _sample_state_pair(t, r, QL, hins, houts):
    n = 2 * QL
    c = r // CHUNK
    r0 = r % CHUNK
    rows = pl.ds(pl.multiple_of(r, n), n)
    lane = lax.broadcasted_iota(jnp.int32, (CHUNK, LANES), 1)
    rowi = lax.broadcasted_iota(jnp.int32, (n, GROUP_COLS), 0)
    total_t = t.aux[c]
    for q in range(2):
        col = jnp.sum(jnp.where(lane == r0 + q * QL, total_t, 0.0), axis=1, keepdims=True)
        rb = jnp.broadcast_to(col, (CHUNK, LANES))
        mine = (rowi >= q * QL) & (rowi < (q + 1) * QL)
        for g in range(N_BC_GROUPS):
            gs = slice(g * D_STATE, (g + 1) * D_STATE)
            cs = slice(g * GROUP_COLS, (g + 1) * GROUP_COLS)
            hg = hins[q][cs, :]
            yoff = _dot_nt(t.c[rows, gs].astype(BF16), hg.astype(BF16)) * t.ea[rows, cs]
            t.yssd[rows, cs] = t.yssd[rows, cs] + jnp.where(mine, yoff, 0.0)
            xw = jnp.where(mine, t.xs[rows, cs] * t.we[rows, cs], 0.0)
            upd = _dot(xw.T.astype(BF16), t.b[rows, gs].astype(BF16))
            for e in range(HEADS_PER_GROUP):
                h = g * HEADS_PER_GROUP + e
                hs = slice(h * HEAD_DIM, (h + 1) * HEAD_DIM)
                es = slice(e * HEAD_DIM, (e + 1) * HEAD_DIM)
                houts[q][hs, :] = hg[es, :] * rb[h:h + 1, :] + upd[es, :]


def _dense_out(w, t, x_ref, gate_of, y_ref, S, L):
    R = S * L
    col_chunks = [slice(j, j + DOT_COLS) for j in range(0, D_INNER, DOT_COLS)]
    for r in range(0, R, ROW_BLOCK):
        rs = slice(r, r + ROW_BLOCK)
        ys = [t.yssd[rs, sl] * _silu(t.zs[rs, sl]) for sl in col_chunks]
        ssq = sum(jnp.sum(y * y, axis=-1, keepdims=True) for y in ys)
        inv = lax.rsqrt(ssq * (1.0 / D_INNER) + EPS)
        for y, sl in zip(ys, col_chunks):
            t.yn[rs, sl] = (y * inv * w.ssmg[:, sl]).astype(BF16)
    for j in range(0, D_MODEL, DOT_COLS):
        t.s[:, j:j + DOT_COLS] = _dot(t.yn[...], w.wso[:, j:j + DOT_COLS])
    for r in range(0, R, ROW_BLOCK):
        rs = slice(r, r + ROW_BLOCK)
        for j in range(0, D_MODEL, DOT_COLS):
            sl = slice(j, j + DOT_COLS)
            sl2 = slice(D_MODEL + j, D_MODEL + j + DOT_COLS)
            t.yn[rs, sl] = (_sigmoid(t.g[rs, sl]) * t.p[rs, sl]
                            + _sigmoid(t.g[rs, sl2]) * t.s[rs, sl]).astype(BF16)
    for j in range(0, D_MODEL, DOT_COLS):
        t.s[:, j:j + DOT_COLS] = _dot(t.yn[:, 0:D_MODEL], w.wo[:, j:j + DOT_COLS])
    for ss, ls, rs in _row_blocks(S, L, ROW_BLOCK):
        o3 = t.s[rs, :].reshape(ss.stop - ss.start, ls.stop - ls.start, D_MODEL)
        xn = x_ref[ss, ls, :] + gate_of(ss) * o3
        ms = jnp.mean(xn * xn, axis=-1, keepdims=True)
        y_ref[ss, ls, :] = xn * lax.rsqrt(ms + EPS) * w.fg[...]


def _ssd_refs(it):
    names = ("xs", "b", "c", "yssd", "pm", "p", "ea", "we", "hd", "aux", "yn", "s", "zs", "g")
    return types.SimpleNamespace(**{n: next(it) for n in names})


def _input_projection(w, h_ref, stores):
    for col0, (ncols, store) in stores.items():
        for col, wd, j in _proj_chunks(col0, ncols):
            if col < COL_DT:
                wc = w.wmain[:, col:col + wd]
            else:
                wc = w.wtail[:, col - COL_DT:col - COL_DT + wd]
            store(j, wd, _dot(h_ref[...], wc))


def _prompt_kernel(*refs, L):
    S = 1
    it = iter(refs)
    x_ref, mod_ref = next(it), next(it)
    w = _weight_refs(it)
    y_ref, pool_out_ref, conv_out_ref, ssm_out_ref = [next(it) for _ in range(4)]
    h_scr, up_scr, zp_scr, xbc_scr, dt_scr = [next(it) for _ in range(5)]
    t = _ssd_refs(it)
    t.yoff = next(it)
    step = pl.program_id(1)

    @pl.when(step == 0)
    def _():
        up_scr[:, 0:POOL_HDR, :] = jnp.zeros((S, POOL_HDR, D_POOL), F32)
        xbc_scr[:, 0:CONV_HDR, :] = jnp.zeros((S, CONV_HDR, D_XBC), F32)
        t.aux[...] = jnp.zeros(t.aux.shape, F32)

    _norm_modulate(w, x_ref, lambda ss: mod_ref[...], h_scr, S, L)

    def store_up(j, wd, v):
        up_scr[:, POOL_HDR:POOL_HDR + L, j:j + wd] = v.reshape(S, L, wd)

    def store_zp(j, wd, v):
        zp_scr[:, j:j + wd] = v

    def store_zs(j, wd, v):
        t.zs[:, j:j + wd] = v

    def store_xbc(j, wd, v):
        xbc_scr[:, CONV_HDR:CONV_HDR + L, j:j + wd] = v.reshape(S, L, wd)

    def store_dt(j, wd, v):
        dt_scr[...] = _softplus(v + w.dtb[...])

    def store_g(j, wd, v):
        t.g[:, j:j + wd] = v

    _input_projection(w, h_scr, {COL_XBC: (D_XBC, store_xbc)})

    for j in range(0, D_XBC, DOT_COLS):
        _conv_chunk(w, xbc_scr, t, j, S, L)
    xbc_scr[:, 0:CONV_HDR, :] = xbc_scr[:, L:L + CONV_HDR, :]

    _input_projection(w, h_scr, {
        COL_UP: (D_POOL, store_up), COL_ZP: (D_POOL, store_zp), COL_DT: (LANES, store_dt),
        COL_ZS: (D_INNER, store_zs), COL_G: (2 * D_MODEL, store_g)})

    _pool_windows(up_scr, t, step * L, S, L)
    _pool_mix(w, zp_scr, t)
    up_scr[:, 0:POOL_HDR, :] = up_scr[:, L:L + POOL_HDR, :]

    for c in range(L // CHUNK):
        _ssd_prep(w, t, dt_scr, c, True)
    for c in range(L // CHUNK):
        _ssd_intra(w, t, c, CHUNK, True)

    _dense_out(w, t, x_ref, lambda ss: mod_ref[...][:, :, 2 * D_MODEL:3 * D_MODEL], y_ref, S, L)

    @pl.when(step == pl.num_programs(1) - 1)
    def _():
        for n in range(D_INNER // LANES):
            cols = slice(n * LANES, (n + 1) * LANES)
            ssm_out_ref[0, cols, :] = t.aux[:, cols].T
        batch = pl.ds(pl.program_id(0), 1)
        for r in range(POOL_BUF):
            pool_out_ref[r, batch, :] = up_scr[0, 1 + r:2 + r, :]
        for r in range(CONV_W - 1):
            r0 = CONV_HDR - (CONV_W - 1) + r
            conv_out_ref[r, batch, :] = xbc_scr[0, r0:r0 + 1, :]


def _sample_kernel(*refs, S, L, n_inner, seq_per_inner):
    R = S * L
    it = iter(refs)
    x_ref, mod_ref, pool_in_ref, conv_in_ref, ssm_in_ref = [next(it) for _ in range(5)]
    w = _weight_refs(it)
    y_ref, pool_out_ref, conv_out_ref, ssm_out_ref = [next(it) for _ in range(4)]
    h_scr, up_scr, zp_scr, xbc_scr, dt_scr = [next(it) for _ in range(5)]
    t = _ssd_refs(it)
    step = pl.program_id(1)

    @pl.when(step == 0)
    def _():
        _norm_modulate(w, x_ref, lambda ss: mod_ref[ss], h_scr, S, L)

        def store_up(j, wd, v):
            up_scr[:, POOL_HDR:POOL_HDR + L, j:j + wd] = v.reshape(S, L, wd)

        def store_zp(j, wd, v):
            zp_scr[:, j:j + wd] = v

        def store_zs(j, wd, v):
            t.zs[:, j:j + wd] = v

        def store_xbc(j, wd, v):
            xbc_scr[:, CONV_HDR:CONV_HDR + L, j:j + wd] = v.reshape(S, L, wd)

        def store_dt(j, wd, v):
            dt_scr[...] = _softplus(v + w.dtb[...])

        def store_g(j, wd, v):
            t.g[:, j:j + wd] = v

        _input_projection(w, h_scr, {
            COL_UP: (D_POOL, store_up), COL_ZP: (D_POOL, store_zp), COL_DT: (LANES, store_dt),
            COL_XBC: (D_XBC, store_xbc), COL_ZS: (D_INNER, store_zs),
            COL_G: (2 * D_MODEL, store_g)})

        for r in range(POOL_BUF):
            up_scr[:, 1 + r, :] = pool_in_ref[r]
        _pool_windows(up_scr, t, PAST_LEN, S, L)
        _pool_mix(w, zp_scr, t)
        for r in range(POOL_BUF):
            pool_out_ref[r] = up_scr[:, L + 1 + r, :]

        for r in range(CONV_W - 1):
            xbc_scr[:, CONV_HDR - (CONV_W - 1) + r, :] = conv_in_ref[r]
        for j in range(0, D_XBC, DOT_COLS):
            _conv_chunk(w, xbc_scr, t, j, S, L)
        for r in range(CONV_W - 1):
            conv_out_ref[r] = xbc_scr[:, L + CONV_HDR - (CONV_W - 1) + r, :]
        for c in range(R // CHUNK):
            _ssd_prep(w, t, dt_scr, c, False)
        for c in range(R // CHUNK):
            _ssd_intra(w, t, c, L, False)

    for q in range(0, seq_per_inner, 2):
        _sample_state_pair(t, (step * seq_per_inner + q) * L, L,
                           (ssm_in_ref.at[q], ssm_in_ref.at[q + 1]),
                           (ssm_out_ref.at[q], ssm_out_ref.at[q + 1]))

    @pl.when(step == n_inner - 1)
    def _():
        _dense_out(w, t, x_ref, lambda ss: mod_ref[ss][:, :, 2 * D_MODEL:3 * D_MODEL],
                   y_ref, S, L)


def _resident(shape):
    zeros = (0,) * len(shape)
    return pl.BlockSpec(shape, lambda *_: zeros, pipeline_mode=pl.Buffered(1))


def _ssd_scratch(R, aux_shape):
    return [
        pltpu.VMEM((R, D_INNER), F32),
        pltpu.VMEM((R, N_BC_GROUPS * D_STATE), F32),
        pltpu.VMEM((R, N_BC_GROUPS * D_STATE), F32),
        pltpu.VMEM((R, D_INNER), F32),
        pltpu.VMEM((R, D_POOL), BF16),
        pltpu.VMEM((R, D_MODEL), F32),
        pltpu.VMEM((R, D_INNER), F32),
        pltpu.VMEM((R, D_INNER), F32),
        pltpu.VMEM((R // CHUNK, 2, CHUNK, LANES), F32),
        pltpu.VMEM(aux_shape, F32),
        pltpu.VMEM((R, D_INNER), BF16),
        pltpu.VMEM((R, D_MODEL), F32),
        pltpu.VMEM((R, D_INNER), F32),
        pltpu.VMEM((R, 2 * D_MODEL), F32),
    ]


def _stage_scratch(S, L):
    return [
        pltpu.VMEM((S * L, D_MODEL), BF16),
        pltpu.VMEM((S, POOL_HDR + L, D_POOL), F32),
        pltpu.VMEM((S * L, D_POOL), F32),
        pltpu.VMEM((S, CONV_HDR + L, D_XBC), F32),
        pltpu.VMEM((S * L, LANES), F32),
    ]


def _prompt_call(x, mod, weights, *, L):
    nb, seq, _ = x.shape
    x_spec = pl.BlockSpec((1, L, D_MODEL), lambda b, t: (b, t, 0))
    seq_map = lambda b, t: (b, 0, 0)
    in_specs = ([x_spec, pl.BlockSpec((1, 1, 3 * D_MODEL), seq_map)]
                + [_resident(v.shape) for v in weights])
    whole = lambda b, t: (0, 0, 0)
    out_shape = (jax.ShapeDtypeStruct(x.shape, F32),
                 jax.ShapeDtypeStruct((POOL_BUF, nb, D_POOL), F32),
                 jax.ShapeDtypeStruct((CONV_W - 1, nb, D_XBC), F32),
                 jax.ShapeDtypeStruct((nb, D_INNER, D_STATE), F32))
    out_specs = (x_spec,
                 pl.BlockSpec((POOL_BUF, nb, D_POOL), whole),
                 pl.BlockSpec((CONV_W - 1, nb, D_XBC), whole),
                 pl.BlockSpec((1, D_INNER, D_STATE), seq_map))
    scratch = (_stage_scratch(1, L) + _ssd_scratch(L, (D_STATE, D_INNER))
               + [pltpu.VMEM((CHUNK, GROUP_COLS), F32)])
    kern = functools.partial(_prompt_kernel, L=L)
    return pl.pallas_call(
        kern, grid=(nb, seq // L), in_specs=in_specs, out_specs=out_specs, out_shape=out_shape,
        scratch_shapes=scratch, name="layer_prompt",
        compiler_params=pltpu.CompilerParams(
            dimension_semantics=("arbitrary", "arbitrary"), vmem_limit_bytes=VMEM_LIMIT_BYTES),
    )(x, mod, *weights)


def _sample_call(x, mod, states, weights, *, S, seq_per_inner):
    ns, L, _ = x.shape
    R = S * L
    n_inner = S // seq_per_inner
    seq_map = lambda o, i: (o, 0, 0)
    st_map = lambda o, i: (o * n_inner + i, 0, 0)
    time_major = lambda o, i: (0, o, 0)
    x_spec = pl.BlockSpec((S, L, D_MODEL), seq_map)
    pool_spec = pl.BlockSpec((POOL_BUF, S, D_POOL), time_major)
    conv_spec = pl.BlockSpec((CONV_W - 1, S, D_XBC), time_major)
    st_spec = pl.BlockSpec((seq_per_inner, D_INNER, D_STATE), st_map)
    in_specs = ([x_spec, pl.BlockSpec((S, 1, 3 * D_MODEL), seq_map), pool_spec, conv_spec, st_spec]
                + [_resident(v.shape) for v in weights])
    out_shape = (jax.ShapeDtypeStruct(x.shape, F32),
                 jax.ShapeDtypeStruct((POOL_BUF, ns, D_POOL), F32),
                 jax.ShapeDtypeStruct((CONV_W - 1, ns, D_XBC), F32),
                 jax.ShapeDtypeStruct((ns, D_INNER, D_STATE), F32))
    out_specs = (x_spec, pool_spec, conv_spec, st_spec)
    scratch = _stage_scratch(S, L) + _ssd_scratch(R, (R // CHUNK, LANES, CHUNK))
    kern = functools.partial(_sample_kernel, S=S, L=L, n_inner=n_inner,
                             seq_per_inner=seq_per_inner)
    return pl.pallas_call(
        kern, grid=(ns // S, n_inner), in_specs=in_specs, out_specs=out_specs,
        out_shape=out_shape, scratch_shapes=scratch, name="layer_sample",
        compiler_params=pltpu.CompilerParams(
            dimension_semantics=("arbitrary", "arbitrary"), vmem_limit_bytes=VMEM_LIMIT_BYTES),
    )(x, mod, *states, *weights)


def _block_sum_matrix(ql):
    r = jnp.arange(CHUNK)[:, None]
    c = jnp.arange(CHUNK)[None, :]
    same = (r // ql) == (c // ql)
    return jnp.concatenate([same & (c <= r), same], axis=0).astype(BF16)


def _expand_matrix():
    k = jnp.arange(LANES)[:, None]
    c = jnp.arange(D_INNER)[None, :] // HEAD_DIM
    return ((k % N_HEADS == c) & (k < 3 * N_HEADS)).astype(BF16)


def kernel(x_prompt, x_sample, state_pool, state_conv, state_ssm, c_prompt, c_sample, w_ada, b_ada, norm_g, w_in, conv_w, conv_b, dt_bias, a_log, d_skip, ssm_norm_g, pool_w, pool_scale, w_pool_out, w_ssm_out, w_o, final_g):
    depth = w_ada.shape[0]
    assert depth == 1, "kernel is written for a single layer"
    nb, seq, _ = x_prompt.shape
    ns, dec_seq, _ = x_sample.shape
    l = 0

    def head_lanes(v):
        return jnp.pad(jnp.tile(v, (1, 3)), ((0, 0), (0, LANES - 3 * N_HEADS)))

    c_all = jnp.concatenate([c_prompt, c_sample], axis=0)
    mod_p, mod_s = pl.pallas_call(
        _adaln_kernel, grid=(3,),
        in_specs=[pl.BlockSpec((nb + ns, D_MODEL), lambda j: (0, 0)),
                  pl.BlockSpec((D_MODEL, D_MODEL), lambda j: (0, j)),
                  pl.BlockSpec((1, D_MODEL), lambda j: (0, j))],
        out_specs=(pl.BlockSpec((nb, 1, D_MODEL), lambda j: (0, 0, j)),
                   pl.BlockSpec((ns, 1, D_MODEL), lambda j: (0, 0, j))),
        out_shape=(jax.ShapeDtypeStruct((nb, 1, 3 * D_MODEL), F32),
                   jax.ShapeDtypeStruct((ns, 1, 3 * D_MODEL), F32)), name="adaln",
    )(c_all, w_ada[l], b_ada[l][None, :])

    e3 = _expand_matrix()
    w_main, w_tail = _repack_w_in(jnp.transpose(w_in[l]))

    def weights(ql):
        return [
            norm_g[l][None, :],
            w_main, w_tail,
            conv_w[l], conv_b[l][None, :],
            head_lanes(dt_bias[l][None, :]),
            head_lanes(a_log[l][None, :]),
            jnp.repeat(d_skip[l], HEAD_DIM)[None, :],
            ssm_norm_g[l][None, :],
            pool_w[l].astype(BF16), pool_scale[l][None, :],
            w_pool_out[l].astype(BF16), w_ssm_out[l].astype(BF16), w_o[l].astype(BF16),
            final_g[None, :],
            _block_sum_matrix(ql), e3,
        ]

    tile_rows = 2 * CHUNK
    y_p, pool_p, conv_p, ssm_p = _prompt_call(x_prompt, mod_p, weights(CHUNK), L=tile_rows)

    swap = lambda v: jnp.transpose(v, (1, 0, 2))
    states = (swap(state_pool[l]), swap(state_conv[l]),
              state_ssm[l].reshape(ns, D_INNER, D_STATE))
    y_s, pool_s, conv_s, ssm_s = _sample_call(
        x_sample, mod_s, states, weights(dec_seq), S=CHUNK // dec_seq, seq_per_inner=4)

    shp = (N_HEADS, HEAD_DIM, D_STATE)
    return (y_p, y_s, swap(pool_p)[None], swap(conv_p)[None], ssm_p.reshape((1, nb) + shp),
            swap(pool_s)[None], swap(conv_s)[None], ssm_s.reshape((1, ns) + shp))
```

```python
import functools
import types

import jax
import jax.numpy as jnp
from jax import lax
from jax.experimental import pallas as pl
from jax.experimental.pallas import tpu as pltpu

F32 = jnp.float32
BF16 = jnp.bfloat16

EPS = 1e-6
D_MODEL = 1024
POOL_WINDOWS = (2, 4, 8, 16)
POOL_GROUP = 128
D_POOL = POOL_GROUP * len(POOL_WINDOWS)
POOL_BUF = max(POOL_WINDOWS) - 1
D_INNER = 2048
HEAD_DIM = 64
N_HEADS = D_INNER // HEAD_DIM
N_BC_GROUPS = 4
HEADS_PER_GROUP = N_HEADS // N_BC_GROUPS
GROUP_COLS = HEADS_PER_GROUP * HEAD_DIM
D_STATE = 128
CONV_W = 4
D_XBC = D_INNER + 2 * N_BC_GROUPS * D_STATE
PAST_LEN = 16384

LANES = 128
CHUNK = 128
POOL_HDR = 16
CONV_HDR = 8
DOT_COLS = 512
ROW_BLOCK = 16
CONV_ROW_BLOCK = 32
VMEM_LIMIT_BYTES = 63 * 1024 * 1024

COL_UP = 0
COL_ZP = D_POOL
COL_ZS = 2 * D_POOL
COL_XBC = COL_ZS + D_INNER
COL_DT = COL_XBC + D_XBC
COL_G = COL_DT + LANES
N_WEIGHTS = 17
LOG2E = 1.4426950408889634


def _dot(a, b):
    return jnp.dot(a, b, preferred_element_type=F32)


def _dot_nt(a, b):
    return lax.dot_general(a, b, (((1,), (1,)), ((), ())), preferred_element_type=F32)


def _pack3(v, lane):
    hi = v.astype(BF16).astype(F32)
    r = v - hi
    mid = r.astype(BF16).astype(F32)
    lo = r - mid
    packed = jnp.where(lane < N_HEADS, hi,
                       jnp.where(lane < 2 * N_HEADS, mid,
                                 jnp.where(lane < 3 * N_HEADS, lo, 0.0)))
    return packed.astype(BF16)


def _silu(v):
    return v / (1.0 + jnp.exp2(v * -LOG2E))


def _sigmoid(v):
    return 1.0 / (1.0 + jnp.exp2(v * -LOG2E))


def _softplus(v):
    return jnp.maximum(v, 0.0) + jnp.log1p(jnp.exp(-jnp.abs(v)))


def _repack_main_kernel(x_ref, o_ref):
    o_ref[...] = x_ref[...].T.astype(o_ref.dtype)


def _repack_tail_kernel(dt_ref, g_ref, o_ref, *, g_row0):
    dt = dt_ref[...]
    heads = jnp.concatenate([dt, dt, dt, jnp.zeros((LANES - 3 * N_HEADS, D_MODEL), F32)], axis=0)
    o_ref[:, 0:LANES] = heads.T.astype(o_ref.dtype)
    for j in range(0, 2 * D_MODEL, DOT_COLS):
        o_ref[:, LANES + j:LANES + j + DOT_COLS] = (
            g_ref[g_row0 + j:g_row0 + j + DOT_COLS, :].T.astype(o_ref.dtype))


def _repack_w_in(w_in_t):
    n_in, d = w_in_t.shape
    assert d == D_MODEL and n_in == COL_DT + N_HEADS + 2 * D_MODEL
    w_main = pl.pallas_call(
        _repack_main_kernel, grid=(COL_DT // D_MODEL,),
        in_specs=[pl.BlockSpec((D_MODEL, D_MODEL), lambda j: (j, 0))],
        out_specs=pl.BlockSpec((D_MODEL, D_MODEL), lambda j: (0, j)),
        out_shape=jax.ShapeDtypeStruct((D_MODEL, COL_DT), BF16), name="repack_w_main",
    )(w_in_t)
    g_block = n_in // 4
    g_row = COL_DT + N_HEADS
    g_row0 = g_row % g_block
    assert n_in % g_block == 0 and g_block % 8 == 0 and g_row0 % 8 == 0
    assert g_row0 + 2 * D_MODEL <= g_block and COL_DT % N_HEADS == 0
    w_tail = pl.pallas_call(
        functools.partial(_repack_tail_kernel, g_row0=g_row0), grid=(1,),
        in_specs=[pl.BlockSpec((N_HEADS, D_MODEL), lambda i: (COL_DT // N_HEADS, 0)),
                  pl.BlockSpec((g_block, D_MODEL), lambda i: (g_row // g_block, 0))],
        out_specs=pl.BlockSpec((D_MODEL, LANES + 2 * D_MODEL), lambda i: (0, 0)),
        out_shape=jax.ShapeDtypeStruct((D_MODEL, LANES + 2 * D_MODEL), BF16), name="repack_w_tail",
        compiler_params=pltpu.CompilerParams(vmem_limit_bytes=VMEM_LIMIT_BYTES),
    )(w_in_t, w_in_t)
    return w_main, w_tail


def _adaln_kernel(c_ref, w_ref, b_ref, op_ref, os_ref):
    c = c_ref[...]
    mod = _dot(_silu(c).astype(BF16), w_ref[...].astype(BF16)) + b_ref[...]
    n_prompt = op_ref.shape[0]
    op_ref[:, 0, :] = mod[0:n_prompt, :]
    os_ref[:, 0, :] = mod[n_prompt:, :]


def _weight_refs(it):
    names = ("normg", "wmain", "wtail", "convw", "convb", "dtb", "alog", "dskip", "ssmg", "poolw",
             "pscale", "wpo", "wso", "wo", "fg", "ta", "e3")
    assert len(names) == N_WEIGHTS
    return types.SimpleNamespace(**{n: next(it) for n in names})


def _row_blocks(S, L, nrows):
    if L >= nrows:
        return [(slice(s, s + 1), slice(l0, l0 + nrows), slice(s * L + l0, s * L + l0 + nrows))
                for s in range(S) for l0 in range(0, L, nrows)]
    sb = nrows // L
    return [(slice(s0, s0 + sb), slice(0, L), slice(s0 * L, (s0 + sb) * L))
            for s0 in range(0, S, sb)]


def _norm_modulate(w, x_ref, mod_of, h_ref, S, L):
    for ss, ls, rs in _row_blocks(S, L, ROW_BLOCK):
        x3 = x_ref[ss, ls, :]
        mod3 = mod_of(ss)
        ms = jnp.mean(x3 * x3, axis=-1, keepdims=True)
        hn = x3 * lax.rsqrt(ms + EPS) * w.normg[...]
        hm = hn * (1.0 + mod3[:, :, D_MODEL:2 * D_MODEL]) + mod3[:, :, 0:D_MODEL]
        h_ref[rs, :] = hm.reshape(rs.stop - rs.start, D_MODEL).astype(BF16)


def _roll_rows(x, d, drop_first=False):
    n, rows, lanes = x.shape
    tiles = [x[:, j:j + 8, :] for j in range(0, rows, 8)]
    first = 1 if drop_first else 0
    if d == 8:
        out = [tiles[j - 1] for j in range(first, len(tiles))]
    else:
        rolled = [pltpu.roll(tl, d, 1) for tl in tiles]
        sub = lax.broadcasted_iota(jnp.int32, (n, 8, lanes), 1)
        out = [jnp.where(sub < d, rolled[j - 1], rolled[j]) for j in range(first, len(tiles))]
    return jnp.concatenate(out, axis=1)


def _proj_chunks(col0, ncols):
    return [(col0 + j, min(DOT_COLS, ncols - j), j) for j in range(0, ncols, DOT_COLS)]


def _pool_windows(up3, t, pos0, S, L):
    for ss, ls, rs in _row_blocks(S, L, ROW_BLOCK):
        nseq, ntok = ss.stop - ss.start, ls.stop - ls.start
        pos = pos0 + ls.start + lax.broadcasted_iota(jnp.int32, (nseq, ntok, POOL_GROUP), 1)
        for gi, win in enumerate(POOL_WINDOWS):
            sl = slice(gi * POOL_GROUP, (gi + 1) * POOL_GROUP)
            s = up3[ss, ls.start:POOL_HDR + ls.stop, sl]
            u = s[:, POOL_HDR:, :]
            shift = 1
            while shift < win:
                s = s + _roll_rows(s, shift)
                shift *= 2
            cnt = jnp.minimum(pos + 1, win).astype(F32)
            t.pm[rs, sl] = (s[:, POOL_HDR:, :] / cnt - u).reshape(
                nseq * ntok, POOL_GROUP).astype(BF16)


def _pool_mix(w, zp, t):
    for gi in range(len(POOL_WINDOWS)):
        sl = slice(gi * POOL_GROUP, (gi + 1) * POOL_GROUP)
        yg = _dot(t.pm[:, sl], w.poolw[gi]) * w.pscale[:, sl]
        t.pm[:, sl] = (yg * _silu(zp[:, sl])).astype(BF16)
    for j in range(0, D_MODEL, DOT_COLS):
        t.p[:, j:j + DOT_COLS] = _dot(t.pm[...], w.wpo[:, j:j + DOT_COLS])


def _conv_chunk(w, xbc3, t, j, S, L):
    sl = slice(j, j + DOT_COLS)
    tile_of = lambda row: jnp.broadcast_to(row[None], (1, 8, DOT_COLS))
    taps_w = [tile_of(w.convw[k:k + 1, sl]) for k in range(CONV_W)]
    bias = tile_of(w.convb[:, sl])
    for ss, ls, rs in _row_blocks(S, L, CONV_ROW_BLOCK):
        ext = xbc3[ss, ls.start:CONV_HDR + ls.stop, sl]
        taps = [ext[:, CONV_HDR:, :] if k == CONV_W - 1
                else _roll_rows(ext, CONV_W - 1 - k, drop_first=True) for k in range(CONV_W)]
        out_tiles = []
        for r in range(0, ls.stop - ls.start, 8):
            acc = bias
            for k in range(CONV_W):
                acc = acc + taps[k][:, r:r + 8, :] * taps_w[k]
            out_tiles.append(_silu(acc))
        v = jnp.concatenate(out_tiles, axis=1).reshape(rs.stop - rs.start, DOT_COLS)
        if j < D_INNER:
            t.xs[rs, sl] = v
        elif j < D_INNER + N_BC_GROUPS * D_STATE:
            t.b[rs, :] = v
        else:
            t.c[rs, :] = v


def _ssd_prep(w, t, dt_ref, c, prompt):
    rows = pl.ds(c * CHUNK, CHUNK)
    lane = lax.broadcasted_iota(jnp.int32, (CHUNK, LANES), 1)
    neg_a = jnp.where(lane[0:1, :] < 3 * N_HEADS, -jnp.exp(w.alog[...]), 0.0)
    dtc = dt_ref[rows, :]
    a = dtc * neg_a
    cs = _dot(w.ta[...], _pack3(a, lane))
    cs = cs + pltpu.roll(cs, 2 * N_HEADS, 1)
    cs = cs + pltpu.roll(cs, N_HEADS, 1)
    acs = cs[0:CHUNK, :]
    last = cs[CHUNK:2 * CHUNK, :]
    exp_acs = jnp.exp(acs)
    wgt = jnp.exp(last - acs) * dtc
    t.ea[rows, :] = _dot(_pack3(exp_acs, lane), w.e3[...])
    t.we[rows, :] = _dot(_pack3(wgt, lane), w.e3[...])
    if not prompt:
        t.aux[c] = jnp.exp(last).T
    acs2 = acs * LOG2E
    t.hd[c, 0] = acs2
    t.hd[c, 1] = (acs2 - jnp.log2(dtc)).T


def _ssd_intra(w, t, c, QL, prompt):
    row_start = c * CHUNK
    rows = pl.ds(row_start, CHUNK)
    lane = lax.broadcasted_iota(jnp.int32, (CHUNK, LANES), 1)
    row = lax.broadcasted_iota(jnp.int32, (CHUNK, LANES), 0)
    acs2, shifted_t = t.hd[c, 0], t.hd[c, 1]
    ql_bits = QL.bit_length() - 1
    mask = ((lax.shift_right_logical(row, ql_bits) == lax.shift_right_logical(lane, ql_bits))
            & (lane <= row))
    for g in range(N_BC_GROUPS):
        gs = slice(g * D_STATE, (g + 1) * D_STATE)
        gc = slice(g * GROUP_COLS, (g + 1) * GROUP_COLS)
        cg = t.c[rows, gs].astype(BF16)
        cb = _dot_nt(cg, t.b[rows, gs].astype(BF16)).astype(BF16)
        if prompt:
            ht = t.aux[:, gc]
            t.yoff[...] = _dot(cg, ht.astype(BF16))
            xw = (t.xs[rows, gc] * t.we[rows, gc]).astype(BF16)
            upd = _dot(t.b[rows, gs].T.astype(BF16), xw)
            t.aux[:, gc] = ht * t.ea[row_start + CHUNK - 1:row_start + CHUNK, gc] + upd
        for k in range(HEADS_PER_GROUP // 2):
            atts = []
            for h in (g * HEADS_PER_GROUP + 2 * k, g * HEADS_PER_GROUP + 2 * k + 1):
                seg2 = acs2[:, h:h + 1] - shifted_t[h:h + 1, :]
                decay_dt = jnp.exp2(jnp.where(mask, seg2, -jnp.inf))
                atts.append(cb * decay_dt.astype(BF16))
            ps = slice((g * HEADS_PER_GROUP + 2 * k) * HEAD_DIM,
                       (g * HEADS_PER_GROUP + 2 * k + 2) * HEAD_DIM)
            xp = t.xs[rows, ps]
            rhs = jnp.concatenate([jnp.where(lane < HEAD_DIM, xp, 0.0).astype(BF16),
                                   jnp.where(lane >= HEAD_DIM, xp, 0.0).astype(BF16)], axis=0)
            yp = _dot(jnp.concatenate(atts, axis=1), rhs) + w.dskip[:, ps] * xp
            if prompt:
                yp = yp + t.yoff[:, 2 * k * HEAD_DIM:(2 * k + 2) * HEAD_DIM] * t.ea[rows, ps]
            t.yssd[rows, ps] = yp


def _sample_state_pair(t, r, QL, hins, houts):
    n = 2 * QL
    c = r // CHUNK
    r0 = r % CHUNK
    rows = pl.ds(pl.multiple_of(r, n), n)
    lane = lax.broadcasted_iota(jnp.int32, (CHUNK, LANES), 1)
    rowi = lax.broadcasted_iota(jnp.int32, (n, GROUP_COLS), 0)
    total_t = t.aux[c]
    for q in range(2):
        col = jnp.sum(jnp.where(lane == r0 + q * QL, total_t, 0.0), axis=1, keepdims=True)
        rb = jnp.broadcast_to(col, (CHUNK, LANES))
        mine = (rowi >= q * QL) & (rowi < (q + 1) * QL)
        for g in range(N_BC_GROUPS):
            gs = slice(g * D_STATE, (g + 1) * D_STATE)
            cs = slice(g * GROUP_COLS, (g + 1) * GROUP_COLS)
            hg = hins[q][cs, :]
            yoff = _dot_nt(t.c[rows, gs].astype(BF16), hg.astype(BF16)) * t.ea[rows, cs]
            t.yssd[rows, cs] = t.yssd[rows, cs] + jnp.where(mine, yoff, 0.0)
            xw = jnp.where(mine, t.xs[rows, cs] * t.we[rows, cs], 0.0)
            upd = _dot(xw.T.astype(BF16), t.b[rows, gs].astype(BF16))
            for e in range(HEADS_PER_GROUP):
                h = g * HEADS_PER_GROUP + e
                hs = slice(h * HEAD_DIM, (h + 1) * HEAD_DIM)
                es = slice(e * HEAD_DIM, (e + 1) * HEAD_DIM)
                houts[q][hs, :] = hg[es, :] * rb[h:h + 1, :] + upd[es, :]


def _dense_out(w, t, x_ref, gate_of, y_ref, S, L):
    R = S * L
    col_chunks = [slice(j, j + DOT_COLS) for j in range(0, D_INNER, DOT_COLS)]
    inv_rms = []
    for r in range(0, R, ROW_BLOCK):
        rs = slice(r, r + ROW_BLOCK)
        ssq = None
        for sl in col_chunks:
            y = t.yssd[rs, sl] * _silu(t.zs[rs, sl])
            t.yn[rs, sl] = (y * w.ssmg[:, sl]).astype(BF16)
            part = jnp.sum(y * y, axis=-1, keepdims=True)
            ssq = part if ssq is None else ssq + part
        inv_rms.append(lax.rsqrt(ssq * (1.0 / D_INNER) + EPS))
    for j in range(0, D_MODEL, DOT_COLS):
        t.s[:, j:j + DOT_COLS] = _dot(t.yn[...], w.wso[:, j:j + DOT_COLS])
    for r in range(0, R, ROW_BLOCK):
        rs = slice(r, r + ROW_BLOCK)
        for j in range(0, D_MODEL, DOT_COLS):
            sl = slice(j, j + DOT_COLS)
            sl2 = slice(D_MODEL + j, D_MODEL + j + DOT_COLS)
            t.yn[rs, sl] = (_sigmoid(t.g[rs, sl]) * t.p[rs, sl]
                            + _sigmoid(t.g[rs, sl2]) * (t.s[rs, sl] * inv_rms[r // ROW_BLOCK])
                            ).astype(BF16)
    for j in range(0, D_MODEL, DOT_COLS):
        t.s[:, j:j + DOT_COLS] = _dot(t.yn[:, 0:D_MODEL], w.wo[:, j:j + DOT_COLS])
    for ss, ls, rs in _row_blocks(S, L, ROW_BLOCK):
        o3 = t.s[rs, :].reshape(ss.stop - ss.start, ls.stop - ls.start, D_MODEL)
        xn = x_ref[ss, ls, :] + gate_of(ss) * o3
        ms = jnp.mean(xn * xn, axis=-1, keepdims=True)
        y_ref[ss, ls, :] = xn * lax.rsqrt(ms + EPS) * w.fg[...]


def _ssd_refs(it):
    names = ("xs", "b", "c", "yssd", "pm", "p", "ea", "we", "hd", "aux", "yn", "s", "zs", "g")
    return types.SimpleNamespace(**{n: next(it) for n in names})


def _input_projection(w, h_ref, stores):
    for col0, (ncols, store) in stores.items():
        for col, wd, j in _proj_chunks(col0, ncols):
            if col < COL_DT:
                wc = w.wmain[:, col:col + wd]
            else:
                wc = w.wtail[:, col - COL_DT:col - COL_DT + wd]
            store(j, wd, _dot(h_ref[...], wc))


def _prompt_kernel(*refs, L):
    S = 1
    it = iter(refs)
    x_ref, mod_ref = next(it), next(it)
    w = _weight_refs(it)
    y_ref, pool_out_ref, conv_out_ref, ssm_out_ref = [next(it) for _ in range(4)]
    h_scr, up_scr, zp_scr, xbc_scr, dt_scr = [next(it) for _ in range(5)]
    t = _ssd_refs(it)
    t.yoff = next(it)
    step = pl.program_id(1)

    @pl.when(step == 0)
    def _():
        up_scr[:, 0:POOL_HDR, :] = jnp.zeros((S, POOL_HDR, D_POOL), F32)
        xbc_scr[:, 0:CONV_HDR, :] = jnp.zeros((S, CONV_HDR, D_XBC), F32)
        t.aux[...] = jnp.zeros(t.aux.shape, F32)

    _norm_modulate(w, x_ref, lambda ss: mod_ref[...], h_scr, S, L)

    def store_up(j, wd, v):
        up_scr[:, POOL_HDR:POOL_HDR + L, j:j + wd] = v.reshape(S, L, wd)

    def store_zp(j, wd, v):
        zp_scr[:, j:j + wd] = v

    def store_zs(j, wd, v):
        t.zs[:, j:j + wd] = v

    def store_xbc(j, wd, v):
        xbc_scr[:, CONV_HDR:CONV_HDR + L, j:j + wd] = v.reshape(S, L, wd)

    def store_dt(j, wd, v):
        dt_scr[...] = _softplus(v + w.dtb[...])

    def store_g(j, wd, v):
        t.g[:, j:j + wd] = v

    _input_projection(w, h_scr, {COL_XBC: (D_XBC, store_xbc)})

    for j in range(0, D_XBC, DOT_COLS):
        _conv_chunk(w, xbc_scr, t, j, S, L)
    xbc_scr[:, 0:CONV_HDR, :] = xbc_scr[:, L:L + CONV_HDR, :]

    _input_projection(w, h_scr, {
        COL_UP: (D_POOL, store_up), COL_ZP: (D_POOL, store_zp), COL_DT: (LANES, store_dt),
        COL_ZS: (D_INNER, store_zs), COL_G: (2 * D_MODEL, store_g)})

    _pool_windows(up_scr, t, step * L, S, L)
    _pool_mix(w, zp_scr, t)
    up_scr[:, 0:POOL_HDR, :] = up_scr[:, L:L + POOL_HDR, :]

    for c in range(L // CHUNK):
        _ssd_prep(w, t, dt_scr, c, True)
    for c in range(L // CHUNK):
        _ssd_intra(w, t, c, CHUNK, True)

    _dense_out(w, t, x_ref, lambda ss: mod_ref[...][:, :, 2 * D_MODEL:3 * D_MODEL], y_ref, S, L)

    @pl.when(step == pl.num_programs(1) - 1)
    def _():
        for n in range(D_INNER // LANES):
            cols = slice(n * LANES, (n + 1) * LANES)
            ssm_out_ref[0, cols, :] = t.aux[:, cols].T
        batch = pl.ds(pl.program_id(0), 1)
        for r in range(POOL_BUF):
            pool_out_ref[r, batch, :] = up_scr[0, 1 + r:2 + r, :]
        for r in range(CONV_W - 1):
            r0 = CONV_HDR - (CONV_W - 1) + r
            conv_out_ref[r, batch, :] = xbc_scr[0, r0:r0 + 1, :]


def _sample_kernel(*refs, S, L, n_inner, seq_per_inner):
    R = S * L
    it = iter(refs)
    x_ref, mod_ref, pool_in_ref, conv_in_ref, ssm_in_ref = [next(it) for _ in range(5)]
    w = _weight_refs(it)
    y_ref, pool_out_ref, conv_out_ref, ssm_out_ref = [next(it) for _ in range(4)]
    h_scr, up_scr, zp_scr, xbc_scr, dt_scr = [next(it) for _ in range(5)]
    t = _ssd_refs(it)
    step = pl.program_id(1)

    @pl.when(step == 0)
    def _():
        _norm_modulate(w, x_ref, lambda ss: mod_ref[ss], h_scr, S, L)

        def store_up(j, wd, v):
            up_scr[:, POOL_HDR:POOL_HDR + L, j:j + wd] = v.reshape(S, L, wd)

        def store_zp(j, wd, v):
            zp_scr[:, j:j + wd] = v

        def store_zs(j, wd, v):
            t.zs[:, j:j + wd] = v

        def store_xbc(j, wd, v):
            xbc_scr[:, CONV_HDR:CONV_HDR + L, j:j + wd] = v.reshape(S, L, wd)

        def store_dt(j, wd, v):
            dt_scr[...] = _softplus(v + w.dtb[...])

        def store_g(j, wd, v):
            t.g[:, j:j + wd] = v

        _input_projection(w, h_scr, {
            COL_UP: (D_POOL, store_up), COL_ZP: (D_POOL, store_zp), COL_DT: (LANES, store_dt),
            COL_XBC: (D_XBC, store_xbc), COL_ZS: (D_INNER, store_zs),
            COL_G: (2 * D_MODEL, store_g)})

        for r in range(POOL_BUF):
            up_scr[:, 1 + r, :] = pool_in_ref[r]
        _pool_windows(up_scr, t, PAST_LEN, S, L)
        _pool_mix(w, zp_scr, t)
        for r in range(POOL_BUF):
            pool_out_ref[r] = up_scr[:, L + 1 + r, :]

        for r in range(CONV_W - 1):
            xbc_scr[:, CONV_HDR - (CONV_W - 1) + r, :] = conv_in_ref[r]
        for j in range(0, D_XBC, DOT_COLS):
            _conv_chunk(w, xbc_scr, t, j, S, L)
        for r in range(CONV_W - 1):
            conv_out_ref[r] = xbc_scr[:, L + CONV_HDR - (CONV_W - 1) + r, :]
        for c in range(R // CHUNK):
            _ssd_prep(w, t, dt_scr, c, False)
        for c in range(R // CHUNK):
            _ssd_intra(w, t, c, L, False)

    for q in range(0, seq_per_inner, 2):
        _sample_state_pair(t, (step * seq_per_inner + q) * L, L,
                           (ssm_in_ref.at[q], ssm_in_ref.at[q + 1]),
                           (ssm_out_ref.at[q], ssm_out_ref.at[q + 1]))

    @pl.when(step == n_inner - 1)
    def _():
        _dense_out(w, t, x_ref, lambda ss: mod_ref[ss][:, :, 2 * D_MODEL:3 * D_MODEL],
                   y_ref, S, L)


def _resident(shape):
    zeros = (0,) * len(shape)
    return pl.BlockSpec(shape, lambda *_: zeros, pipeline_mode=pl.Buffered(1))


def _ssd_scratch(R, aux_shape):
    return [
        pltpu.VMEM((R, D_INNER), F32),
        pltpu.VMEM((R, N_BC_GROUPS * D_STATE), F32),
        pltpu.VMEM((R, N_BC_GROUPS * D_STATE), F32),
        pltpu.VMEM((R, D_INNER), F32),
        pltpu.VMEM((R, D_POOL), BF16),
        pltpu.VMEM((R, D_MODEL), F32),
        pltpu.VMEM((R, D_INNER), F32),
        pltpu.VMEM((R, D_INNER), F32),
        pltpu.VMEM((R // CHUNK, 2, CHUNK, LANES), F32),
        pltpu.VMEM(aux_shape, F32),
        pltpu.VMEM((R, D_INNER), BF16),
        pltpu.VMEM((R, D_MODEL), F32),
        pltpu.VMEM((R, D_INNER), F32),
        pltpu.VMEM((R, 2 * D_MODEL), F32),
    ]


def _stage_scratch(S, L):
    return [
        pltpu.VMEM((S * L, D_MODEL), BF16),
        pltpu.VMEM((S, POOL_HDR + L, D_POOL), F32),
        pltpu.VMEM((S * L, D_POOL), F32),
        pltpu.VMEM((S, CONV_HDR + L, D_XBC), F32),
        pltpu.VMEM((S * L, LANES), F32),
    ]


def _prompt_call(x, mod, weights, *, L):
    nb, seq, _ = x.shape
    x_spec = pl.BlockSpec((1, L, D_MODEL), lambda b, t: (b, t, 0))
    seq_map = lambda b, t: (b, 0, 0)
    in_specs = ([x_spec, pl.BlockSpec((1, 1, 3 * D_MODEL), seq_map)]
                + [_resident(v.shape) for v in weights])
    whole = lambda b, t: (0, 0, 0)
    out_shape = (jax.ShapeDtypeStruct(x.shape, F32),
                 jax.ShapeDtypeStruct((POOL_BUF, nb, D_POOL), F32),
                 jax.ShapeDtypeStruct((CONV_W - 1, nb, D_XBC), F32),
                 jax.ShapeDtypeStruct((nb, D_INNER, D_STATE), F32))
    out_specs = (x_spec,
                 pl.BlockSpec((POOL_BUF, nb, D_POOL), whole),
                 pl.BlockSpec((CONV_W - 1, nb, D_XBC), whole),
                 pl.BlockSpec((1, D_INNER, D_STATE), seq_map))
    scratch = (_stage_scratch(1, L) + _ssd_scratch(L, (D_STATE, D_INNER))
               + [pltpu.VMEM((CHUNK, GROUP_COLS), F32)])
    kern = functools.partial(_prompt_kernel, L=L)
    return pl.pallas_call(
        kern, grid=(nb, seq // L), in_specs=in_specs, out_specs=out_specs, out_shape=out_shape,
        scratch_shapes=scratch, name="layer_prompt",
        compiler_params=pltpu.CompilerParams(
            dimension_semantics=("arbitrary", "arbitrary"), vmem_limit_bytes=VMEM_LIMIT_BYTES),
    )(x, mod, *weights)


def _sample_call(x, mod, states, weights, *, S, seq_per_inner):
    ns, L, _ = x.shape
    R = S * L
    n_inner = S // seq_per_inner
    seq_map = lambda o, i: (o, 0, 0)
    st_map = lambda o, i: (o * n_inner + i, 0, 0)
    time_major = lambda o, i: (0, o, 0)
    x_spec = pl.BlockSpec((S, L, D_MODEL), seq_map)
    pool_spec = pl.BlockSpec((POOL_BUF, S, D_POOL), time_major)
    conv_spec = pl.BlockSpec((CONV_W - 1, S, D_XBC), time_major)
    st_spec = pl.BlockSpec((seq_per_inner, D_INNER, D_STATE), st_map)
    in_specs = ([x_spec, pl.BlockSpec((S, 1, 3 * D_MODEL), seq_map), pool_spec, conv_spec, st_spec]
                + [_resident(v.shape) for v in weights])
    out_shape = (jax.ShapeDtypeStruct(x.shape, F32),
                 jax.ShapeDtypeStruct((POOL_BUF, ns, D_POOL), F32),
                 jax.ShapeDtypeStruct((CONV_W - 1, ns, D_XBC), F32),
                 jax.ShapeDtypeStruct((ns, D_INNER, D_STATE), F32))
    out_specs = (x_spec, pool_spec, conv_spec, st_spec)
    scratch = _stage_scratch(S, L) + _ssd_scratch(R, (R // CHUNK, LANES, CHUNK))
    kern = functools.partial(_sample_kernel, S=S, L=L, n_inner=n_inner,
                             seq_per_inner=seq_per_inner)
    return pl.pallas_call(
        kern, grid=(ns // S, n_inner), in_specs=in_specs, out_specs=out_specs,
        out_shape=out_shape, scratch_shapes=scratch, name="layer_sample",
        compiler_params=pltpu.CompilerParams(
            dimension_semantics=("arbitrary", "arbitrary"), vmem_limit_bytes=VMEM_LIMIT_BYTES),
    )(x, mod, *states, *weights)


def _block_sum_matrix(ql):
    r = jnp.arange(CHUNK)[:, None]
    c = jnp.arange(CHUNK)[None, :]
    same = (r // ql) == (c // ql)
    return jnp.concatenate([same & (c <= r), same], axis=0).astype(BF16)


def _expand_matrix():
    k = jnp.arange(LANES)[:, None]
    c = jnp.arange(D_INNER)[None, :] // HEAD_DIM
    return ((k % N_HEADS == c) & (k < 3 * N_HEADS)).astype(BF16)


def kernel(x_prompt, x_sample, state_pool, state_conv, state_ssm, c_prompt, c_sample, w_ada, b_ada, norm_g, w_in, conv_w, conv_b, dt_bias, a_log, d_skip, ssm_norm_g, pool_w, pool_scale, w_pool_out, w_ssm_out, w_o, final_g):
    depth = w_ada.shape[0]
    assert depth == 1, "kernel is written for a single layer"
    nb, seq, _ = x_prompt.shape
    ns, dec_seq, _ = x_sample.shape
    l = 0

    def head_lanes(v):
        return jnp.pad(jnp.tile(v, (1, 3)), ((0, 0), (0, LANES - 3 * N_HEADS)))

    c_all = jnp.concatenate([c_prompt, c_sample], axis=0)
    mod_p, mod_s = pl.pallas_call(
        _adaln_kernel, grid=(3,),
        in_specs=[pl.BlockSpec((nb + ns, D_MODEL), lambda j: (0, 0)),
                  pl.BlockSpec((D_MODEL, D_MODEL), lambda j: (0, j)),
                  pl.BlockSpec((1, D_MODEL), lambda j: (0, j))],
        out_specs=(pl.BlockSpec((nb, 1, D_MODEL), lambda j: (0, 0, j)),
                   pl.BlockSpec((ns, 1, D_MODEL), lambda j: (0, 0, j))),
        out_shape=(jax.ShapeDtypeStruct((nb, 1, 3 * D_MODEL), F32),
                   jax.ShapeDtypeStruct((ns, 1, 3 * D_MODEL), F32)), name="adaln",
    )(c_all, w_ada[l], b_ada[l][None, :])

    e3 = _expand_matrix()
    w_main, w_tail = _repack_w_in(jnp.transpose(w_in[l]))

    def weights(ql):
        return [
            norm_g[l][None, :],
            w_main, w_tail,
            conv_w[l], conv_b[l][None, :],
            head_lanes(dt_bias[l][None, :]),
            head_lanes(a_log[l][None, :]),
            jnp.repeat(d_skip[l], HEAD_DIM)[None, :],
            ssm_norm_g[l][None, :],
            pool_w[l].astype(BF16), pool_scale[l][None, :],
            w_pool_out[l].astype(BF16), w_ssm_out[l].astype(BF16), w_o[l].astype(BF16),
            final_g[None, :],
            _block_sum_matrix(ql), e3,
        ]

    tile_rows = 2 * CHUNK
    y_p, pool_p, conv_p, ssm_p = _prompt_call(x_prompt, mod_p, weights(CHUNK), L=tile_rows)

    swap = lambda v: jnp.transpose(v, (1, 0, 2))
    states = (swap(state_pool[l]), swap(state_conv[l]),
              state_ssm[l].reshape(ns, D_INNER, D_STATE))
    y_s, pool_s, conv_s, ssm_s = _sample_call(
        x_sample, mod_s, states, weights(dec_seq), S=CHUNK // dec_seq, seq_per_inner=4)

    shp = (N_HEADS, HEAD_DIM, D_STATE)
    return (y_p, y_s, swap(pool_p)[None], swap(conv_p)[None], ssm_p.reshape((1, nb) + shp),
            swap(pool_s)[None], swap(conv_s)[None], ssm_s.reshape((1, ns) + shp))
```

```python
import functools
import types

import jax
import jax.numpy as jnp
from jax import lax
from jax.experimental import pallas as pl
from jax.experimental.pallas import tpu as pltpu

F32 = jnp.float32
BF16 = jnp.bfloat16

EPS = 1e-6
D_MODEL = 1024
POOL_WINDOWS = (2, 4, 8, 16)
POOL_GROUP = 128
D_POOL = POOL_GROUP * len(POOL_WINDOWS)
POOL_BUF = max(POOL_WINDOWS) - 1
D_INNER = 2048
HEAD_DIM = 64
N_HEADS = D_INNER // HEAD_DIM
N_BC_GROUPS = 4
HEADS_PER_GROUP = N_HEADS // N_BC_GROUPS
GROUP_COLS = HEADS_PER_GROUP * HEAD_DIM
D_STATE = 128
CONV_W = 4
D_XBC = D_INNER + 2 * N_BC_GROUPS * D_STATE
PAST_LEN = 16384

LANES = 128
CHUNK = 128
POOL_HDR = 16
CONV_HDR = 8
DOT_COLS = 512
ROW_BLOCK = 16
CONV_ROW_BLOCK = 32
VMEM_LIMIT_BYTES = 63 * 1024 * 1024

COL_UP = 0
COL_ZP = D_POOL
COL_ZS = 2 * D_POOL
COL_XBC = COL_ZS + D_INNER
COL_DT = COL_XBC + D_XBC
COL_G = COL_DT + LANES
N_WEIGHTS = 17
LOG2E = 1.4426950408889634


def _dot(a, b):
    return jnp.dot(a, b, preferred_element_type=F32)


def _dot_nt(a, b):
    return lax.dot_general(a, b, (((1,), (1,)), ((), ())), preferred_element_type=F32)


def _pack3(v, lane):
    hi = v.astype(BF16).astype(F32)
    r = v - hi
    mid = r.astype(BF16).astype(F32)
    lo = r - mid
    packed = jnp.where(lane < N_HEADS, hi,
                       jnp.where(lane < 2 * N_HEADS, mid,
                                 jnp.where(lane < 3 * N_HEADS, lo, 0.0)))
    return packed.astype(BF16)


def _silu(v):
    return v * _sigmoid(v)


def _sigmoid(v):
    return pl.reciprocal(1.0 + jnp.exp2(v * -LOG2E), approx=True)


def _softplus(v):
    return jnp.maximum(v, 0.0) + jnp.log1p(jnp.exp(-jnp.abs(v)))


def _repack_main_kernel(x_ref, o_ref):
    o_ref[...] = x_ref[...].T.astype(o_ref.dtype)


def _repack_tail_kernel(dt_ref, g_ref, o_ref, *, g_row0):
    dt = dt_ref[...]
    heads = jnp.concatenate([dt, dt, dt, jnp.zeros((LANES - 3 * N_HEADS, D_MODEL), F32)], axis=0)
    o_ref[:, 0:LANES] = heads.T.astype(o_ref.dtype)
    for j in range(0, 2 * D_MODEL, DOT_COLS):
        o_ref[:, LANES + j:LANES + j + DOT_COLS] = (
            g_ref[g_row0 + j:g_row0 + j + DOT_COLS, :].T.astype(o_ref.dtype))


def _repack_w_in(w_in_t):
    n_in, d = w_in_t.shape
    assert d == D_MODEL and n_in == COL_DT + N_HEADS + 2 * D_MODEL
    w_main = pl.pallas_call(
        _repack_main_kernel, grid=(COL_DT // D_MODEL,),
        in_specs=[pl.BlockSpec((D_MODEL, D_MODEL), lambda j: (j, 0))],
        out_specs=pl.BlockSpec((D_MODEL, D_MODEL), lambda j: (0, j)),
        out_shape=jax.ShapeDtypeStruct((D_MODEL, COL_DT), BF16), name="repack_w_main",
    )(w_in_t)
    g_block = n_in // 4
    g_row = COL_DT + N_HEADS
    g_row0 = g_row % g_block
    assert n_in % g_block == 0 and g_block % 8 == 0 and g_row0 % 8 == 0
    assert g_row0 + 2 * D_MODEL <= g_block and COL_DT % N_HEADS == 0
    w_tail = pl.pallas_call(
        functools.partial(_repack_tail_kernel, g_row0=g_row0), grid=(1,),
        in_specs=[pl.BlockSpec((N_HEADS, D_MODEL), lambda i: (COL_DT // N_HEADS, 0)),
                  pl.BlockSpec((g_block, D_MODEL), lambda i: (g_row // g_block, 0))],
        out_specs=pl.BlockSpec((D_MODEL, LANES + 2 * D_MODEL), lambda i: (0, 0)),
        out_shape=jax.ShapeDtypeStruct((D_MODEL, LANES + 2 * D_MODEL), BF16), name="repack_w_tail",
        compiler_params=pltpu.CompilerParams(vmem_limit_bytes=VMEM_LIMIT_BYTES),
    )(w_in_t, w_in_t)
    return w_main, w_tail


def _adaln_kernel(c_ref, w_ref, b_ref, op_ref, os_ref):
    c = c_ref[...]
    mod = _dot(_silu(c).astype(BF16), w_ref[...].astype(BF16)) + b_ref[...]
    n_prompt = op_ref.shape[0]
    op_ref[:, 0, :] = mod[0:n_prompt, :]
    os_ref[:, 0, :] = mod[n_prompt:, :]


def _weight_refs(it):
    names = ("normg", "wmain", "wtail", "convw", "convb", "dtb", "alog", "dskip", "ssmg", "poolw",
             "pscale", "wpo", "wso", "wo", "fg", "ta", "e3")
    assert len(names) == N_WEIGHTS
    return types.SimpleNamespace(**{n: next(it) for n in names})


def _row_blocks(S, L, nrows):
    if L >= nrows:
        return [(slice(s, s + 1), slice(l0, l0 + nrows), slice(s * L + l0, s * L + l0 + nrows))
                for s in range(S) for l0 in range(0, L, nrows)]
    sb = nrows // L
    return [(slice(s0, s0 + sb), slice(0, L), slice(s0 * L, (s0 + sb) * L))
            for s0 in range(0, S, sb)]


def _norm_modulate(w, x_ref, mod_of, h_ref, S, L):
    for ss, ls, rs in _row_blocks(S, L, ROW_BLOCK):
        x3 = x_ref[ss, ls, :]
        mod3 = mod_of(ss)
        ms = jnp.mean(x3 * x3, axis=-1, keepdims=True)
        hn = x3 * lax.rsqrt(ms + EPS) * w.normg[...]
        hm = hn * (1.0 + mod3[:, :, D_MODEL:2 * D_MODEL]) + mod3[:, :, 0:D_MODEL]
        h_ref[rs, :] = hm.reshape(rs.stop - rs.start, D_MODEL).astype(BF16)


def _roll_rows(x, d, drop_first=False):
    n, rows, lanes = x.shape
    tiles = [x[:, j:j + 8, :] for j in range(0, rows, 8)]
    first = 1 if drop_first else 0
    if d == 8:
        out = [tiles[j - 1] for j in range(first, len(tiles))]
    else:
        rolled = [pltpu.roll(tl, d, 1) for tl in tiles]
        sub = lax.broadcasted_iota(jnp.int32, (n, 8, lanes), 1)
        out = [jnp.where(sub < d, rolled[j - 1], rolled[j]) for j in range(first, len(tiles))]
    return jnp.concatenate(out, axis=1)


def _proj_chunks(col0, ncols):
    return [(col0 + j, min(DOT_COLS, ncols - j), j) for j in range(0, ncols, DOT_COLS)]


def _pool_windows(up3, t, pos0, S, L):
    for ss, ls, rs in _row_blocks(S, L, ROW_BLOCK):
        nseq, ntok = ss.stop - ss.start, ls.stop - ls.start
        pos = pos0 + ls.start + lax.broadcasted_iota(jnp.int32, (nseq, ntok, POOL_GROUP), 1)
        for gi, win in enumerate(POOL_WINDOWS):
            sl = slice(gi * POOL_GROUP, (gi + 1) * POOL_GROUP)
            s = up3[ss, ls.start:POOL_HDR + ls.stop, sl]
            u = s[:, POOL_HDR:, :]
            shift = 1
            while shift < win:
                s = s + _roll_rows(s, shift)
                shift *= 2
            cnt = jnp.minimum(pos + 1, win).astype(F32)
            t.pm[rs, sl] = (s[:, POOL_HDR:, :] / cnt - u).reshape(
                nseq * ntok, POOL_GROUP).astype(BF16)


def _pool_mix(w, zp, t):
    for gi in range(len(POOL_WINDOWS)):
        sl = slice(gi * POOL_GROUP, (gi + 1) * POOL_GROUP)
        yg = _dot(t.pm[:, sl], w.poolw[gi]) * w.pscale[:, sl]
        t.pm[:, sl] = (yg * _silu(zp[:, sl])).astype(BF16)
    for j in range(0, D_MODEL, DOT_COLS):
        t.p[:, j:j + DOT_COLS] = _dot(t.pm[...], w.wpo[:, j:j + DOT_COLS])


def _conv_chunk(w, xbc3, t, j, S, L):
    sl = slice(j, j + DOT_COLS)
    tile_of = lambda row: jnp.broadcast_to(row[None], (1, 8, DOT_COLS))
    taps_w = [tile_of(w.convw[k:k + 1, sl]) for k in range(CONV_W)]
    bias = tile_of(w.convb[:, sl])
    for ss, ls, rs in _row_blocks(S, L, CONV_ROW_BLOCK):
        ext = xbc3[ss, ls.start:CONV_HDR + ls.stop, sl]
        taps = [ext[:, CONV_HDR:, :] if k == CONV_W - 1
                else _roll_rows(ext, CONV_W - 1 - k, drop_first=True) for k in range(CONV_W)]
        out_tiles = []
        for r in range(0, ls.stop - ls.start, 8):
            acc = bias
            for k in range(CONV_W):
                acc = acc + taps[k][:, r:r + 8, :] * taps_w[k]
            out_tiles.append(_silu(acc))
        v = jnp.concatenate(out_tiles, axis=1).reshape(rs.stop - rs.start, DOT_COLS)
        if j < D_INNER:
            t.xs[rs, sl] = v
        elif j < D_INNER + N_BC_GROUPS * D_STATE:
            t.b[rs, :] = v
        else:
            t.c[rs, :] = v


def _ssd_prep(w, t, dt_ref, c, prompt):
    rows = pl.ds(c * CHUNK, CHUNK)
    lane = lax.broadcasted_iota(jnp.int32, (CHUNK, LANES), 1)
    neg_a = jnp.where(lane[0:1, :] < 3 * N_HEADS, -jnp.exp(w.alog[...]), 0.0)
    dtc = dt_ref[rows, :]
    a = dtc * neg_a
    cs = _dot(w.ta[...], _pack3(a, lane))
    cs = cs + pltpu.roll(cs, 2 * N_HEADS, 1)
    cs = cs + pltpu.roll(cs, N_HEADS, 1)
    acs = cs[0:CHUNK, :]
    last = cs[CHUNK:2 * CHUNK, :]
    exp_acs = jnp.exp(acs)
    wgt = jnp.exp(last - acs) * dtc
    t.ea[rows, :] = _dot(_pack3(exp_acs, lane), w.e3[...])
    t.we[rows, :] = _dot(_pack3(wgt, lane), w.e3[...])
    if not prompt:
        t.aux[c] = jnp.exp(last).T
    acs2 = acs * LOG2E
    t.hd[c, 0] = acs2
    t.hd[c, 1] = (acs2 - jnp.log2(dtc)).T


def _ssd_intra(w, t, c, QL, prompt):
    row_start = c * CHUNK
    rows = pl.ds(row_start, CHUNK)
    lane = lax.broadcasted_iota(jnp.int32, (CHUNK, LANES), 1)
    row = lax.broadcasted_iota(jnp.int32, (CHUNK, LANES), 0)
    acs2, shifted_t = t.hd[c, 0], t.hd[c, 1]
    ql_bits = QL.bit_length() - 1
    mask = ((lax.shift_right_logical(row, ql_bits) == lax.shift_right_logical(lane, ql_bits))
            & (lane <= row))
    for g in range(N_BC_GROUPS):
        gs = slice(g * D_STATE, (g + 1) * D_STATE)
        gc = slice(g * GROUP_COLS, (g + 1) * GROUP_COLS)
        cg = t.c[rows, gs].astype(BF16)
        cb = _dot_nt(cg, t.b[rows, gs].astype(BF16)).astype(BF16)
        if prompt:
            ht = t.aux[:, gc]
            t.yoff[...] = _dot(cg, ht.astype(BF16))
            xw = (t.xs[rows, gc] * t.we[rows, gc]).astype(BF16)
            upd = _dot(t.b[rows, gs].T.astype(BF16), xw)
            t.aux[:, gc] = ht * t.ea[row_start + CHUNK - 1:row_start + CHUNK, gc] + upd
        for k in range(HEADS_PER_GROUP // 2):
            atts = []
            for h in (g * HEADS_PER_GROUP + 2 * k, g * HEADS_PER_GROUP + 2 * k + 1):
                seg2 = acs2[:, h:h + 1] - shifted_t[h:h + 1, :]
                decay_dt = jnp.exp2(jnp.where(mask, seg2, -jnp.inf))
                atts.append(cb * decay_dt.astype(BF16))
            ps = slice((g * HEADS_PER_GROUP + 2 * k) * HEAD_DIM,
                       (g * HEADS_PER_GROUP + 2 * k + 2) * HEAD_DIM)
            xp = t.xs[rows, ps]
            rhs = jnp.concatenate([jnp.where(lane < HEAD_DIM, xp, 0.0).astype(BF16),
                                   jnp.where(lane >= HEAD_DIM, xp, 0.0).astype(BF16)], axis=0)
            yp = _dot(jnp.concatenate(atts, axis=1), rhs) + w.dskip[:, ps] * xp
            if prompt:
                yp = yp + t.yoff[:, 2 * k * HEAD_DIM:(2 * k + 2) * HEAD_DIM] * t.ea[rows, ps]
            t.yssd[rows, ps] = yp


def _sample_state_pair(t, r, QL, hins, houts):
    n = 2 * QL
    c = r // CHUNK
    r0 = r % CHUNK
    rows = pl.ds(pl.multiple_of(r, n), n)
    lane = lax.broadcasted_iota(jnp.int32, (CHUNK, LANES), 1)
    rowi = lax.broadcasted_iota(jnp.int32, (n, GROUP_COLS), 0)
    total_t = t.aux[c]
    for q in range(2):
        col = jnp.sum(jnp.where(lane == r0 + q * QL, total_t, 0.0), axis=1, keepdims=True)
        rb = jnp.broadcast_to(col, (CHUNK, LANES))
        mine = (rowi >= q * QL) & (rowi < (q + 1) * QL)
        for g in range(N_BC_GROUPS):
            gs = slice(g * D_STATE, (g + 1) * D_STATE)
            cs = slice(g * GROUP_COLS, (g + 1) * GROUP_COLS)
            hg = hins[q][cs, :]
            yoff = _dot_nt(t.c[rows, gs].astype(BF16), hg.astype(BF16)) * t.ea[rows, cs]
            t.yssd[rows, cs] = t.yssd[rows, cs] + jnp.where(mine, yoff, 0.0)
            xw = jnp.where(mine, t.xs[rows, cs] * t.we[rows, cs], 0.0)
            upd = _dot(xw.T.astype(BF16), t.b[rows, gs].astype(BF16))
            for e in range(HEADS_PER_GROUP):
                h = g * HEADS_PER_GROUP + e
                hs = slice(h * HEAD_DIM, (h + 1) * HEAD_DIM)
                es = slice(e * HEAD_DIM, (e + 1) * HEAD_DIM)
                houts[q][hs, :] = hg[es, :] * rb[h:h + 1, :] + upd[es, :]


def _dense_out(w, t, x_ref, gate_of, y_ref, S, L):
    R = S * L
    col_chunks = [slice(j, j + DOT_COLS) for j in range(0, D_INNER, DOT_COLS)]
    inv_rms = []
    for r in range(0, R, ROW_BLOCK):
        rs = slice(r, r + ROW_BLOCK)
        ssq = None
        for sl in col_chunks:
            y = t.yssd[rs, sl] * _silu(t.zs[rs, sl])
            t.yn[rs, sl] = (y * w.ssmg[:, sl]).astype(BF16)
            part = jnp.sum(y * y, axis=-1, keepdims=True)
            ssq = part if ssq is None else ssq + part
        inv_rms.append(lax.rsqrt(ssq * (1.0 / D_INNER) + EPS))
    for j in range(0, D_MODEL, DOT_COLS):
        t.s[:, j:j + DOT_COLS] = _dot(t.yn[...], w.wso[:, j:j + DOT_COLS])
    for r in range(0, R, ROW_BLOCK):
        rs = slice(r, r + ROW_BLOCK)
        for j in range(0, D_MODEL, DOT_COLS):
            sl = slice(j, j + DOT_COLS)
            sl2 = slice(D_MODEL + j, D_MODEL + j + DOT_COLS)
            t.yn[rs, sl] = (_sigmoid(t.g[rs, sl]) * t.p[rs, sl]
                            + _sigmoid(t.g[rs, sl2]) * (t.s[rs, sl] * inv_rms[r // ROW_BLOCK])
                            ).astype(BF16)
    for j in range(0, D_MODEL, DOT_COLS):
        t.s[:, j:j + DOT_COLS] = _dot(t.yn[:, 0:D_MODEL], w.wo[:, j:j + DOT_COLS])
    for ss, ls, rs in _row_blocks(S, L, ROW_BLOCK):
        o3 = t.s[rs, :].reshape(ss.stop - ss.start, ls.stop - ls.start, D_MODEL)
        xn = x_ref[ss, ls, :] + gate_of(ss) * o3
        ms = jnp.mean(xn * xn, axis=-1, keepdims=True)
        y_ref[ss, ls, :] = xn * lax.rsqrt(ms + EPS) * w.fg[...]


def _ssd_refs(it):
    names = ("xs", "b", "c", "yssd", "pm", "p", "ea", "we", "hd", "aux", "yn", "s", "zs", "g")
    return types.SimpleNamespace(**{n: next(it) for n in names})


def _input_projection(w, h_ref, stores):
    for col0, (ncols, store) in stores.items():
        for col, wd, j in _proj_chunks(col0, ncols):
            if col < COL_DT:
                wc = w.wmain[:, col:col + wd]
            else:
                wc = w.wtail[:, col - COL_DT:col - COL_DT + wd]
            store(j, wd, _dot(h_ref[...], wc))


def _prompt_kernel(*refs, L):
    S = 1
    it = iter(refs)
    x_ref, mod_ref = next(it), next(it)
    w = _weight_refs(it)
    y_ref, pool_out_ref, conv_out_ref, ssm_out_ref = [next(it) for _ in range(4)]
    h_scr, up_scr, zp_scr, xbc_scr, dt_scr = [next(it) for _ in range(5)]
    t = _ssd_refs(it)
    t.yoff = next(it)
    step = pl.program_id(1)

    @pl.when(step == 0)
    def _():
        up_scr[:, 0:POOL_HDR, :] = jnp.zeros((S, POOL_HDR, D_POOL), F32)
        xbc_scr[:, 0:CONV_HDR, :] = jnp.zeros((S, CONV_HDR, D_XBC), F32)
        t.aux[...] = jnp.zeros(t.aux.shape, F32)

    _norm_modulate(w, x_ref, lambda ss: mod_ref[...], h_scr, S, L)

    def store_up(j, wd, v):
        up_scr[:, POOL_HDR:POOL_HDR + L, j:j + wd] = v.reshape(S, L, wd)

    def store_zp(j, wd, v):
        zp_scr[:, j:j + wd] = v

    def store_zs(j, wd, v):
        t.zs[:, j:j + wd] = v

    def store_xbc(j, wd, v):
        xbc_scr[:, CONV_HDR:CONV_HDR + L, j:j + wd] = v.reshape(S, L, wd)

    def store_dt(j, wd, v):
        dt_scr[...] = _softplus(v + w.dtb[...])

    def store_g(j, wd, v):
        t.g[:, j:j + wd] = v

    _input_projection(w, h_scr, {COL_XBC: (D_XBC, store_xbc)})

    for j in range(0, D_XBC, DOT_COLS):
        _conv_chunk(w, xbc_scr, t, j, S, L)
    xbc_scr[:, 0:CONV_HDR, :] = xbc_scr[:, L:L + CONV_HDR, :]

    _input_projection(w, h_scr, {
        COL_UP: (D_POOL, store_up), COL_ZP: (D_POOL, store_zp), COL_DT: (LANES, store_dt),
        COL_ZS: (D_INNER, store_zs), COL_G: (2 * D_MODEL, store_g)})

    _pool_windows(up_scr, t, step * L, S, L)
    _pool_mix(w, zp_scr, t)
    up_scr[:, 0:POOL_HDR, :] = up_scr[:, L:L + POOL_HDR, :]

    for c in range(L // CHUNK):
        _ssd_prep(w, t, dt_scr, c, True)
    for c in range(L // CHUNK):
        _ssd_intra(w, t, c, CHUNK, True)

    _dense_out(w, t, x_ref, lambda ss: mod_ref[...][:, :, 2 * D_MODEL:3 * D_MODEL], y_ref, S, L)

    @pl.when(step == pl.num_programs(1) - 1)
    def _():
        for n in range(D_INNER // LANES):
            cols = slice(n * LANES, (n + 1) * LANES)
            ssm_out_ref[0, cols, :] = t.aux[:, cols].T
        batch = pl.ds(pl.program_id(0), 1)
        for r in range(POOL_BUF):
            pool_out_ref[r, batch, :] = up_scr[0, 1 + r:2 + r, :]
        for r in range(CONV_W - 1):
            r0 = CONV_HDR - (CONV_W - 1) + r
            conv_out_ref[r, batch, :] = xbc_scr[0, r0:r0 + 1, :]


def _sample_kernel(*refs, S, L, n_inner, seq_per_inner):
    R = S * L
    it = iter(refs)
    x_ref, mod_ref, pool_in_ref, conv_in_ref, ssm_in_ref = [next(it) for _ in range(5)]
    w = _weight_refs(it)
    y_ref, pool_out_ref, conv_out_ref, ssm_out_ref = [next(it) for _ in range(4)]
    h_scr, up_scr, zp_scr, xbc_scr, dt_scr = [next(it) for _ in range(5)]
    t = _ssd_refs(it)
    step = pl.program_id(1)

    @pl.when(step == 0)
    def _():
        _norm_modulate(w, x_ref, lambda ss: mod_ref[ss], h_scr, S, L)

        def store_up(j, wd, v):
            up_scr[:, POOL_HDR:POOL_HDR + L, j:j + wd] = v.reshape(S, L, wd)

        def store_zp(j, wd, v):
            zp_scr[:, j:j + wd] = v

        def store_zs(j, wd, v):
            t.zs[:, j:j + wd] = v

        def store_xbc(j, wd, v):
            xbc_scr[:, CONV_HDR:CONV_HDR + L, j:j + wd] = v.reshape(S, L, wd)

        def store_dt(j, wd, v):
            dt_scr[...] = _softplus(v + w.dtb[...])

        def store_g(j, wd, v):
            t.g[:, j:j + wd] = v

        _input_projection(w, h_scr, {
            COL_UP: (D_POOL, store_up), COL_ZP: (D_POOL, store_zp), COL_DT: (LANES, store_dt),
            COL_XBC: (D_XBC, store_xbc), COL_ZS: (D_INNER, store_zs),
            COL_G: (2 * D_MODEL, store_g)})

        for r in range(POOL_BUF):
            up_scr[:, 1 + r, :] = pool_in_ref[r]
        _pool_windows(up_scr, t, PAST_LEN, S, L)
        _pool_mix(w, zp_scr, t)
        for r in range(POOL_BUF):
            pool_out_ref[r] = up_scr[:, L + 1 + r, :]

        for r in range(CONV_W - 1):
            xbc_scr[:, CONV_HDR - (CONV_W - 1) + r, :] = conv_in_ref[r]
        for j in range(0, D_XBC, DOT_COLS):
            _conv_chunk(w, xbc_scr, t, j, S, L)
        for r in range(CONV_W - 1):
            conv_out_ref[r] = xbc_scr[:, L + CONV_HDR - (CONV_W - 1) + r, :]
        for c in range(R // CHUNK):
            _ssd_prep(w, t, dt_scr, c, False)
        for c in range(R // CHUNK):
            _ssd_intra(w, t, c, L, False)

    for q in range(0, seq_per_inner, 2):
        _sample_state_pair(t, (step * seq_per_inner + q) * L, L,
                           (ssm_in_ref.at[q], ssm_in_ref.at[q + 1]),
                           (ssm_out_ref.at[q], ssm_out_ref.at[q + 1]))

    @pl.when(step == n_inner - 1)
    def _():
        _dense_out(w, t, x_ref, lambda ss: mod_ref[ss][:, :, 2 * D_MODEL:3 * D_MODEL],
                   y_ref, S, L)


def _resident(shape):
    zeros = (0,) * len(shape)
    return pl.BlockSpec(shape, lambda *_: zeros, pipeline_mode=pl.Buffered(1))


def _ssd_scratch(R, aux_shape):
    return [
        pltpu.VMEM((R, D_INNER), F32),
        pltpu.VMEM((R, N_BC_GROUPS * D_STATE), F32),
        pltpu.VMEM((R, N_BC_GROUPS * D_STATE), F32),
        pltpu.VMEM((R, D_INNER), F32),
        pltpu.VMEM((R, D_POOL), BF16),
        pltpu.VMEM((R, D_MODEL), F32),
        pltpu.VMEM((R, D_INNER), F32),
        pltpu.VMEM((R, D_INNER), F32),
        pltpu.VMEM((R // CHUNK, 2, CHUNK, LANES), F32),
        pltpu.VMEM(aux_shape, F32),
        pltpu.VMEM((R, D_INNER), BF16),
        pltpu.VMEM((R, D_MODEL), F32),
        pltpu.VMEM((R, D_INNER), F32),
        pltpu.VMEM((R, 2 * D_MODEL), F32),
    ]


def _stage_scratch(S, L):
    return [
        pltpu.VMEM((S * L, D_MODEL), BF16),
        pltpu.VMEM((S, POOL_HDR + L, D_POOL), F32),
        pltpu.VMEM((S * L, D_POOL), F32),
        pltpu.VMEM((S, CONV_HDR + L, D_XBC), F32),
        pltpu.VMEM((S * L, LANES), F32),
    ]


def _prompt_call(x, mod, weights, *, L):
    nb, seq, _ = x.shape
    x_spec = pl.BlockSpec((1, L, D_MODEL), lambda b, t: (b, t, 0))
    seq_map = lambda b, t: (b, 0, 0)
    in_specs = ([x_spec, pl.BlockSpec((1, 1, 3 * D_MODEL), seq_map)]
                + [_resident(v.shape) for v in weights])
    whole = lambda b, t: (0, 0, 0)
    out_shape = (jax.ShapeDtypeStruct(x.shape, F32),
                 jax.ShapeDtypeStruct((POOL_BUF, nb, D_POOL), F32),
                 jax.ShapeDtypeStruct((CONV_W - 1, nb, D_XBC), F32),
                 jax.ShapeDtypeStruct((nb, D_INNER, D_STATE), F32))
    out_specs = (x_spec,
                 pl.BlockSpec((POOL_BUF, nb, D_POOL), whole),
                 pl.BlockSpec((CONV_W - 1, nb, D_XBC), whole),
                 pl.BlockSpec((1, D_INNER, D_STATE), seq_map))
    scratch = (_stage_scratch(1, L) + _ssd_scratch(L, (D_STATE, D_INNER))
               + [pltpu.VMEM((CHUNK, GROUP_COLS), F32)])
    kern = functools.partial(_prompt_kernel, L=L)
    return pl.pallas_call(
        kern, grid=(nb, seq // L), in_specs=in_specs, out_specs=out_specs, out_shape=out_shape,
        scratch_shapes=scratch, name="layer_prompt",
        compiler_params=pltpu.CompilerParams(
            dimension_semantics=("arbitrary", "arbitrary"), vmem_limit_bytes=VMEM_LIMIT_BYTES),
    )(x, mod, *weights)


def _sample_call(x, mod, states, weights, *, S, seq_per_inner):
    ns, L, _ = x.shape
    R = S * L
    n_inner = S // seq_per_inner
    seq_map = lambda o, i: (o, 0, 0)
    st_map = lambda o, i: (o * n_inner + i, 0, 0)
    time_major = lambda o, i: (0, o, 0)
    x_spec = pl.BlockSpec((S, L, D_MODEL), seq_map)
    pool_spec = pl.BlockSpec((POOL_BUF, S, D_POOL), time_major)
    conv_spec = pl.BlockSpec((CONV_W - 1, S, D_XBC), time_major)
    st_spec = pl.BlockSpec((seq_per_inner, D_INNER, D_STATE), st_map)
    in_specs = ([x_spec, pl.BlockSpec((S, 1, 3 * D_MODEL), seq_map), pool_spec, conv_spec, st_spec]
                + [_resident(v.shape) for v in weights])
    out_shape = (jax.ShapeDtypeStruct(x.shape, F32),
                 jax.ShapeDtypeStruct((POOL_BUF, ns, D_POOL), F32),
                 jax.ShapeDtypeStruct((CONV_W - 1, ns, D_XBC), F32),
                 jax.ShapeDtypeStruct((ns, D_INNER, D_STATE), F32))
    out_specs = (x_spec, pool_spec, conv_spec, st_spec)
    scratch = _stage_scratch(S, L) + _ssd_scratch(R, (R // CHUNK, LANES, CHUNK))
    kern = functools.partial(_sample_kernel, S=S, L=L, n_inner=n_inner,
                             seq_per_inner=seq_per_inner)
    return pl.pallas_call(
        kern, grid=(ns // S, n_inner), in_specs=in_specs, out_specs=out_specs,
        out_shape=out_shape, scratch_shapes=scratch, name="layer_sample",
        compiler_params=pltpu.CompilerParams(
            dimension_semantics=("arbitrary", "arbitrary"), vmem_limit_bytes=VMEM_LIMIT_BYTES),
    )(x, mod, *states, *weights)


def _block_sum_matrix(ql):
    r = jnp.arange(CHUNK)[:, None]
    c = jnp.arange(CHUNK)[None, :]
    same = (r // ql) == (c // ql)
    return jnp.concatenate([same & (c <= r), same], axis=0).astype(BF16)


def _expand_matrix():
    k = jnp.arange(LANES)[:, None]
    c = jnp.arange(D_INNER)[None, :] // HEAD_DIM
    return ((k % N_HEADS == c) & (k < 3 * N_HEADS)).astype(BF16)


def kernel(x_prompt, x_sample, state_pool, state_conv, state_ssm, c_prompt, c_sample, w_ada, b_ada, norm_g, w_in, conv_w, conv_b, dt_bias, a_log, d_skip, ssm_norm_g, pool_w, pool_scale, w_pool_out, w_ssm_out, w_o, final_g):
    depth = w_ada.shape[0]
    assert depth == 1, "kernel is written for a single layer"
    nb, seq, _ = x_prompt.shape
    ns, dec_seq, _ = x_sample.shape
    l = 0

    def head_lanes(v):
        return jnp.pad(jnp.tile(v, (1, 3)), ((0, 0), (0, LANES - 3 * N_HEADS)))

    c_all = jnp.concatenate([c_prompt, c_sample], axis=0)
    mod_p, mod_s = pl.pallas_call(
        _adaln_kernel, grid=(3,),
        in_specs=[pl.BlockSpec((nb + ns, D_MODEL), lambda j: (0, 0)),
                  pl.BlockSpec((D_MODEL, D_MODEL), lambda j: (0, j)),
                  pl.BlockSpec((1, D_MODEL), lambda j: (0, j))],
        out_specs=(pl.BlockSpec((nb, 1, D_MODEL), lambda j: (0, 0, j)),
                   pl.BlockSpec((ns, 1, D_MODEL), lambda j: (0, 0, j))),
        out_shape=(jax.ShapeDtypeStruct((nb, 1, 3 * D_MODEL), F32),
                   jax.ShapeDtypeStruct((ns, 1, 3 * D_MODEL), F32)), name="adaln",
    )(c_all, w_ada[l], b_ada[l][None, :])

    e3 = _expand_matrix()
    w_main, w_tail = _repack_w_in(jnp.transpose(w_in[l]))

    def weights(ql):
        return [
            norm_g[l][None, :],
            w_main, w_tail,
            conv_w[l], conv_b[l][None, :],
            head_lanes(dt_bias[l][None, :]),
            head_lanes(a_log[l][None, :]),
            jnp.repeat(d_skip[l], HEAD_DIM)[None, :],
            ssm_norm_g[l][None, :],
            pool_w[l].astype(BF16), pool_scale[l][None, :],
            w_pool_out[l].astype(BF16), w_ssm_out[l].astype(BF16), w_o[l].astype(BF16),
            final_g[None, :],
            _block_sum_matrix(ql), e3,
        ]

    tile_rows = 2 * CHUNK
    y_p, pool_p, conv_p, ssm_p = _prompt_call(x_prompt, mod_p, weights(CHUNK), L=tile_rows)

    swap = lambda v: jnp.transpose(v, (1, 0, 2))
    states = (swap(state_pool[l]), swap(state_conv[l]),
              state_ssm[l].reshape(ns, D_INNER, D_STATE))
    y_s, pool_s, conv_s, ssm_s = _sample_call(
        x_sample, mod_s, states, weights(dec_seq), S=CHUNK // dec_seq, seq_per_inner=4)

    shp = (N_HEADS, HEAD_DIM, D_STATE)
    return (y_p, y_s, swap(pool_p)[None], swap(conv_p)[None], ssm_p.reshape((1, nb) + shp),
            swap(pool_s)[None], swap(conv_s)[None], ssm_s.reshape((1, ns) + shp))
```
